```python
import math
import jax, jax.numpy as jnp
from jax import lax
import numpy as np

D_MODEL = 1024
BATCH = 16
SEQ = 2048
DEPTH = 1

HEAD_DIM = 64
NSA_HEADS = 8
NSA_KV_HEADS = 2
NSA_GROUP = NSA_HEADS // NSA_KV_HEADS
SB_HEADS = 8
CMP_LEN = 32
CMP_STRIDE = 16
CMP_HIDDEN = 256
SEL_LEN = 64
SEL_TOPK = 4
WINDOW = 512
Q_BLOCK = 128
N_BUCKETS = 32
MAX_DISTANCE = 128
D_FF = 4 * D_MODEL
EPS = 1e-6
NEG = -1e30
FORCE_BONUS = 1e4

NSA_Q_W = NSA_HEADS * HEAD_DIM
NSA_KV_W = NSA_KV_HEADS * HEAD_DIM
SB_W = SB_HEADS * HEAD_DIM
SPLITS = (NSA_Q_W, 6 * NSA_KV_W, 3 * NSA_HEADS, 3 * SB_W, 2 * D_MODEL)
IN_COLS = sum(SPLITS)

kernel_name = 'hybrid_nsa_stickbreaking_block'


def rmsnorm(x, g):
    xf = x.astype(jnp.float32)
    y = xf * lax.rsqrt(jnp.mean(xf * xf, axis=-1, keepdims=True) + EPS)
    return y * g.astype(jnp.float32)


def t5_bucket(dist):
    n = jnp.maximum(dist, 0)
    max_exact = N_BUCKETS // 2
    nf = jnp.maximum(n, 1).astype(jnp.float32)
    large = max_exact + (jnp.log(nf / max_exact) / math.log(MAX_DISTANCE / max_exact)
                         * (N_BUCKETS - max_exact)).astype(jnp.int32)
    return jnp.where(n < max_exact, n, jnp.minimum(large, N_BUCKETS - 1))


def compress_blocks(kv, pe, w1, b1, w2):
    b, t, g, dh = kv.shape
    n_cmp = (t - CMP_LEN) // CMP_STRIDE + 1
    idx = np.arange(n_cmp)[:, None] * CMP_STRIDE + np.arange(CMP_LEN)[None, :]
    blocks = kv[:, idx] + pe[None, None, :, None, :]
    flat = blocks.transpose(0, 1, 3, 2, 4).reshape(b, n_cmp, g, CMP_LEN * dh)
    hid = jax.nn.gelu(flat @ w1 + b1)
    return hid @ w2


def nsa_attention(q, kv, gate_logits, rel_bias, cmp_k, cmp_v):
    b, t = q.shape[:2]
    g, hg, dh = NSA_KV_HEADS, NSA_GROUP, HEAD_DIM
    scale = dh ** -0.5
    qg = q.reshape(b, t, g, hg, dh)
    k_c, v_c, k_s, v_s, k_w, v_w = (kv[:, :, i] for i in range(6))
    tbl = rel_bias.reshape(N_BUCKETS, g, hg)
    pos = jnp.arange(t)

    kc = compress_blocks(k_c, *cmp_k)
    vc = compress_blocks(v_c, *cmp_v)
    n_cmp = kc.shape[1]
    c_start = np.arange(n_cmp) * CMP_STRIDE
    dist_c = pos[:, None] - jnp.asarray(c_start + CMP_LEN - 1)[None, :]
    valid_c = dist_c >= 0
    bias_c = jnp.moveaxis(tbl[t5_bucket(dist_c)], (2, 3), (0, 1))
    s_c = (jnp.einsum('btgjd,bcgd->bgjtc', qg, kc) * scale + bias_c).astype(jnp.float32)
    p_c = jax.nn.softmax(jnp.where(valid_c, s_c, NEG), axis=-1) * valid_c
    o_cmp = jnp.einsum('bgjtc,bcgd->btgjd', p_c, vc)

    n_sel = t // SEL_LEN
    s_start = np.arange(n_sel) * SEL_LEN
    overlap = ((c_start[:, None] < s_start[None, :] + SEL_LEN)
               & (c_start[:, None] + CMP_LEN > s_start[None, :])).astype(np.float32)
    imp = jnp.einsum('bgjtc,cn->bgtn', p_c, jnp.asarray(overlap))
    blk = jnp.arange(n_sel)[None, :]
    sel_ok = jnp.asarray(s_start)[None, :] <= pos[:, None]
    forced = (blk == (pos // SEL_LEN)[:, None]) | (blk == 0)
    score = jnp.where(forced, imp + FORCE_BONUS, jnp.where(sel_ok, imp, -FORCE_BONUS))
    _, sel_idx = lax.top_k(score, SEL_TOPK)

    ks_blk = k_s.reshape(b, n_sel, SEL_LEN, g, dh).transpose(0, 3, 1, 2, 4)
    vs_blk = v_s.reshape(b, n_sel, SEL_LEN, g, dh).transpose(0, 3, 1, 2, 4)
    gather = jax.vmap(jax.vmap(lambda arr, i: arr[i]))
    kw_pad = jnp.pad(k_w, ((0, 0), (WINDOW, 0), (0, 0), (0, 0)))
    vw_pad = jnp.pad(v_w, ((0, 0), (WINDOW, 0), (0, 0), (0, 0)))
    g_idx = jnp.arange(g)[None, :, None, None, None]
    offs = jnp.arange(SEL_LEN)

    def block(s0):
        tq = s0 + jnp.arange(Q_BLOCK)
        qb = lax.dynamic_slice_in_dim(qg, s0, Q_BLOCK, axis=1)
        ib = lax.dynamic_slice_in_dim(sel_idx, s0, Q_BLOCK, axis=2)
        kb = gather(ks_blk, ib)
        vb = gather(vs_blk, ib)
        kpos = ib[..., None] * SEL_LEN + offs
        dist_s = tq[:, None, None] - kpos
        valid_s = (dist_s >= 0)[:, :, None]
        bias_s = jnp.moveaxis(tbl[t5_bucket(dist_s), g_idx], -1, 2)
        s_s = (jnp.einsum('bqgjd,bgqkld->bgjqkl', qb, kb) * scale + bias_s).astype(jnp.float32)
        s_s = jnp.where(valid_s, s_s, NEG).reshape(b, g, hg, Q_BLOCK, SEL_TOPK * SEL_LEN)
        p_s = jax.nn.softmax(s_s, axis=-1)
        o_s = jnp.einsum('bgjqn,bgqnd->bqgjd', p_s,
                         vb.reshape(b, g, Q_BLOCK, SEL_TOPK * SEL_LEN, dh))
        kwb = lax.dynamic_slice_in_dim(kw_pad, s0, WINDOW + Q_BLOCK, axis=1)
        vwb = lax.dynamic_slice_in_dim(vw_pad, s0, WINDOW + Q_BLOCK, axis=1)
        kpos_w = s0 - WINDOW + jnp.arange(WINDOW + Q_BLOCK)
        dist_w = tq[:, None] - kpos_w[None, :]
        valid_w = (dist_w >= 0) & (dist_w < WINDOW) & (kpos_w >= 0)[None, :]
        bias_w = jnp.moveaxis(tbl[t5_bucket(dist_w)], (2, 3), (0, 1))
        s_w = (jnp.einsum('bqgjd,bkgd->bgjqk', qb, kwb) * scale + bias_w).astype(jnp.float32)
        p_w = jax.nn.softmax(jnp.where(valid_w, s_w, NEG), axis=-1)
        o_w = jnp.einsum('bgjqk,bkgd->bqgjd', p_w, vwb)
        return o_s, o_w

    o_sel, o_win = lax.map(block, jnp.arange(t // Q_BLOCK) * Q_BLOCK)
    o_sel = jnp.moveaxis(o_sel, 0, 1).reshape(b, t, g, hg, dh)
    o_win = jnp.moveaxis(o_win, 0, 1).reshape(b, t, g, hg, dh)
    gts = jax.nn.sigmoid(gate_logits).reshape(b, t, g, hg, 3)
    o = gts[..., 0:1] * o_cmp + gts[..., 1:2] * o_sel + gts[..., 2:3] * o_win
    return o.reshape(b, t, NSA_HEADS * dh)


def stick_breaking_attention(q, k, v):
    b, t, h, dh = q.shape
    scale = dh ** -0.5
    kpos = jnp.arange(t)

    def block(s0):
        qb = lax.dynamic_slice_in_dim(q, s0, Q_BLOCK, axis=1)
        z = jnp.einsum('bqhd,bkhd->bhqk', qb, k).astype(jnp.float32) * scale
        tq = s0 + jnp.arange(Q_BLOCK)
        causal = kpos[None, :] < tq[:, None]
        log_beta = jax.nn.log_sigmoid(z)
        log_keep = jnp.where(causal, log_beta - z, 0.0)
        later = lax.cumsum(log_keep, axis=3, reverse=True) - log_keep
        a = jnp.where(causal, jnp.exp(log_beta + later), 0.0)
        return jnp.einsum('bhqk,bkhd->bqhd', a, v)

    o = lax.map(block, jnp.arange(t // Q_BLOCK) * Q_BLOCK)
    return jnp.moveaxis(o, 0, 1).reshape(b, t, h * dh)


def setup_inputs(seed: int = 0) -> dict:
    key = jax.random.key(seed)
    ks = jax.random.split(key, 20)

    def nrm(k, shape, scale):
        return jax.random.normal(k, shape, jnp.float32) * scale

    flat_in = CMP_LEN * HEAD_DIM
    return {
        'x': nrm(ks[0], (BATCH, SEQ, D_MODEL), 1.0),
        'norm_attn': 1.0 + nrm(ks[1], (DEPTH, D_MODEL), 0.1),
        'w_in': nrm(ks[2], (DEPTH, D_MODEL, IN_COLS), D_MODEL ** -0.5),
        'rel_bias': nrm(ks[3], (N_BUCKETS, NSA_HEADS), 0.5),
        'cmp_k_pe': nrm(ks[4], (DEPTH, CMP_LEN, HEAD_DIM), 0.1),
        'cmp_k_w1': nrm(ks[5], (DEPTH, flat_in, CMP_HIDDEN), flat_in ** -0.5),
        'cmp_k_b1': nrm(ks[6], (DEPTH, CMP_HIDDEN), 0.02),
        'cmp_k_w2': nrm(ks[7], (DEPTH, CMP_HIDDEN, HEAD_DIM), CMP_HIDDEN ** -0.5),
        'cmp_v_pe': nrm(ks[8], (DEPTH, CMP_LEN, HEAD_DIM), 0.1),
        'cmp_v_w1': nrm(ks[9], (DEPTH, flat_in, CMP_HIDDEN), flat_in ** -0.5),
        'cmp_v_b1': nrm(ks[10], (DEPTH, CMP_HIDDEN), 0.02),
        'cmp_v_w2': nrm(ks[11], (DEPTH, CMP_HIDDEN, HEAD_DIM), CMP_HIDDEN ** -0.5),
        'w_up_nsa': nrm(ks[12], (DEPTH, NSA_Q_W, D_MODEL), NSA_Q_W ** -0.5),
        'w_up_sb': nrm(ks[13], (DEPTH, SB_W, D_MODEL), SB_W ** -0.5),
        'w_out': nrm(ks[14], (DEPTH, D_MODEL, D_MODEL), D_MODEL ** -0.5),
        'norm_mlp': 1.0 + nrm(ks[15], (DEPTH, D_MODEL), 0.1),
        'w_ff1': nrm(ks[16], (DEPTH, D_MODEL, D_FF), D_MODEL ** -0.5),
        'w_ff2': nrm(ks[17], (DEPTH, D_FF, D_MODEL), D_FF ** -0.5),
        'norm_final': 1.0 + nrm(ks[18], (D_MODEL,), 0.1),
    }


def reference(x, norm_attn, w_in, rel_bias, cmp_k_pe, cmp_k_w1, cmp_k_b1, cmp_k_w2,
              cmp_v_pe, cmp_v_w1, cmp_v_b1, cmp_v_w2, w_up_nsa, w_up_sb, w_out,
              norm_mlp, w_ff1, w_ff2, norm_final):
    b, t, _ = x.shape
    split_at = np.cumsum(SPLITS)[:-1].tolist()
    for layer in range(DEPTH):
        h = rmsnorm(x, norm_attn[layer])
        proj = h @ w_in[layer]
        q_n, kv_n, g_n, qkv_s, g_m = jnp.split(proj, split_at, axis=-1)
        o_nsa = nsa_attention(
            q_n.reshape(b, t, NSA_HEADS, HEAD_DIM),
            kv_n.reshape(b, t, 6, NSA_KV_HEADS, HEAD_DIM),
            g_n.reshape(b, t, NSA_HEADS, 3),
            rel_bias,
            (cmp_k_pe[layer], cmp_k_w1[layer], cmp_k_b1[layer], cmp_k_w2[layer]),
            (cmp_v_pe[layer], cmp_v_w1[layer], cmp_v_b1[layer], cmp_v_w2[layer]))
        qkv = qkv_s.reshape(b, t, 3, SB_HEADS, HEAD_DIM)
        o_sb = stick_breaking_attention(qkv[:, :, 0], qkv[:, :, 1], qkv[:, :, 2])
        merge = jax.nn.sigmoid(g_m.reshape(b, t, 2, D_MODEL))
        mixed = merge[:, :, 0] * (o_nsa @ w_up_nsa[layer]) + merge[:, :, 1] * (o_sb @ w_up_sb[layer])
        x = x + mixed @ w_out[layer]
        h = rmsnorm(x, norm_mlp[layer])
        x = x + jnp.square(jax.nn.relu(h @ w_ff1[layer])) @ w_ff2[layer]
    return rmsnorm(x, norm_final)
```

```python
import functools
import math

import numpy as np
import jax
import jax.numpy as jnp
from jax import lax
from jax.experimental import pallas as pl
from jax.experimental.pallas import tpu as pltpu

HEAD_DIM = 64
NSA_HEADS = 8
NSA_KV_HEADS = 2
NSA_GROUP = NSA_HEADS // NSA_KV_HEADS
SB_HEADS = 8
CMP_LEN = 32
CMP_STRIDE = 16
CMP_HIDDEN = 256
SEL_LEN = 64
SEL_TOPK = 4
WINDOW = 512
N_BUCKETS = 32
MAX_DISTANCE = 128
EPS = 1e-6
NEG = -1e30
FORCE_BONUS = 1e4

LANES = 128
TQ = 128
VMEM_LIMIT = 56 * 1024 * 1024

F32 = jnp.float32
BF16 = jnp.bfloat16

_C_QN = 0
_C_KS = 4
_C_VS = 6
_C_KW = 8
_C_VW = 10
_C_GATE = 12
_C_SBQ = 24
_C_SBK = 28
_C_SBV = 32
_C_KVC = 36
_N_CHUNKS = 38


def _dot(a, b):
    return jnp.dot(a, b, preferred_element_type=F32)


def _dot_nt(a, b):
    return lax.dot_general(a, b, (((1,), (1,)), ((), ())), preferred_element_type=F32)


def _rms(x, g):
    return x * lax.rsqrt(jnp.mean(x * x, axis=-1, keepdims=True) + EPS) * g


def _inproj_kernel(x_ref, g_ref, w_ref, o_ref):
    h = _rms(x_ref[...], g_ref[...]).astype(BF16)
    n = o_ref.shape[1]
    for c in range(0, n, 256):
        o_ref[:, c:c + 256] = _dot(h, w_ref[:, c:c + 256]).astype(BF16)


def _inproj(x2, g, w):
    m, d = x2.shape
    n = w.shape[1]
    tm = 512
    return pl.pallas_call(
        _inproj_kernel,
        grid=(m // tm,),
        in_specs=[pl.BlockSpec((tm, d), lambda i: (i, 0)),
                  pl.BlockSpec((1, d), lambda i: (0, 0)),
                  pl.BlockSpec((d, n), lambda i: (0, 0))],
        out_specs=pl.BlockSpec((tm, n), lambda i: (i, 0)),
        out_shape=jax.ShapeDtypeStruct((m, n), BF16),
        compiler_params=pltpu.CompilerParams(dimension_semantics=("parallel",),
                                             vmem_limit_bytes=VMEM_LIMIT),
        name="inproj",
    )(x2, g, w)


def _gelu_tanh(x):
    return 0.5 * x * (1.0 + jnp.tanh(math.sqrt(2.0 / math.pi) * (x + 0.044715 * (x * x * x))))


def _compress_kernel(x_ref, pea_ref, peb_ref, w1a_ref, w1b_ref, b1_ref, w2_ref, o_ref):
    for s in range(4):
        r = s // 2
        x = x_ref[0, s].astype(F32)
        a = _dot((x + pea_ref[r]).astype(BF16), w1a_ref[r])
        b = _dot((x + peb_ref[r]).astype(BF16), w1b_ref[r])
        hid = a + pltpu.roll(b, shift=b.shape[0] - 1, axis=0) + b1_ref[r]
        o_ref[0, s] = _dot(_gelu_tanh(hid).astype(BF16), w2_ref[r]).astype(BF16)


def _compress(xc, pea, peb, w1a, w1b, b1, w2d):
    b = xc.shape[0]
    nchunk, width = xc.shape[2], xc.shape[3]
    full = lambda shape: pl.BlockSpec(shape, lambda i: (0,) * len(shape))
    return pl.pallas_call(
        _compress_kernel,
        grid=(b,),
        in_specs=[pl.BlockSpec((1, 4, nchunk, width), lambda i: (i, 0, 0, 0)),
                  full(pea.shape), full(peb.shape), full(w1a.shape), full(w1b.shape),
                  full(b1.shape), full(w2d.shape)],
        out_specs=pl.BlockSpec((1, 4, nchunk, LANES), lambda i: (i, 0, 0, 0)),
        out_shape=jax.ShapeDtypeStruct((b, 4, nchunk, LANES), BF16),
        compiler_params=pltpu.CompilerParams(dimension_semantics=("parallel",),
                                             vmem_limit_bytes=VMEM_LIMIT),
        name="compress",
    )(xc, pea, peb, w1a, w1b, b1, w2d)


def _nsa_kernel(q_ref, ks_ref, vs_ref, kw_ref, vw_ref, g0_ref, g1_ref, g2_ref, kc_ref, vc_ref,
                bc_ref, tz_ref, ov_ref, et_ref, o_ref, acc_ref, m_ref, l_ref):
    i = pl.program_id(2)
    hg = NSA_GROUP
    lane = lax.broadcasted_iota(jnp.int32, (TQ, LANES), 1)
    row = lax.broadcasted_iota(jnp.int32, (TQ, LANES), 0)
    lo = lane < HEAD_DIM
    t_abs = i * TQ + row

    q = q_ref[0].astype(F32)
    qa, qb = q[:, :LANES], q[:, LANES:]
    qs = jnp.concatenate([jnp.where(lo, qa, 0.0), jnp.where(lo, 0.0, qa),
                          jnp.where(lo, qb, 0.0), jnp.where(lo, 0.0, qb)], axis=0).astype(BF16)

    def hrows(h):
        return slice(h * TQ, (h + 1) * TQ)

    s = _dot_nt(qs, kc_ref[0, 0])
    valid_c = (t_abs >= CMP_STRIDE * lane + (CMP_LEN - 1)) & (lane < LANES - 1)
    ps = []
    for h in range(hg):
        sh = jnp.where(valid_c, s[hrows(h)] + bc_ref[h, 0], NEG)
        e = jnp.exp(sh - jnp.max(sh, axis=1, keepdims=True))
        ps.append(jnp.where(valid_c, e, 0.0) / jnp.sum(e, axis=1, keepdims=True))
    o_cmp = _dot(jnp.concatenate(ps, axis=0).astype(BF16), vc_ref[0, 0])

    psum = (ps[0] + ps[1]) + (ps[2] + ps[3])
    p_hi = psum.astype(BF16)
    p_lo = (psum - p_hi.astype(F32)).astype(BF16)
    imp = _dot(p_hi, ov_ref[...]) + _dot(p_lo, ov_ref[...])
    n_sel = ks_ref.shape[1] // SEL_LEN
    forced = (lane == t_abs // SEL_LEN) | (lane == 0)
    sel_ok = lane * SEL_LEN <= t_abs
    score = jnp.where(forced, imp + FORCE_BONUS, jnp.where(sel_ok, imp, -FORCE_BONUS))
    score = jnp.where(lane < n_sel, score, -jnp.inf)
    lane_f = lane.astype(F32)
    selm = jnp.zeros((TQ, LANES), F32)
    for _ in range(SEL_TOPK):
        mx = jnp.max(score, axis=1, keepdims=True)
        idx = jnp.min(jnp.where(score == mx, lane_f, float(LANES)), axis=1, keepdims=True)
        hit = lane_f == idx
        selm = jnp.where(hit, 1.0, selm)
        score = jnp.where(hit, -jnp.inf, score)
    selb = selm.astype(BF16)

    def reset():
        m_ref[...] = jnp.full(m_ref.shape, NEG, F32)
        l_ref[...] = jnp.zeros(l_ref.shape, F32)
        acc_ref[...] = jnp.zeros(acc_ref.shape, F32)

    def attend(kt, vt, rr, ok):
        s = _dot_nt(qs, kt)
        ps, alphas = [], []
        for h in range(hg):
            sh = jnp.where(ok, s[hrows(h)] + tz_ref[h, rr], NEG)
            m_old = m_ref[hrows(h)]
            m_new = jnp.maximum(m_old, jnp.max(sh, axis=1, keepdims=True))
            alpha = jnp.exp(m_old - m_new)
            p = jnp.exp(sh - m_new)
            l_ref[hrows(h)] = alpha * l_ref[hrows(h)] + jnp.sum(p, axis=1, keepdims=True)
            m_ref[hrows(h)] = m_new
            ps.append(p)
            alphas.append(jnp.broadcast_to(alpha, (TQ, LANES)))
        pv = _dot(jnp.concatenate(ps, axis=0).astype(BF16), vt)
        acc_ref[...] = jnp.concatenate(alphas, axis=0) * acc_ref[...] + pv

    def finish():
        return acc_ref[...] / l_ref[...]

    reset()

    def sel_body(j, carry):
        off = pl.multiple_of(j * TQ, TQ)
        maskf = _dot_nt(selb, et_ref[pl.ds(off, TQ), :])
        ok = (maskf > 0.5) & (t_abs >= off + lane)
        attend(ks_ref[0, pl.ds(off, TQ), :], vs_ref[0, pl.ds(off, TQ), :], jnp.minimum(i - j, 2), ok)
        return carry

    lax.fori_loop(0, i + 1, sel_body, 0)
    o_sel = finish()

    reset()

    def win_body(r, carry):
        off = pl.multiple_of((i - r) * TQ, TQ)
        dist = r * TQ + row - lane
        ok = (dist >= 0) & (dist < WINDOW)
        attend(kw_ref[0, pl.ds(off, TQ), :], vw_ref[0, pl.ds(off, TQ), :], jnp.minimum(r, 2), ok)
        return carry

    lax.fori_loop(0, jnp.minimum(i, WINDOW // TQ) + 1, win_body, 0)
    o_win = finish()

    def sig(ref):
        return 1.0 / (1.0 + jnp.exp(-ref[0].astype(F32)))

    gates = (sig(g0_ref), sig(g1_ref), sig(g2_ref))
    for c in range(2):
        out = jnp.zeros((TQ, LANES), F32)
        for gt, ob in zip(gates, (o_cmp, o_sel, o_win)):
            out = out + gt[:, c * LANES:(c + 1) * LANES] * jnp.where(lo, ob[hrows(2 * c)], ob[hrows(2 * c + 1)])
        o_ref[0, :, c * LANES:(c + 1) * LANES] = out.astype(BF16)


def _nsa(p3, kvc, bias_c, tz, ov, et):
    b, t, _ = p3.shape
    g = NSA_KV_HEADS
    nq = t // TQ
    seq = lambda col: pl.BlockSpec((1, t, LANES), lambda bi, gi, i, col=col: (bi, 0, col + gi))
    gate = lambda br: pl.BlockSpec((1, TQ, 2 * LANES),
                                   lambda bi, gi, i, br=br: (bi, i, (_C_GATE + 4 * br) // 2 + gi))
    return pl.pallas_call(
        _nsa_kernel,
        grid=(b, g, nq),
        in_specs=[pl.BlockSpec((1, TQ, 2 * LANES), lambda bi, gi, i: (bi, i, _C_QN // 2 + gi)),
                  seq(_C_KS), seq(_C_VS), seq(_C_KW), seq(_C_VW),
                  gate(0), gate(1), gate(2),
                  pl.BlockSpec((1, 1, LANES, LANES), lambda bi, gi, i: (bi, gi, 0, 0)),
                  pl.BlockSpec((1, 1, LANES, LANES), lambda bi, gi, i: (bi, 2 + gi, 0, 0)),
                  pl.BlockSpec((NSA_GROUP, 1, TQ, LANES), lambda bi, gi, i: (gi, i, 0, 0)),
                  pl.BlockSpec((NSA_GROUP, 3, TQ, LANES), lambda bi, gi, i: (gi, 0, 0, 0)),
                  pl.BlockSpec((LANES, LANES), lambda bi, gi, i: (0, 0)),
                  pl.BlockSpec((t, LANES), lambda bi, gi, i: (0, 0))],
        out_specs=pl.BlockSpec((1, TQ, 2 * LANES), lambda bi, gi, i: (bi, i, gi)),
        out_shape=jax.ShapeDtypeStruct((b, t, NSA_HEADS * HEAD_DIM), BF16),
        scratch_shapes=[pltpu.VMEM((NSA_GROUP * TQ, LANES), F32),
                        pltpu.VMEM((NSA_GROUP * TQ, 1), F32),
                        pltpu.VMEM((NSA_GROUP * TQ, 1), F32)],
        compiler_params=pltpu.CompilerParams(dimension_semantics=("parallel", "parallel", "arbitrary"),
                                             vmem_limit_bytes=VMEM_LIMIT),
        name="nsa",
    )(p3, p3, p3, p3, p3, p3, p3, p3, kvc, kvc, bias_c, tz, ov, et)


def _sb_kernel(q_ref, k_ref, v_ref, u_ref, o_ref, acc_ref, carry_ref):
    i = pl.program_id(2)
    lane = lax.broadcasted_iota(jnp.int32, (TQ, LANES), 1)
    row = lax.broadcasted_iota(jnp.int32, (TQ, LANES), 0)
    lo = lane < HEAD_DIM
    q = q_ref[0].astype(F32)
    qs = jnp.concatenate([jnp.where(lo, q, 0.0), jnp.where(lo, 0.0, q)], axis=0).astype(BF16)
    lane2 = lax.broadcasted_iota(jnp.int32, (2 * TQ, LANES), 1)
    row2 = lax.broadcasted_iota(jnp.int32, (2 * TQ, LANES), 0) & (TQ - 1)
    causal = lane2 < row2

    acc_ref[...] = jnp.zeros(acc_ref.shape, F32)
    carry_ref[...] = jnp.zeros(carry_ref.shape, F32)

    def tile(j, diag):
        off = pl.multiple_of(j * TQ, TQ)
        z = _dot_nt(qs, k_ref[0, pl.ds(off, TQ), :])
        log_beta = jnp.minimum(z, 0.0) - jnp.log1p(jnp.exp(-jnp.abs(z)))
        log_keep = log_beta - z
        if diag:
            log_keep = jnp.where(causal, log_keep, 0.0)
        k_hi = log_keep.astype(BF16)
        k_lo = (log_keep - k_hi.astype(F32)).astype(BF16)
        later = _dot(k_hi, u_ref[...]) + _dot(k_lo, u_ref[...])
        a = jnp.exp(log_beta + later + carry_ref[...])
        if diag:
            a = jnp.where(causal, a, 0.0)
        acc_ref[...] += _dot(a.astype(BF16), v_ref[0, pl.ds(off, TQ), :])
        carry_ref[...] += jnp.sum(log_keep, axis=1, keepdims=True)

    tile(i, True)

    def body(n, c):
        tile(i - 1 - n, False)
        return c

    lax.fori_loop(0, i, body, 0)
    o = acc_ref[...]
    o_ref[0] = jnp.where(lo, o[:TQ], o[TQ:]).astype(BF16)


def _sb(p3, u):
    b, t, _ = p3.shape
    npair = SB_HEADS // 2
    nq = t // TQ
    return pl.pallas_call(
        _sb_kernel,
        grid=(b, npair, nq),
        in_specs=[pl.BlockSpec((1, TQ, LANES), lambda bi, c, i: (bi, i, _C_SBQ + c)),
                  pl.BlockSpec((1, t, LANES), lambda bi, c, i: (bi, 0, _C_SBK + c)),
                  pl.BlockSpec((1, t, LANES), lambda bi, c, i: (bi, 0, _C_SBV + c)),
                  pl.BlockSpec((TQ, TQ), lambda bi, c, i: (0, 0))],
        out_specs=pl.BlockSpec((1, TQ, LANES), lambda bi, c, i: (bi, i, c)),
        out_shape=jax.ShapeDtypeStruct((b, t, SB_HEADS * HEAD_DIM), BF16),
        scratch_shapes=[pltpu.VMEM((2 * TQ, LANES), F32), pltpu.VMEM((2 * TQ, 1), F32)],
        compiler_params=pltpu.CompilerParams(dimension_semantics=("parallel", "parallel", "arbitrary"),
                                             vmem_limit_bytes=VMEM_LIMIT),
        name="sb",
    )(p3, p3, p3, u)


def _tail_kernel(x_ref, on_ref, os_ref, ga_ref, wgm_ref, wun_ref, wus_ref, wo_ref, gm_ref,
                 w1_ref, w2_ref, gf_ref, o_ref, *, last_layer):
    x = x_ref[...]
    d = x.shape[1]
    h = _rms(x, ga_ref[...]).astype(BF16)
    gm = 1.0 / (1.0 + jnp.exp(-_dot(h, wgm_ref[...])))
    mixed = gm[:, :d] * _dot(on_ref[...], wun_ref[...]) + gm[:, d:] * _dot(os_ref[...], wus_ref[...])
    x1 = x + _dot(mixed.astype(BF16), wo_ref[...])
    h2 = _rms(x1, gm_ref[...]).astype(BF16)
    dff = w1_ref.shape[1]
    fc = 512
    acc = jnp.zeros_like(x1)
    for f in range(0, dff, fc):
        u = jnp.maximum(_dot(h2, w1_ref[:, f:f + fc]), 0.0)
        acc = acc + _dot((u * u).astype(BF16), w2_ref[f:f + fc, :])
    x2 = x1 + acc
    o_ref[...] = _rms(x2, gf_ref[...]) if last_layer else x2


def _tail(x2, o_nsa, o_sb, g_attn, wgm, wun, wus, wo, g_mlp, w1, w2, g_final, last_layer):
    m, d = x2.shape
    tm = 256
    tok = lambda w: pl.BlockSpec((tm, w), lambda i: (i, 0))
    const = lambda a: pl.BlockSpec(a.shape, lambda i: (0, 0), pipeline_mode=pl.Buffered(1))
    return pl.pallas_call(
        functools.partial(_tail_kernel, last_layer=last_layer),
        grid=(m // tm,),
        in_specs=[tok(d), tok(o_nsa.shape[1]), tok(o_sb.shape[1]), const(g_attn), const(wgm),
                  const(wun), const(wus), const(wo), const(g_mlp), const(w1), const(w2), const(g_final)],
        out_specs=tok(d),
        out_shape=jax.ShapeDtypeStruct((m, d), F32),
        compiler_params=pltpu.CompilerParams(dimension_semantics=("parallel",),
                                             vmem_limit_bytes=VMEM_LIMIT),
        name="tail",
    )(x2, o_nsa, o_sb, g_attn, wgm, wun, wus, wo, g_mlp, w1, w2, g_final)


def _t5_bucket_np(dist):
    n = np.maximum(dist, 0)
    max_exact = N_BUCKETS // 2
    nf = np.maximum(n, 1).astype(np.float32)
    large = max_exact + (np.log(nf / np.float32(max_exact)) / np.float32(math.log(MAX_DISTANCE / max_exact))
                         * np.float32(N_BUCKETS - max_exact)).astype(np.int32)
    return np.where(n < max_exact, n, np.minimum(large, N_BUCKETS - 1)).astype(np.int32)


def _bias_tables(rel_bias, t):
    qi = np.arange(TQ)[:, None]
    kj = np.arange(LANES)[None, :]
    idx = np.stack([_t5_bucket_np(r * TQ + qi - kj) for r in range(3)])
    tz = jnp.transpose(rel_bias[idx], (3, 0, 1, 2))
    near = -(-(MAX_DISTANCE + CMP_LEN - 1) // CMP_STRIDE) - 1
    dist0 = qi - CMP_STRIDE * (kj - near) - (CMP_LEN - 1)
    idx0 = np.where(dist0 >= 0, _t5_bucket_np(dist0), N_BUCKETS - 1)
    base = jnp.transpose(rel_bias[idx0], (2, 0, 1))
    per_tile = TQ // CMP_STRIDE
    bias_c = jnp.stack([jnp.roll(base, per_tile * i - near, axis=2) for i in range(t // TQ)], axis=1)
    return tz, bias_c


def _const_tables(t):
    c_start = np.arange(LANES) * CMP_STRIDE
    s_start = np.arange(LANES) * SEL_LEN
    n_cmp = (t - CMP_LEN) // CMP_STRIDE + 1
    n_sel = t // SEL_LEN
    ov = ((c_start[:, None] < s_start[None, :] + SEL_LEN) & (c_start[:, None] + CMP_LEN > s_start[None, :])
          & (np.arange(LANES)[:, None] < n_cmp) & (np.arange(LANES)[None, :] < n_sel))
    et = (np.arange(t)[:, None] // SEL_LEN) == np.arange(LANES)[None, :]
    u = np.arange(TQ)[:, None] > np.arange(TQ)[None, :]
    return (jnp.asarray(ov, BF16), jnp.asarray(et, BF16), jnp.asarray(u, BF16))


def _arrange_w_in(w):
    d = w.shape[0]
    nq, nkv, ng = NSA_HEADS * HEAD_DIM, 6 * NSA_KV_HEADS * HEAD_DIM, 3 * NSA_HEADS
    nsb = SB_HEADS * HEAD_DIM
    scale = HEAD_DIM ** -0.5
    q = w[:, :nq] * scale
    kv = w[:, nq:nq + nkv]
    gl = w[:, nq + nkv:nq + nkv + ng]
    sb = w[:, nq + nkv + ng:nq + nkv + ng + 3 * nsb]
    gm = w[:, nq + nkv + ng + 3 * nsb:]
    kvw = NSA_KV_HEADS * HEAD_DIM
    dup = []
    for i6 in (2, 3, 4, 5):
        for g in range(NSA_KV_HEADS):
            sl = kv[:, i6 * kvw + g * HEAD_DIM:i6 * kvw + (g + 1) * HEAD_DIM]
            dup += [sl, sl]
    gates = jnp.transpose(gl.reshape(d, NSA_HEADS, 3), (0, 2, 1))
    gates = jnp.broadcast_to(gates[..., None], (d, 3, NSA_HEADS, HEAD_DIM)).reshape(d, 3 * nq)
    cols = [q] + dup + [gates, sb[:, :nsb] * scale, sb[:, nsb:], kv[:, :2 * kvw]]
    return jnp.concatenate(cols, axis=1).astype(BF16), gm.astype(BF16)


def kernel(x, norm_attn, w_in, rel_bias, cmp_k_pe, cmp_k_w1, cmp_k_b1, cmp_k_w2, cmp_v_pe, cmp_v_w1, cmp_v_b1, cmp_v_w2, w_up_nsa, w_up_sb, w_out, norm_mlp, w_ff1, w_ff2, norm_final):
    b, t, d = x.shape
    depth = w_in.shape[0]
    assert t % TQ == 0 and t // SEL_LEN <= N_BUCKETS and (t - CMP_LEN) // CMP_STRIDE + 1 < LANES
    tz, bias_c = _bias_tables(rel_bias, t)
    ov, et, u = _const_tables(t)
    half = CMP_LEN // 2 * HEAD_DIM
    xf = x.reshape(b * t, d)
    for layer in range(depth):
        w_arr, w_gm = _arrange_w_in(w_in[layer])
        assert w_arr.shape[1] == _N_CHUNKS * LANES
        proj = _inproj(xf, norm_attn[layer][None, :], w_arr).reshape(b, t, _N_CHUNKS * LANES)

        kvc = proj[:, :, _C_KVC * LANES:].reshape(b, t, 4, HEAD_DIM)
        kvc = jnp.transpose(kvc, (0, 2, 1, 3)).reshape(b, 4, t // CMP_STRIDE, CMP_STRIDE * HEAD_DIM)
        pe = jnp.stack([cmp_k_pe[layer], cmp_v_pe[layer]]).reshape(2, 1, CMP_LEN * HEAD_DIM)
        w1 = jnp.stack([cmp_k_w1[layer], cmp_v_w1[layer]]).astype(BF16)
        b1 = jnp.stack([cmp_k_b1[layer], cmp_v_b1[layer]])[:, None, :]
        w2 = jnp.stack([cmp_k_w2[layer], cmp_v_w2[layer]])
        w2d = jnp.concatenate([w2, w2], axis=2).astype(BF16)
        kvc = _compress(kvc, pe[:, :, :half], pe[:, :, half:], w1[:, :half], w1[:, half:], b1, w2d)

        o_nsa = _nsa(proj, kvc, bias_c, tz, ov, et)
        o_sb = _sb(proj, u)
        xf = _tail(xf, o_nsa.reshape(b * t, -1), o_sb.reshape(b * t, -1), norm_attn[layer][None, :], w_gm,
                   w_up_nsa[layer].astype(BF16), w_up_sb[layer].astype(BF16), w_out[layer].astype(BF16),
                   norm_mlp[layer][None, :], w_ff1[layer].astype(BF16), w_ff2[layer].astype(BF16),
                   norm_final[None, :], layer == depth - 1)
    return xf.reshape(b, t, d)
```

```python
import functools
import math

import numpy as np
import jax
import jax.numpy as jnp
from jax import lax
from jax.experimental import pallas as pl
from jax.experimental.pallas import tpu as pltpu

HEAD_DIM = 64
NSA_HEADS = 8
NSA_KV_HEADS = 2
NSA_GROUP = NSA_HEADS // NSA_KV_HEADS
SB_HEADS = 8
CMP_LEN = 32
CMP_STRIDE = 16
CMP_HIDDEN = 256
SEL_LEN = 64
SEL_TOPK = 4
WINDOW = 512
N_BUCKETS = 32
MAX_DISTANCE = 128
EPS = 1e-6
NEG = -1e30
FORCE_BONUS = 1e4

LANES = 128
TQ = 128
TS = 256
LOG2E = math.log2(math.e)
VMEM_LIMIT = 56 * 1024 * 1024

F32 = jnp.float32
BF16 = jnp.bfloat16

_C_QN = 0
_C_KS = 4
_C_VS = 6
_C_KW = 8
_C_VW = 10
_C_GATE = 12
_C_SBQ = 24
_C_SBK = 28
_C_SBV = 32
_C_KVC = 36
_N_CHUNKS = 38


def _dot(a, b):
    return jnp.dot(a, b, preferred_element_type=F32)


def _dot_nt(a, b):
    return lax.dot_general(a, b, (((1,), (1,)), ((), ())), preferred_element_type=F32)


def _rms(x, g):
    return x * lax.rsqrt(jnp.mean(x * x, axis=-1, keepdims=True) + EPS) * g


def _inproj_kernel(x_ref, g_ref, w_ref, o_ref):
    h = _rms(x_ref[...], g_ref[...]).astype(BF16)
    n = o_ref.shape[1]
    for c in range(0, n, 256):
        o_ref[:, c:c + 256] = _dot(h, w_ref[:, c:c + 256]).astype(BF16)


def _inproj(x2, g, w):
    m, d = x2.shape
    n = w.shape[1]
    tm = 512
    return pl.pallas_call(
        _inproj_kernel,
        grid=(m // tm,),
        in_specs=[pl.BlockSpec((tm, d), lambda i: (i, 0)),
                  pl.BlockSpec((1, d), lambda i: (0, 0)),
                  pl.BlockSpec((d, n), lambda i: (0, 0))],
        out_specs=pl.BlockSpec((tm, n), lambda i: (i, 0)),
        out_shape=jax.ShapeDtypeStruct((m, n), BF16),
        compiler_params=pltpu.CompilerParams(dimension_semantics=("parallel",),
                                             vmem_limit_bytes=VMEM_LIMIT),
        name="inproj",
    )(x2, g, w)


def _gelu_tanh(x):
    return 0.5 * x * (1.0 + jnp.tanh(math.sqrt(2.0 / math.pi) * (x + 0.044715 * (x * x * x))))


def _compress_kernel(x_ref, pea_ref, peb_ref, w1a_ref, w1b_ref, b1_ref, w2_ref, o_ref):
    for s in range(4):
        r = s // 2
        x = x_ref[0, s].astype(F32)
        a = _dot((x + pea_ref[r]).astype(BF16), w1a_ref[r])
        b = _dot((x + peb_ref[r]).astype(BF16), w1b_ref[r])
        hid = a + pltpu.roll(b, shift=b.shape[0] - 1, axis=0) + b1_ref[r]
        o_ref[0, s] = _dot(_gelu_tanh(hid).astype(BF16), w2_ref[r]).astype(BF16)


def _compress(xc, pea, peb, w1a, w1b, b1, w2d):
    b = xc.shape[0]
    nchunk, width = xc.shape[2], xc.shape[3]
    full = lambda shape: pl.BlockSpec(shape, lambda i: (0,) * len(shape))
    return pl.pallas_call(
        _compress_kernel,
        grid=(b,),
        in_specs=[pl.BlockSpec((1, 4, nchunk, width), lambda i: (i, 0, 0, 0)),
                  full(pea.shape), full(peb.shape), full(w1a.shape), full(w1b.shape),
                  full(b1.shape), full(w2d.shape)],
        out_specs=pl.BlockSpec((1, 4, nchunk, LANES), lambda i: (i, 0, 0, 0)),
        out_shape=jax.ShapeDtypeStruct((b, 4, nchunk, LANES), BF16),
        compiler_params=pltpu.CompilerParams(dimension_semantics=("parallel",),
                                             vmem_limit_bytes=VMEM_LIMIT),
        name="compress",
    )(xc, pea, peb, w1a, w1b, b1, w2d)


def _nsa_kernel(q_ref, ks_ref, vs_ref, kw_ref, vw_ref, g0_ref, g1_ref, g2_ref, kc_ref, vc_ref,
                bc_ref, tz_ref, ov_ref, et_ref, o_ref, acc_ref, m_ref, l_ref):
    i = pl.program_id(2)
    hg = NSA_GROUP
    lane = lax.broadcasted_iota(jnp.int32, (TQ, LANES), 1)
    row = lax.broadcasted_iota(jnp.int32, (TQ, LANES), 0)
    lo = lane < HEAD_DIM
    t_abs = i * TQ + row

    q = q_ref[0].astype(F32)
    qa, qb = q[:, :LANES], q[:, LANES:]
    qs = jnp.concatenate([jnp.where(lo, qa, 0.0), jnp.where(lo, 0.0, qa),
                          jnp.where(lo, qb, 0.0), jnp.where(lo, 0.0, qb)], axis=0).astype(BF16)

    def hrows(h):
        return slice(h * TQ, (h + 1) * TQ)

    s = _dot_nt(qs, kc_ref[0, 0])
    valid_c = (t_abs >= CMP_STRIDE * lane + (CMP_LEN - 1)) & (lane < LANES - 1)
    ps = []
    for h in range(hg):
        sh = jnp.where(valid_c, s[hrows(h)] + bc_ref[h, 0], NEG)
        e = jnp.exp(sh - jnp.max(sh, axis=1, keepdims=True))
        ps.append(jnp.where(valid_c, e, 0.0) / jnp.sum(e, axis=1, keepdims=True))
    o_cmp = _dot(jnp.concatenate(ps, axis=0).astype(BF16), vc_ref[0, 0])

    psum = (ps[0] + ps[1]) + (ps[2] + ps[3])
    p_hi = psum.astype(BF16)
    p_lo = (psum - p_hi.astype(F32)).astype(BF16)
    imp = _dot(p_hi, ov_ref[...]) + _dot(p_lo, ov_ref[...])
    n_sel = ks_ref.shape[1] // SEL_LEN
    forced = (lane == t_abs // SEL_LEN) | (lane == 0)
    sel_ok = lane * SEL_LEN <= t_abs
    score = jnp.where(forced, imp + FORCE_BONUS, jnp.where(sel_ok, imp, -FORCE_BONUS))
    score = jnp.where(lane < n_sel, score, -jnp.inf)
    lane_f = lane.astype(F32)
    selm = jnp.zeros((TQ, LANES), F32)
    for _ in range(SEL_TOPK):
        mx = jnp.max(score, axis=1, keepdims=True)
        idx = jnp.min(jnp.where(score == mx, lane_f, float(LANES)), axis=1, keepdims=True)
        hit = lane_f == idx
        selm = jnp.where(hit, 1.0, selm)
        score = jnp.where(hit, -jnp.inf, score)
    selb = selm.astype(BF16)

    def reset():
        m_ref[...] = jnp.full(m_ref.shape, NEG, F32)
        l_ref[...] = jnp.zeros(l_ref.shape, F32)
        acc_ref[...] = jnp.zeros(acc_ref.shape, F32)

    def attend(kt, vt, rr, ok):
        s = _dot_nt(qs, kt)
        ps, alphas = [], []
        for h in range(hg):
            sh = jnp.where(ok, s[hrows(h)] + tz_ref[h, rr], NEG)
            m_old = m_ref[hrows(h)]
            m_new = jnp.maximum(m_old, jnp.max(sh, axis=1, keepdims=True))
            alpha = jnp.exp(m_old - m_new)
            p = jnp.exp(sh - m_new)
            l_ref[hrows(h)] = alpha * l_ref[hrows(h)] + jnp.sum(p, axis=1, keepdims=True)
            m_ref[hrows(h)] = m_new
            ps.append(p)
            alphas.append(jnp.broadcast_to(alpha, (TQ, LANES)))
        pv = _dot(jnp.concatenate(ps, axis=0).astype(BF16), vt)
        acc_ref[...] = jnp.concatenate(alphas, axis=0) * acc_ref[...] + pv

    def finish():
        return acc_ref[...] / l_ref[...]

    reset()

    def sel_body(j, carry):
        off = pl.multiple_of(j * TQ, TQ)
        maskf = _dot_nt(selb, et_ref[pl.ds(off, TQ), :])
        ok = (maskf > 0.5) & (t_abs >= off + lane)
        attend(ks_ref[0, pl.ds(off, TQ), :], vs_ref[0, pl.ds(off, TQ), :], jnp.minimum(i - j, 2), ok)
        return carry

    lax.fori_loop(0, i + 1, sel_body, 0)
    o_sel = finish()

    reset()

    def win_body(r, carry):
        off = pl.multiple_of((i - r) * TQ, TQ)
        dist = r * TQ + row - lane
        ok = (dist >= 0) & (dist < WINDOW)
        attend(kw_ref[0, pl.ds(off, TQ), :], vw_ref[0, pl.ds(off, TQ), :], jnp.minimum(r, 2), ok)
        return carry

    lax.fori_loop(0, jnp.minimum(i, WINDOW // TQ) + 1, win_body, 0)
    o_win = finish()

    def sig(ref):
        return 1.0 / (1.0 + jnp.exp(-ref[0].astype(F32)))

    gates = (sig(g0_ref), sig(g1_ref), sig(g2_ref))
    for c in range(2):
        out = jnp.zeros((TQ, LANES), F32)
        for gt, ob in zip(gates, (o_cmp, o_sel, o_win)):
            out = out + gt[:, c * LANES:(c + 1) * LANES] * jnp.where(lo, ob[hrows(2 * c)], ob[hrows(2 * c + 1)])
        o_ref[0, :, c * LANES:(c + 1) * LANES] = out.astype(BF16)


def _nsa(p3, kvc, bias_c, tz, ov, et):
    b, t, _ = p3.shape
    g = NSA_KV_HEADS
    nq = t // TQ
    seq = lambda col: pl.BlockSpec((1, t, LANES), lambda bi, gi, i, col=col: (bi, 0, col + gi))
    gate = lambda br: pl.BlockSpec((1, TQ, 2 * LANES),
                                   lambda bi, gi, i, br=br: (bi, i, (_C_GATE + 4 * br) // 2 + gi))
    return pl.pallas_call(
        _nsa_kernel,
        grid=(b, g, nq),
        in_specs=[pl.BlockSpec((1, TQ, 2 * LANES), lambda bi, gi, i: (bi, i, _C_QN // 2 + gi)),
                  seq(_C_KS), seq(_C_VS), seq(_C_KW), seq(_C_VW),
                  gate(0), gate(1), gate(2),
                  pl.BlockSpec((1, 1, LANES, LANES), lambda bi, gi, i: (bi, gi, 0, 0)),
                  pl.BlockSpec((1, 1, LANES, LANES), lambda bi, gi, i: (bi, 2 + gi, 0, 0)),
                  pl.BlockSpec((NSA_GROUP, 1, TQ, LANES), lambda bi, gi, i: (gi, i, 0, 0)),
                  pl.BlockSpec((NSA_GROUP, 3, TQ, LANES), lambda bi, gi, i: (gi, 0, 0, 0)),
                  pl.BlockSpec((LANES, LANES), lambda bi, gi, i: (0, 0)),
                  pl.BlockSpec((t, LANES), lambda bi, gi, i: (0, 0))],
        out_specs=pl.BlockSpec((1, TQ, 2 * LANES), lambda bi, gi, i: (bi, i, gi)),
        out_shape=jax.ShapeDtypeStruct((b, t, NSA_HEADS * HEAD_DIM), BF16),
        scratch_shapes=[pltpu.VMEM((NSA_GROUP * TQ, LANES), F32),
                        pltpu.VMEM((NSA_GROUP * TQ, 1), F32),
                        pltpu.VMEM((NSA_GROUP * TQ, 1), F32)],
        compiler_params=pltpu.CompilerParams(dimension_semantics=("parallel", "parallel", "arbitrary"),
                                             vmem_limit_bytes=VMEM_LIMIT),
        name="nsa",
    )(p3, p3, p3, p3, p3, p3, p3, p3, kvc, kvc, bias_c, tz, ov, et)


def _sb_kernel(q_ref, k_ref, v_ref, u_ref, o_ref, acc_ref, carry_ref):
    i = pl.program_id(1)
    npair = SB_HEADS // 2
    lane = lax.broadcasted_iota(jnp.int32, (TS, LANES), 1)
    lo = lane < HEAD_DIM
    lane2 = lax.broadcasted_iota(jnp.int32, (2 * TS, TS), 1)
    row2 = lax.broadcasted_iota(jnp.int32, (2 * TS, TS), 0) & (TS - 1)
    causal = lane2 < row2

    qs = []
    for c in range(npair):
        q = q_ref[0, :, c * LANES:(c + 1) * LANES].astype(F32)
        qs.append(jnp.concatenate([jnp.where(lo, q, 0.0), jnp.where(lo, 0.0, q)], axis=0).astype(BF16))

    acc_ref[...] = jnp.zeros(acc_ref.shape, F32)
    carry_ref[...] = jnp.zeros(carry_ref.shape, F32)

    def tile(j, diag):
        off = pl.multiple_of(j * TS, TS)
        pairs = range(npair)
        cols = [slice(c * LANES, (c + 1) * LANES) for c in pairs]
        zs = [_dot_nt(qs[c], k_ref[0, pl.ds(off, TS), cols[c]]) for c in pairs]
        log_betas, log_keeps, k_his, k_los = [], [], [], []
        for c in pairs:
            z = zs[c]
            log_beta = jnp.minimum(z, 0.0) - jnp.log(1.0 + jnp.exp2(-jnp.abs(z))) * LOG2E
            log_keep = log_beta - z
            if diag:
                log_keep = jnp.where(causal, log_keep, 0.0)
            k_hi = log_keep.astype(BF16)
            log_betas.append(log_beta)
            log_keeps.append(log_keep)
            k_his.append(k_hi)
            k_los.append((log_keep - k_hi.astype(F32)).astype(BF16))
        laters = [_dot(k_his[c], u_ref[...]) + _dot(k_los[c], u_ref[...]) for c in pairs]
        a_s = []
        for c in pairs:
            carry = carry_ref[c]
            a = jnp.exp2(log_betas[c] + laters[c] + jnp.concatenate([carry] * (TS // LANES), axis=1))
            if diag:
                a = jnp.where(causal, a, 0.0)
            a_s.append(a.astype(BF16))
            carry_ref[c] = carry + jnp.sum(log_keeps[c], axis=1, keepdims=True)
        for c in pairs:
            acc_ref[c] += _dot(a_s[c], v_ref[0, pl.ds(off, TS), cols[c]])

    tile(i, True)

    def body(n, c):
        tile(i - 1 - n, False)
        return c

    lax.fori_loop(0, i, body, 0)
    for c in range(npair):
        o = acc_ref[c]
        o_ref[0, :, c * LANES:(c + 1) * LANES] = jnp.where(lo, o[:TS], o[TS:]).astype(BF16)


def _sb(p3, u):
    b, t, _ = p3.shape
    width = SB_HEADS * HEAD_DIM
    return pl.pallas_call(
        _sb_kernel,
        grid=(b, t // TS),
        in_specs=[pl.BlockSpec((1, TS, width), lambda bi, i: (bi, i, _C_SBQ * LANES // width)),
                  pl.BlockSpec((1, t, width), lambda bi, i: (bi, 0, _C_SBK * LANES // width)),
                  pl.BlockSpec((1, t, width), lambda bi, i: (bi, 0, _C_SBV * LANES // width)),
                  pl.BlockSpec((TS, TS), lambda bi, i: (0, 0))],
        out_specs=pl.BlockSpec((1, TS, width), lambda bi, i: (bi, i, 0)),
        out_shape=jax.ShapeDtypeStruct((b, t, width), BF16),
        scratch_shapes=[pltpu.VMEM((SB_HEADS // 2, 2 * TS, LANES), F32),
                        pltpu.VMEM((SB_HEADS // 2, 2 * TS, LANES), F32)],
        compiler_params=pltpu.CompilerParams(dimension_semantics=("parallel", "arbitrary"),
                                             vmem_limit_bytes=VMEM_LIMIT),
        name="sb",
    )(p3, p3, p3, u)


def _tail_kernel(x_ref, on_ref, os_ref, ga_ref, wgm_ref, wun_ref, wus_ref, wo_ref, gm_ref,
                 w1_ref, w2_ref, gf_ref, o_ref, *, last_layer):
    x = x_ref[...]
    d = x.shape[1]
    h = _rms(x, ga_ref[...]).astype(BF16)
    gm = 1.0 / (1.0 + jnp.exp(-_dot(h, wgm_ref[...])))
    mixed = gm[:, :d] * _dot(on_ref[...], wun_ref[...]) + gm[:, d:] * _dot(os_ref[...], wus_ref[...])
    x1 = x + _dot(mixed.astype(BF16), wo_ref[...])
    h2 = _rms(x1, gm_ref[...]).astype(BF16)
    dff = w1_ref.shape[1]
    fc = 512
    acc = jnp.zeros_like(x1)
    for f in range(0, dff, fc):
        u = jnp.maximum(_dot(h2, w1_ref[:, f:f + fc]), 0.0)
        acc = acc + _dot((u * u).astype(BF16), w2_ref[f:f + fc, :])
    x2 = x1 + acc
    o_ref[...] = _rms(x2, gf_ref[...]) if last_layer else x2


def _tail(x2, o_nsa, o_sb, g_attn, wgm, wun, wus, wo, g_mlp, w1, w2, g_final, last_layer):
    m, d = x2.shape
    tm = 256
    tok = lambda w: pl.BlockSpec((tm, w), lambda i: (i, 0))
    const = lambda a: pl.BlockSpec(a.shape, lambda i: (0, 0), pipeline_mode=pl.Buffered(1))
    return pl.pallas_call(
        functools.partial(_tail_kernel, last_layer=last_layer),
        grid=(m // tm,),
        in_specs=[tok(d), tok(o_nsa.shape[1]), tok(o_sb.shape[1]), const(g_attn), const(wgm),
                  const(wun), const(wus), const(wo), const(g_mlp), const(w1), const(w2), const(g_final)],
        out_specs=tok(d),
        out_shape=jax.ShapeDtypeStruct((m, d), F32),
        compiler_params=pltpu.CompilerParams(dimension_semantics=("parallel",),
                                             vmem_limit_bytes=VMEM_LIMIT),
        name="tail",
    )(x2, o_nsa, o_sb, g_attn, wgm, wun, wus, wo, g_mlp, w1, w2, g_final)


def _t5_bucket_np(dist):
    n = np.maximum(dist, 0)
    max_exact = N_BUCKETS // 2
    nf = np.maximum(n, 1).astype(np.float32)
    large = max_exact + (np.log(nf / np.float32(max_exact)) / np.float32(math.log(MAX_DISTANCE / max_exact))
                         * np.float32(N_BUCKETS - max_exact)).astype(np.int32)
    return np.where(n < max_exact, n, np.minimum(large, N_BUCKETS - 1)).astype(np.int32)


def _bias_tables(rel_bias, t):
    qi = np.arange(TQ)[:, None]
    kj = np.arange(LANES)[None, :]
    idx = np.stack([_t5_bucket_np(r * TQ + qi - kj) for r in range(3)])
    tz = jnp.transpose(rel_bias[idx], (3, 0, 1, 2))
    near = -(-(MAX_DISTANCE + CMP_LEN - 1) // CMP_STRIDE) - 1
    dist0 = qi - CMP_STRIDE * (kj - near) - (CMP_LEN - 1)
    idx0 = np.where(dist0 >= 0, _t5_bucket_np(dist0), N_BUCKETS - 1)
    base = jnp.transpose(rel_bias[idx0], (2, 0, 1))
    per_tile = TQ // CMP_STRIDE
    bias_c = jnp.stack([jnp.roll(base, per_tile * i - near, axis=2) for i in range(t // TQ)], axis=1)
    return tz, bias_c


def _const_tables(t):
    c_start = np.arange(LANES) * CMP_STRIDE
    s_start = np.arange(LANES) * SEL_LEN
    n_cmp = (t - CMP_LEN) // CMP_STRIDE + 1
    n_sel = t // SEL_LEN
    ov = ((c_start[:, None] < s_start[None, :] + SEL_LEN) & (c_start[:, None] + CMP_LEN > s_start[None, :])
          & (np.arange(LANES)[:, None] < n_cmp) & (np.arange(LANES)[None, :] < n_sel))
    et = (np.arange(t)[:, None] // SEL_LEN) == np.arange(LANES)[None, :]
    u = np.arange(TS)[:, None] > np.arange(TS)[None, :]
    return (jnp.asarray(ov, BF16), jnp.asarray(et, BF16), jnp.asarray(u, BF16))


def _arrange_w_in(w):
    d = w.shape[0]
    nq, nkv, ng = NSA_HEADS * HEAD_DIM, 6 * NSA_KV_HEADS * HEAD_DIM, 3 * NSA_HEADS
    nsb = SB_HEADS * HEAD_DIM
    scale = HEAD_DIM ** -0.5
    q = w[:, :nq] * scale
    kv = w[:, nq:nq + nkv]
    gl = w[:, nq + nkv:nq + nkv + ng]
    sb = w[:, nq + nkv + ng:nq + nkv + ng + 3 * nsb]
    gm = w[:, nq + nkv + ng + 3 * nsb:]
    kvw = NSA_KV_HEADS * HEAD_DIM
    dup = []
    for i6 in (2, 3, 4, 5):
        for g in range(NSA_KV_HEADS):
            sl = kv[:, i6 * kvw + g * HEAD_DIM:i6 * kvw + (g + 1) * HEAD_DIM]
            dup += [sl, sl]
    gates = jnp.transpose(gl.reshape(d, NSA_HEADS, 3), (0, 2, 1))
    gates = jnp.broadcast_to(gates[..., None], (d, 3, NSA_HEADS, HEAD_DIM)).reshape(d, 3 * nq)
    cols = [q] + dup + [gates, sb[:, :nsb] * (scale * LOG2E), sb[:, nsb:], kv[:, :2 * kvw]]
    return jnp.concatenate(cols, axis=1).astype(BF16), gm.astype(BF16)


def kernel(x, norm_attn, w_in, rel_bias, cmp_k_pe, cmp_k_w1, cmp_k_b1, cmp_k_w2, cmp_v_pe, cmp_v_w1, cmp_v_b1, cmp_v_w2, w_up_nsa, w_up_sb, w_out, norm_mlp, w_ff1, w_ff2, norm_final):
    b, t, d = x.shape
    depth = w_in.shape[0]
    assert t % TQ == 0 and t // SEL_LEN <= N_BUCKETS and (t - CMP_LEN) // CMP_STRIDE + 1 < LANES
    tz, bias_c = _bias_tables(rel_bias, t)
    ov, et, u = _const_tables(t)
    half = CMP_LEN // 2 * HEAD_DIM
    xf = x.reshape(b * t, d)
    for layer in range(depth):
        w_arr, w_gm = _arrange_w_in(w_in[layer])
        assert w_arr.shape[1] == _N_CHUNKS * LANES
        proj = _inproj(xf, norm_attn[layer][None, :], w_arr).reshape(b, t, _N_CHUNKS * LANES)

        kvc = proj[:, :, _C_KVC * LANES:].reshape(b, t, 4, HEAD_DIM)
        kvc = jnp.transpose(kvc, (0, 2, 1, 3)).reshape(b, 4, t // CMP_STRIDE, CMP_STRIDE * HEAD_DIM)
        pe = jnp.stack([cmp_k_pe[layer], cmp_v_pe[layer]]).reshape(2, 1, CMP_LEN * HEAD_DIM)
        w1 = jnp.stack([cmp_k_w1[layer], cmp_v_w1[layer]]).astype(BF16)
        b1 = jnp.stack([cmp_k_b1[layer], cmp_v_b1[layer]])[:, None, :]
        w2 = jnp.stack([cmp_k_w2[layer], cmp_v_w2[layer]])
        w2d = jnp.concatenate([w2, w2], axis=2).astype(BF16)
        kvc = _compress(kvc, pe[:, :, :half], pe[:, :, half:], w1[:, :half], w1[:, half:], b1, w2d)

        o_nsa = _nsa(proj, kvc, bias_c, tz, ov, et)
        o_sb = _sb(proj, u)
        xf = _tail(xf, o_nsa.reshape(b * t, -1), o_sb.reshape(b * t, -1), norm_attn[layer][None, :], w_gm,
                   w_up_nsa[layer].astype(BF16), w_up_sb[layer].astype(BF16), w_out[layer].astype(BF16),
                   norm_mlp[layer][None, :], w_ff1[layer].astype(BF16), w_ff2[layer].astype(BF16),
                   norm_final[None, :], layer == depth - 1)
    return xf.reshape(b, t, d)
```

```python
import functools
import math

import numpy as np
import jax
import jax.numpy as jnp
from jax import lax
from jax.experimental import pallas as pl
from jax.experimental.pallas import tpu as pltpu

HEAD_DIM = 64
NSA_HEADS = 8
NSA_KV_HEADS = 2
NSA_GROUP = NSA_HEADS // NSA_KV_HEADS
SB_HEADS = 8
CMP_LEN = 32
CMP_STRIDE = 16
CMP_HIDDEN = 256
SEL_LEN = 64
SEL_TOPK = 4
WINDOW = 512
N_BUCKETS = 32
MAX_DISTANCE = 128
EPS = 1e-6
NEG = -1e30
FORCE_BONUS = 1e4

LANES = 128
TN = 256
TS = 256
VMEM_LIMIT = 56 * 1024 * 1024
LOG2E = math.log2(math.e)
MASK_BIG = 2.0 ** 30
SEL_LANE0 = HEAD_DIM
BIAS_LANE = HEAD_DIM + N_BUCKETS

F32 = jnp.float32
BF16 = jnp.bfloat16

_C_QN = 0
_C_KS = 4
_C_VS = 6
_C_KW = 8
_C_VW = 10
_C_GATE = 12
_C_SBQ = 24
_C_SBK = 28
_C_SBV = 32
_C_KVC = 36
_N_CHUNKS = 38


def _dot(a, b):
    return jnp.dot(a, b, preferred_element_type=F32)


def _dot_nt(a, b):
    return lax.dot_general(a, b, (((1,), (1,)), ((), ())), preferred_element_type=F32)


def _rms(x, g):
    return x * lax.rsqrt(jnp.mean(x * x, axis=-1, keepdims=True) + EPS) * g


def _inproj_kernel(x_ref, g_ref, w_ref, o_ref):
    h = _rms(x_ref[...], g_ref[...]).astype(BF16)
    n = o_ref.shape[1]
    for c in range(0, n, 256):
        o_ref[:, c:c + 256] = _dot(h, w_ref[:, c:c + 256]).astype(BF16)


def _inproj(x2, g, w):
    m, d = x2.shape
    n = w.shape[1]
    tm = 512
    return pl.pallas_call(
        _inproj_kernel,
        grid=(m // tm,),
        in_specs=[pl.BlockSpec((tm, d), lambda i: (i, 0)),
                  pl.BlockSpec((1, d), lambda i: (0, 0)),
                  pl.BlockSpec((d, n), lambda i: (0, 0))],
        out_specs=pl.BlockSpec((tm, n), lambda i: (i, 0)),
        out_shape=jax.ShapeDtypeStruct((m, n), BF16),
        compiler_params=pltpu.CompilerParams(dimension_semantics=("parallel",),
                                             vmem_limit_bytes=VMEM_LIMIT),
        name="inproj",
    )(x2, g, w)


def _gelu_tanh(x):
    return 0.5 * x * (1.0 + jnp.tanh(math.sqrt(2.0 / math.pi) * (x + 0.044715 * (x * x * x))))


def _compress_kernel(x_ref, pea_ref, peb_ref, w1a_ref, w1b_ref, b1_ref, w2_ref, o_ref):
    for s in range(4):
        r = s // 2
        x = x_ref[0, s].astype(F32)
        a = _dot((x + pea_ref[r]).astype(BF16), w1a_ref[r])
        b = _dot((x + peb_ref[r]).astype(BF16), w1b_ref[r])
        hid = a + pltpu.roll(b, shift=b.shape[0] - 1, axis=0) + b1_ref[r]
        o_ref[0, s] = _dot(_gelu_tanh(hid).astype(BF16), w2_ref[r]).astype(BF16)


def _compress(xc, pea, peb, w1a, w1b, b1, w2d):
    b = xc.shape[0]
    nchunk, width = xc.shape[2], xc.shape[3]
    full = lambda shape: pl.BlockSpec(shape, lambda i: (0,) * len(shape))
    return pl.pallas_call(
        _compress_kernel,
        grid=(b,),
        in_specs=[pl.BlockSpec((1, 4, nchunk, width), lambda i: (i, 0, 0, 0)),
                  full(pea.shape), full(peb.shape), full(w1a.shape), full(w1b.shape),
                  full(b1.shape), full(w2d.shape)],
        out_specs=pl.BlockSpec((1, 4, nchunk, LANES), lambda i: (i, 0, 0, 0)),
        out_shape=jax.ShapeDtypeStruct((b, 4, nchunk, LANES), BF16),
        compiler_params=pltpu.CompilerParams(dimension_semantics=("parallel",),
                                             vmem_limit_bytes=VMEM_LIMIT),
        name="compress",
    )(xc, pea, peb, w1a, w1b, b1, w2d)


def _nsa_kernel(q_ref, ks_ref, vs_ref, kw_ref, vw_ref, g0_ref, g1_ref, g2_ref, kc_ref, vc_ref,
                bc_ref, tzd_ref, ov_ref, qaux_ref, kaux_ref, o_ref,
                ksx_ref, vsx_ref, kwx_ref, vwx_ref, m_ref, acc_ref):
    i = pl.program_id(2)
    hg = NSA_GROUP
    t = ks_ref.shape[1]
    lane = lax.broadcasted_iota(jnp.int32, (TN, LANES), 1)
    row = lax.broadcasted_iota(jnp.int32, (TN, LANES), 0)
    lo = lane < HEAD_DIM
    t_abs = i * TN + row
    krow = lax.broadcasted_iota(jnp.int32, (TN, TN), 0)
    kcol = lax.broadcasted_iota(jnp.int32, (TN, TN), 1)

    def hrows(h):
        return slice(h * TN, (h + 1) * TN)

    @pl.when(i == 0)
    def _():
        ch = 512
        lo_c = lax.broadcasted_iota(jnp.int32, (ch, LANES), 1) < HEAD_DIM
        for r0 in range(0, t, ch):
            rs = slice(r0, r0 + ch)
            ksx_ref[rs] = jnp.where(lo_c, ks_ref[0, rs].astype(F32), kaux_ref[0, rs].astype(F32)).astype(BF16)
            kwx_ref[rs] = jnp.where(lo_c, kw_ref[0, rs].astype(F32), kaux_ref[1, rs].astype(F32)).astype(BF16)
            vsx_ref[rs] = jnp.where(lo_c, vs_ref[0, rs].astype(F32), 1.0).astype(BF16)
            vwx_ref[rs] = jnp.where(lo_c, vw_ref[0, rs].astype(F32), 1.0).astype(BF16)

    q = q_ref[0].astype(F32)
    qa, qb = q[:, :LANES], q[:, LANES:]
    heads = (qa, pltpu.roll(qa, HEAD_DIM, 1), qb, pltpu.roll(qb, HEAD_DIM, 1))
    base = [jnp.where(lo, heads[h], qaux_ref[h]) for h in range(hg)]
    lhs_cmp = jnp.concatenate(base, axis=0).astype(BF16)

    lo_c = lax.broadcasted_iota(jnp.int32, (LANES, LANES), 1) < HEAD_DIM
    kc = jnp.where(lo_c, kc_ref[0, 0].astype(F32), 0.0).astype(BF16)
    s = _dot_nt(lhs_cmp, kc)
    valid_c = (t_abs >= CMP_STRIDE * lane + (CMP_LEN - 1)) & (lane < LANES - 1)
    ps = []
    for h in range(hg):
        sh = jnp.where(valid_c, s[hrows(h)] + bc_ref[h, 0], NEG)
        e = jnp.exp2(sh - jnp.max(sh, axis=1, keepdims=True))
        ps.append(jnp.where(valid_c, e, 0.0) / jnp.sum(e, axis=1, keepdims=True))
    o_cmp = _dot(jnp.concatenate(ps, axis=0).astype(BF16), vc_ref[0, 0])

    psum = (ps[0] + ps[1]) + (ps[2] + ps[3])
    p_hi = psum.astype(BF16)
    p_lo = (psum - p_hi.astype(F32)).astype(BF16)
    imp = _dot(p_hi, ov_ref[...]) + _dot(p_lo, ov_ref[...])
    n_sel = t // SEL_LEN
    forced = (lane == t_abs // SEL_LEN) | (lane == 0)
    sel_ok = lane * SEL_LEN <= t_abs
    score = jnp.where(forced | (lane >= n_sel), -jnp.inf, jnp.where(sel_ok, imp, -FORCE_BONUS))
    lane_f = lane.astype(F32)
    selm = jnp.where(forced, 1.0, 0.0)
    for k in range(SEL_TOPK - 1):
        mx = jnp.max(score, axis=1, keepdims=True)
        idx = jnp.min(jnp.where(score == mx, lane_f, float(LANES)), axis=1, keepdims=True)
        hit = lane_f == idx
        if k == SEL_TOPK - 2:
            hit = hit & (t_abs < SEL_LEN)
        selm = jnp.where(hit, 1.0, selm)
        score = jnp.where(hit, -jnp.inf, score)
    negsel = pltpu.roll(jnp.where(lane < n_sel, (selm - 1.0) * MASK_BIG, 0.0), SEL_LANE0, 1)
    lhs = [(bh + negsel).astype(BF16) for bh in base]

    def process(jobs, first):
        ss = [post(_dot_nt(lhs[h], kt)) for (_, h, kt, _, post) in jobs]
        ps, alphas = [], []
        for (br, h, _, _, _), sh in zip(jobs, ss):
            mx = jnp.max(sh, axis=1, keepdims=True)
            if first:
                m_new = jnp.broadcast_to(mx, (TN, LANES))
                alphas.append(None)
            else:
                m_old = m_ref[br, h]
                m_new = jnp.maximum(m_old, mx)
                alphas.append(jnp.exp2(m_old - m_new))
            m_ref[br, h] = m_new
            ps.append(jnp.exp2(sh - jnp.concatenate([m_new] * (TN // LANES), axis=1)).astype(BF16))
        for (br, h, _, vt, _), p, alpha in zip(jobs, ps, alphas):
            pv = _dot(p, vt)
            acc_ref[br, h] = pv if first else alpha * acc_ref[br, h] + pv

    def finish(br):
        outs = []
        for h in range(hg):
            acc = acc_ref[br, h]
            outs.append(acc / pltpu.roll(acc, HEAD_DIM, 1))
        return outs

    kxs, vxs = (ksx_ref, kwx_ref), (vsx_ref, vwx_ref)
    causal = kcol <= krow
    inside = kcol > krow

    def tile_jobs(branches, off, post):
        return [(br, h, kxs[br][pl.ds(off, TN), :], vxs[br][pl.ds(off, TN), :], functools.partial(post, h))
                for br in branches for h in range(hg)]

    off_i = pl.multiple_of(i * TN, TN)
    process(tile_jobs((0, 1), off_i, lambda h, s: jnp.where(causal, s + tzd_ref[h, 0], NEG)), True)

    @pl.when(i >= 1)
    def _():
        off = pl.multiple_of(jnp.maximum(i - 1, 0) * TN, TN)
        process(tile_jobs((0, 1), off, lambda h, s: s + tzd_ref[h, 1]), False)

    @pl.when(i >= WINDOW // TN)
    def _():
        off = pl.multiple_of(jnp.maximum(i - WINDOW // TN, 0) * TN, TN)
        jobs = (tile_jobs((1,), off, lambda h, s: jnp.where(inside, s, NEG))
                + tile_jobs((0,), 0, lambda h, s: s))
        process(jobs, False)

    def sel_body(j, carry):
        process(tile_jobs((0,), pl.multiple_of(j * TN, TN), lambda h, s: s), False)
        return carry

    lax.fori_loop(1, jnp.maximum(i - 1, 1), sel_body, 0)
    o_sel = finish(0)
    o_win = finish(1)

    def sig(ref):
        return 1.0 / (1.0 + jnp.exp(-ref[0].astype(F32)))

    g_cmp, g_sel, g_win = sig(g0_ref), sig(g1_ref), sig(g2_ref)
    for c in range(hg // 2):
        cl = slice(c * LANES, (c + 1) * LANES)
        he, ho = 2 * c, 2 * c + 1
        out = (g_cmp[:, cl] * jnp.where(lo, o_cmp[hrows(he)], o_cmp[hrows(ho)])
               + g_sel[:, cl] * jnp.where(lo, o_sel[he], pltpu.roll(o_sel[ho], HEAD_DIM, 1))
               + g_win[:, cl] * jnp.where(lo, o_win[he], pltpu.roll(o_win[ho], HEAD_DIM, 1)))
        o_ref[0, :, cl] = out.astype(BF16)


def _nsa(p3, kvc, bias_c, tzd, ov, qaux, kaux):
    b, t, _ = p3.shape
    g = NSA_KV_HEADS
    nq = t // TN
    seq = lambda col: pl.BlockSpec((1, t, LANES), lambda bi, gi, i, col=col: (bi, 0, col + gi))
    gate = lambda br: pl.BlockSpec((1, TN, 2 * LANES),
                                   lambda bi, gi, i, br=br: (bi, i, (_C_GATE + 4 * br) // 2 + gi))
    return pl.pallas_call(
        _nsa_kernel,
        grid=(b, g, nq),
        in_specs=[pl.BlockSpec((1, TN, 2 * LANES), lambda bi, gi, i: (bi, i, _C_QN // 2 + gi)),
                  seq(_C_KS), seq(_C_VS), seq(_C_KW), seq(_C_VW),
                  gate(0), gate(1), gate(2),
                  pl.BlockSpec((1, 1, LANES, LANES), lambda bi, gi, i: (bi, gi, 0, 0)),
                  pl.BlockSpec((1, 1, LANES, LANES), lambda bi, gi, i: (bi, 2 + gi, 0, 0)),
                  pl.BlockSpec((NSA_GROUP, 1, TN, LANES), lambda bi, gi, i: (gi, i, 0, 0)),
                  pl.BlockSpec((NSA_GROUP, 2, TN, TN), lambda bi, gi, i: (gi, 0, 0, 0)),
                  pl.BlockSpec((LANES, LANES), lambda bi, gi, i: (0, 0)),
                  pl.BlockSpec((NSA_GROUP, 1, LANES), lambda bi, gi, i: (gi, 0, 0)),
                  pl.BlockSpec((2, t, LANES), lambda bi, gi, i: (0, 0, 0))],
        out_specs=pl.BlockSpec((1, TN, 2 * LANES), lambda bi, gi, i: (bi, i, gi)),
        out_shape=jax.ShapeDtypeStruct((b, t, NSA_HEADS * HEAD_DIM), BF16),
        scratch_shapes=[pltpu.VMEM((t, LANES), BF16)] * 4
                       + [pltpu.VMEM((2, NSA_GROUP, TN, LANES), F32)] * 2,
        compiler_params=pltpu.CompilerParams(dimension_semantics=("parallel", "parallel", "arbitrary"),
                                             vmem_limit_bytes=VMEM_LIMIT),
        name="nsa",
    )(p3, p3, p3, p3, p3, p3, p3, p3, kvc, kvc, bias_c, tzd, ov, qaux, kaux)


def _sb_kernel(q_ref, k_ref, v_ref, u_ref, o_ref, acc_ref, carry_ref):
    i = pl.program_id(1)
    npair = SB_HEADS // 2
    lane = lax.broadcasted_iota(jnp.int32, (TS, LANES), 1)
    lo = lane < HEAD_DIM
    lane2 = lax.broadcasted_iota(jnp.int32, (2 * TS, TS), 1)
    row2 = lax.broadcasted_iota(jnp.int32, (2 * TS, TS), 0) & (TS - 1)
    causal = lane2 < row2

    qs = []
    for c in range(npair):
        q = q_ref[0, :, c * LANES:(c + 1) * LANES].astype(F32)
        qs.append(jnp.concatenate([jnp.where(lo, q, 0.0), jnp.where(lo, 0.0, q)], axis=0).astype(BF16))

    acc_ref[...] = jnp.zeros(acc_ref.shape, F32)
    carry_ref[...] = jnp.zeros(carry_ref.shape, F32)

    def tile(j, diag):
        off = pl.multiple_of(j * TS, TS)
        pairs = range(npair)
        cols = [slice(c * LANES, (c + 1) * LANES) for c in pairs]
        zs = [_dot_nt(qs[c], k_ref[0, pl.ds(off, TS), cols[c]]) for c in pairs]
        log_betas, log_keeps, k_his, k_los = [], [], [], []
        for c in pairs:
            z = zs[c]
            log_beta = jnp.minimum(z, 0.0) - jnp.log(1.0 + jnp.exp2(-jnp.abs(z))) * LOG2E
            log_keep = log_beta - z
            if diag:
                log_keep = jnp.where(causal, log_keep, 0.0)
            k_hi = log_keep.astype(BF16)
            log_betas.append(log_beta)
            log_keeps.append(log_keep)
            k_his.append(k_hi)
            k_los.append((log_keep - k_hi.astype(F32)).astype(BF16))
        laters = [_dot(k_his[c], u_ref[...]) + _dot(k_los[c], u_ref[...]) for c in pairs]
        a_s = []
        for c in pairs:
            carry = carry_ref[c]
            a = jnp.exp2(log_betas[c] + laters[c] + jnp.concatenate([carry] * (TS // LANES), axis=1))
            if diag:
                a = jnp.where(causal, a, 0.0)
            a_s.append(a.astype(BF16))
            carry_ref[c] = carry + jnp.sum(log_keeps[c], axis=1, keepdims=True)
        for c in pairs:
            acc_ref[c] += _dot(a_s[c], v_ref[0, pl.ds(off, TS), cols[c]])

    tile(i, True)

    def body(n, c):
        tile(i - 1 - n, False)
        return c

    lax.fori_loop(0, i, body, 0)
    for c in range(npair):
        o = acc_ref[c]
        o_ref[0, :, c * LANES:(c + 1) * LANES] = jnp.where(lo, o[:TS], o[TS:]).astype(BF16)


def _sb(p3, u):
    b, t, _ = p3.shape
    width = SB_HEADS * HEAD_DIM
    return pl.pallas_call(
        _sb_kernel,
        grid=(b, t // TS),
        in_specs=[pl.BlockSpec((1, TS, width), lambda bi, i: (bi, i, _C_SBQ * LANES // width)),
                  pl.BlockSpec((1, t, width), lambda bi, i: (bi, 0, _C_SBK * LANES // width)),
                  pl.BlockSpec((1, t, width), lambda bi, i: (bi, 0, _C_SBV * LANES // width)),
                  pl.BlockSpec((TS, TS), lambda bi, i: (0, 0))],
        out_specs=pl.BlockSpec((1, TS, width), lambda bi, i: (bi, i, 0)),
        out_shape=jax.ShapeDtypeStruct((b, t, width), BF16),
        scratch_shapes=[pltpu.VMEM((SB_HEADS // 2, 2 * TS, LANES), F32),
                        pltpu.VMEM((SB_HEADS // 2, 2 * TS, LANES), F32)],
        compiler_params=pltpu.CompilerParams(dimension_semantics=("parallel", "arbitrary"),
                                             vmem_limit_bytes=VMEM_LIMIT),
        name="sb",
    )(p3, p3, p3, u)


def _tail_kernel(x_ref, on_ref, os_ref, ga_ref, wgm_ref, wun_ref, wus_ref, wo_ref, gm_ref,
                 w1_ref, w2_ref, gf_ref, o_ref, *, last_layer):
    x = x_ref[...]
    d = x.shape[1]
    h = _rms(x, ga_ref[...]).astype(BF16)
    gm = 1.0 / (1.0 + jnp.exp(-_dot(h, wgm_ref[...])))
    mixed = gm[:, :d] * _dot(on_ref[...], wun_ref[...]) + gm[:, d:] * _dot(os_ref[...], wus_ref[...])
    x1 = x + _dot(mixed.astype(BF16), wo_ref[...])
    h2 = _rms(x1, gm_ref[...]).astype(BF16)
    dff = w1_ref.shape[1]
    fc = 512
    acc = jnp.zeros_like(x1)
    for f in range(0, dff, fc):
        u = jnp.maximum(_dot(h2, w1_ref[:, f:f + fc]), 0.0)
        acc = acc + _dot((u * u).astype(BF16), w2_ref[f:f + fc, :])
    x2 = x1 + acc
    o_ref[...] = _rms(x2, gf_ref[...]) if last_layer else x2


def _tail(x2, o_nsa, o_sb, g_attn, wgm, wun, wus, wo, g_mlp, w1, w2, g_final, last_layer):
    m, d = x2.shape
    tm = 256
    tok = lambda w: pl.BlockSpec((tm, w), lambda i: (i, 0))
    const = lambda a: pl.BlockSpec(a.shape, lambda i: (0, 0), pipeline_mode=pl.Buffered(1))
    return pl.pallas_call(
        functools.partial(_tail_kernel, last_layer=last_layer),
        grid=(m // tm,),
        in_specs=[tok(d), tok(o_nsa.shape[1]), tok(o_sb.shape[1]), const(g_attn), const(wgm),
                  const(wun), const(wus), const(wo), const(g_mlp), const(w1), const(w2), const(g_final)],
        out_specs=tok(d),
        out_shape=jax.ShapeDtypeStruct((m, d), F32),
        compiler_params=pltpu.CompilerParams(dimension_semantics=("parallel",),
                                             vmem_limit_bytes=VMEM_LIMIT),
        name="tail",
    )(x2, o_nsa, o_sb, g_attn, wgm, wun, wus, wo, g_mlp, w1, w2, g_final)


def _t5_bucket_np(dist):
    n = np.maximum(dist, 0)
    max_exact = N_BUCKETS // 2
    nf = np.maximum(n, 1).astype(np.float32)
    large = max_exact + (np.log(nf / np.float32(max_exact)) / np.float32(math.log(MAX_DISTANCE / max_exact))
                         * np.float32(N_BUCKETS - max_exact)).astype(np.int32)
    return np.where(n < max_exact, n, np.minimum(large, N_BUCKETS - 1)).astype(np.int32)


def _toeplitz(w, n):
    lead = w.shape[:-1]
    tiled = jnp.tile(w, (1,) * len(lead) + (n,))[..., :n * (2 * n - 1)]
    return tiled.reshape(lead + (n, 2 * n - 1))[..., :n]


def _bias_tables(rel_bias, t):
    nh = rel_bias.shape[1]
    span = 2 * TN + MAX_DISTANCE
    bvec = rel_bias[_t5_bucket_np(np.arange(span))].T * LOG2E
    far = bvec[:, MAX_DISTANCE]
    tiles = []
    for r in range(2):
        m = np.arange(2 * TN)
        dist = r * TN - np.where(m < TN, m, m - 2 * TN)
        w = jnp.where(dist[None, :] >= 0, bvec[:, np.clip(dist, 0, span - 1)], 0.0)
        tiles.append(_toeplitz(w - far[:, None], TN))
    tzd = jnp.stack(tiles, axis=1)
    near = -(-(MAX_DISTANCE + CMP_LEN - 1) // CMP_STRIDE) - 1
    cols = []
    for x in range(LANES):
        d0 = -CMP_STRIDE * (x - near) - (CMP_LEN - 1)
        if d0 + TN <= 0 or d0 >= MAX_DISTANCE:
            cols.append(jnp.broadcast_to(far[:, None], (nh, TN)))
        else:
            seg = bvec[:, max(d0, 0):d0 + TN]
            cols.append(jnp.pad(seg, ((0, 0), (TN - seg.shape[1], 0)), constant_values=0.0))
    base = jnp.stack(cols, axis=2)
    per_tile = TN // CMP_STRIDE
    bias_c = jnp.stack([jnp.roll(base, per_tile * i - near, axis=2) for i in range(t // TN)], axis=1)
    far_hi = far.astype(BF16).astype(F32)
    qaux = jnp.zeros((nh, 1, LANES), F32)
    qaux = qaux.at[:, 0, BIAS_LANE].set(far_hi).at[:, 0, BIAS_LANE + 1].set(far - far_hi)
    return tzd, bias_c, qaux


def _const_tables(t):
    c_start = np.arange(LANES) * CMP_STRIDE
    s_start = np.arange(LANES) * SEL_LEN
    n_cmp = (t - CMP_LEN) // CMP_STRIDE + 1
    n_sel = t // SEL_LEN
    ov = ((c_start[:, None] < s_start[None, :] + SEL_LEN) & (c_start[:, None] + CMP_LEN > s_start[None, :])
          & (np.arange(LANES)[:, None] < n_cmp) & (np.arange(LANES)[None, :] < n_sel))
    kaux = np.zeros((2, t, LANES), np.float32)
    kaux[0, np.arange(t), SEL_LANE0 + np.arange(t) // SEL_LEN] = 1.0
    kaux[:, :, BIAS_LANE:BIAS_LANE + 2] = 1.0
    u = np.arange(TS)[:, None] > np.arange(TS)[None, :]
    return jnp.asarray(ov, BF16), jnp.asarray(kaux, BF16), jnp.asarray(u, BF16)


def _arrange_w_in(w):
    d = w.shape[0]
    nq, nkv, ng = NSA_HEADS * HEAD_DIM, 6 * NSA_KV_HEADS * HEAD_DIM, 3 * NSA_HEADS
    nsb = SB_HEADS * HEAD_DIM
    scale = HEAD_DIM ** -0.5 * LOG2E
    q = w[:, :nq] * scale
    kv = w[:, nq:nq + nkv]
    gl = w[:, nq + nkv:nq + nkv + ng]
    sb = w[:, nq + nkv + ng:nq + nkv + ng + 3 * nsb]
    gm = w[:, nq + nkv + ng + 3 * nsb:]
    kvw = NSA_KV_HEADS * HEAD_DIM
    dup = []
    for i6 in (2, 3, 4, 5):
        for g in range(NSA_KV_HEADS):
            sl = kv[:, i6 * kvw + g * HEAD_DIM:i6 * kvw + (g + 1) * HEAD_DIM]
            dup += [sl, sl]
    gates = jnp.transpose(gl.reshape(d, NSA_HEADS, 3), (0, 2, 1))
    gates = jnp.broadcast_to(gates[..., None], (d, 3, NSA_HEADS, HEAD_DIM)).reshape(d, 3 * nq)
    cols = [q] + dup + [gates, sb[:, :nsb] * scale, sb[:, nsb:], kv[:, :2 * kvw]]
    return jnp.concatenate(cols, axis=1).astype(BF16), gm.astype(BF16)


def kernel(x, norm_attn, w_in, rel_bias, cmp_k_pe, cmp_k_w1, cmp_k_b1, cmp_k_w2, cmp_v_pe, cmp_v_w1, cmp_v_b1, cmp_v_w2, w_up_nsa, w_up_sb, w_out, norm_mlp, w_ff1, w_ff2, norm_final):
    b, t, d = x.shape
    depth = w_in.shape[0]
    assert t % TN == 0 and t % TS == 0 and WINDOW == 2 * TN and MAX_DISTANCE <= TN
    assert t // SEL_LEN <= N_BUCKETS and (t - CMP_LEN) // CMP_STRIDE + 1 < LANES
    tzd, bias_c, qaux = _bias_tables(rel_bias, t)
    ov, kaux, u = _const_tables(t)
    half = CMP_LEN // 2 * HEAD_DIM
    xf = x.reshape(b * t, d)
    for layer in range(depth):
        w_arr, w_gm = _arrange_w_in(w_in[layer])
        assert w_arr.shape[1] == _N_CHUNKS * LANES
        proj = _inproj(xf, norm_attn[layer][None, :], w_arr).reshape(b, t, _N_CHUNKS * LANES)

        kvc = proj[:, :, _C_KVC * LANES:].reshape(b, t, 4, HEAD_DIM)
        kvc = jnp.transpose(kvc, (0, 2, 1, 3)).reshape(b, 4, t // CMP_STRIDE, CMP_STRIDE * HEAD_DIM)
        pe = jnp.stack([cmp_k_pe[layer], cmp_v_pe[layer]]).reshape(2, 1, CMP_LEN * HEAD_DIM)
        w1 = jnp.stack([cmp_k_w1[layer], cmp_v_w1[layer]]).astype(BF16)
        b1 = jnp.stack([cmp_k_b1[layer], cmp_v_b1[layer]])[:, None, :]
        w2 = jnp.stack([cmp_k_w2[layer], cmp_v_w2[layer]])
        w2d = jnp.concatenate([w2, w2], axis=2).astype(BF16)
        kvc = _compress(kvc, pe[:, :, :half], pe[:, :, half:], w1[:, :half], w1[:, half:], b1, w2d)

        o_nsa = _nsa(proj, kvc, bias_c, tzd, ov, qaux, kaux)
        o_sb = _sb(proj, u)
        xf = _tail(xf, o_nsa.reshape(b * t, -1), o_sb.reshape(b * t, -1), norm_attn[layer][None, :], w_gm,
                   w_up_nsa[layer].astype(BF16), w_up_sb[layer].astype(BF16), w_out[layer].astype(BF16),
                   norm_mlp[layer][None, :], w_ff1[layer].astype(BF16), w_ff2[layer].astype(BF16),
                   norm_final[None, :], layer == depth - 1)
    return xf.reshape(b, t, d)
```

```python
import functools
import math

import numpy as np
import jax
import jax.numpy as jnp
from jax import lax
from jax.experimental import pallas as pl
from jax.experimental.pallas import tpu as pltpu

HEAD_DIM = 64
NSA_HEADS = 8
NSA_KV_HEADS = 2
NSA_GROUP = NSA_HEADS // NSA_KV_HEADS
SB_HEADS = 8
CMP_LEN = 32
CMP_STRIDE = 16
CMP_HIDDEN = 256
SEL_LEN = 64
SEL_TOPK = 4
WINDOW = 512
N_BUCKETS = 32
MAX_DISTANCE = 128
EPS = 1e-6
NEG = -1e30
FORCE_BONUS = 1e4

LANES = 128
TN = 256
TS = 256
VMEM_LIMIT = 56 * 1024 * 1024
LOG2E = math.log2(math.e)
MASK_BIG = 2.0 ** 30
SEL_LANE0 = HEAD_DIM
BIAS_LANE = HEAD_DIM + N_BUCKETS

F32 = jnp.float32
BF16 = jnp.bfloat16

_C_SBQ = 0
_C_SBK = 4
_C_SBV = 8
_C_QN = 12
_C_KS = 16
_C_VS = 17
_C_KW = 18
_C_VW = 19
_C_GATE = 20
_C_KVC = 21
_N_CHUNKS = 23


def _dot(a, b):
    return jnp.dot(a, b, preferred_element_type=F32)


def _dot_nt(a, b):
    return lax.dot_general(a, b, (((1,), (1,)), ((), ())), preferred_element_type=F32)


def _rms(x, g):
    return x * lax.rsqrt(jnp.mean(x * x, axis=-1, keepdims=True) + EPS) * g


def _inproj_kernel(x_ref, g_ref, w_ref, o_ref):
    h = _rms(x_ref[...], g_ref[...]).astype(BF16)
    n = o_ref.shape[1]
    for c in range(0, n, 256):
        cs = slice(c, min(c + 256, n))
        o_ref[:, cs] = _dot(h, w_ref[:, cs]).astype(BF16)


def _inproj(x2, g, w):
    m, d = x2.shape
    n = w.shape[1]
    tm = 512
    return pl.pallas_call(
        _inproj_kernel,
        grid=(m // tm,),
        in_specs=[pl.BlockSpec((tm, d), lambda i: (i, 0)),
                  pl.BlockSpec((1, d), lambda i: (0, 0)),
                  pl.BlockSpec((d, n), lambda i: (0, 0))],
        out_specs=pl.BlockSpec((tm, n), lambda i: (i, 0)),
        out_shape=jax.ShapeDtypeStruct((m, n), BF16),
        compiler_params=pltpu.CompilerParams(dimension_semantics=("parallel",),
                                             vmem_limit_bytes=VMEM_LIMIT),
        name="inproj",
    )(x2, g, w)


def _gelu_tanh(x):
    return 0.5 * x * (1.0 + jnp.tanh(math.sqrt(2.0 / math.pi) * (x + 0.044715 * (x * x * x))))


def _compress_kernel(x_ref, pea_ref, peb_ref, w1a_ref, w1b_ref, b1_ref, w2_ref, o_ref):
    for s in range(4):
        r = s // 2
        x = x_ref[0, s].astype(F32)
        a = _dot((x + pea_ref[r]).astype(BF16), w1a_ref[r])
        b = _dot((x + peb_ref[r]).astype(BF16), w1b_ref[r])
        hid = a + pltpu.roll(b, shift=b.shape[0] - 1, axis=0) + b1_ref[r]
        o_ref[0, s] = _dot(_gelu_tanh(hid).astype(BF16), w2_ref[r]).astype(BF16)


def _compress(xc, pea, peb, w1a, w1b, b1, w2d):
    b = xc.shape[0]
    nchunk, width = xc.shape[2], xc.shape[3]
    full = lambda shape: pl.BlockSpec(shape, lambda i: (0,) * len(shape))
    return pl.pallas_call(
        _compress_kernel,
        grid=(b,),
        in_specs=[pl.BlockSpec((1, 4, nchunk, width), lambda i: (i, 0, 0, 0)),
                  full(pea.shape), full(peb.shape), full(w1a.shape), full(w1b.shape),
                  full(b1.shape), full(w2d.shape)],
        out_specs=pl.BlockSpec((1, 4, nchunk, LANES), lambda i: (i, 0, 0, 0)),
        out_shape=jax.ShapeDtypeStruct((b, 4, nchunk, LANES), BF16),
        compiler_params=pltpu.CompilerParams(dimension_semantics=("parallel",),
                                             vmem_limit_bytes=VMEM_LIMIT),
        name="compress",
    )(xc, pea, peb, w1a, w1b, b1, w2d)


def _nsa_kernel(q_ref, ks_ref, vs_ref, kw_ref, vw_ref, g_ref, gexp_ref, kc_ref, vc_ref,
                bc_ref, tzd_ref, ov_ref, qaux_ref, kaux_ref, o_ref,
                ksx_ref, vsx_ref, kwx_ref, vwx_ref, m_ref, acc_ref):
    i = pl.program_id(2)
    hg = NSA_GROUP
    t = ks_ref.shape[1]
    lane = lax.broadcasted_iota(jnp.int32, (TN, LANES), 1)
    row = lax.broadcasted_iota(jnp.int32, (TN, LANES), 0)
    lo = lane < HEAD_DIM
    t_abs = i * TN + row
    krow = lax.broadcasted_iota(jnp.int32, (TN, TN), 0)
    kcol = lax.broadcasted_iota(jnp.int32, (TN, TN), 1)

    def hrows(h):
        return slice(h * TN, (h + 1) * TN)

    @pl.when(i == 0)
    def _():
        ch = 512
        lo_c = lax.broadcasted_iota(jnp.int32, (ch, LANES), 1) < HEAD_DIM
        first_group = pl.program_id(1) == 0

        def own(ref, rs):
            x = ref[0, rs].astype(F32)
            return jnp.where(first_group, x, pltpu.roll(x, HEAD_DIM, 1))

        for r0 in range(0, t, ch):
            rs = slice(r0, r0 + ch)
            ksx_ref[rs] = jnp.where(lo_c, own(ks_ref, rs), kaux_ref[0, rs].astype(F32)).astype(BF16)
            kwx_ref[rs] = jnp.where(lo_c, own(kw_ref, rs), kaux_ref[1, rs].astype(F32)).astype(BF16)
            vsx_ref[rs] = jnp.where(lo_c, own(vs_ref, rs), 1.0).astype(BF16)
            vwx_ref[rs] = jnp.where(lo_c, own(vw_ref, rs), 1.0).astype(BF16)

    q = q_ref[0].astype(F32)
    qa, qb = q[:, :LANES], q[:, LANES:]
    heads = (qa, pltpu.roll(qa, HEAD_DIM, 1), qb, pltpu.roll(qb, HEAD_DIM, 1))
    base = [jnp.where(lo, heads[h], qaux_ref[h]) for h in range(hg)]
    lhs_cmp = jnp.concatenate(base, axis=0).astype(BF16)

    lo_c = lax.broadcasted_iota(jnp.int32, (LANES, LANES), 1) < HEAD_DIM
    kc = jnp.where(lo_c, kc_ref[0, 0].astype(F32), 0.0).astype(BF16)
    s = _dot_nt(lhs_cmp, kc)
    valid_c = (t_abs >= CMP_STRIDE * lane + (CMP_LEN - 1)) & (lane < LANES - 1)
    ps = []
    for h in range(hg):
        sh = jnp.where(valid_c, s[hrows(h)] + bc_ref[h, 0], NEG)
        e = jnp.exp2(sh - jnp.max(sh, axis=1, keepdims=True))
        ps.append(jnp.where(valid_c, e, 0.0) / jnp.sum(e, axis=1, keepdims=True))
    o_cmp = _dot(jnp.concatenate(ps, axis=0).astype(BF16), vc_ref[0, 0])

    psum = (ps[0] + ps[1]) + (ps[2] + ps[3])
    p_hi = psum.astype(BF16)
    p_lo = (psum - p_hi.astype(F32)).astype(BF16)
    imp = _dot(p_hi, ov_ref[...]) + _dot(p_lo, ov_ref[...])
    n_sel = t // SEL_LEN
    forced = (lane == t_abs // SEL_LEN) | (lane == 0)
    sel_ok = lane * SEL_LEN <= t_abs
    score = jnp.where(forced | (lane >= n_sel), -jnp.inf, jnp.where(sel_ok, imp, -FORCE_BONUS))
    lane_f = lane.astype(F32)
    selm = jnp.where(forced, 1.0, 0.0)
    for k in range(SEL_TOPK - 1):
        mx = jnp.max(score, axis=1, keepdims=True)
        idx = jnp.min(jnp.where(score == mx, lane_f, float(LANES)), axis=1, keepdims=True)
        hit = lane_f == idx
        if k == SEL_TOPK - 2:
            hit = hit & (t_abs < SEL_LEN)
        selm = jnp.where(hit, 1.0, selm)
        score = jnp.where(hit, -jnp.inf, score)
    negsel = pltpu.roll(jnp.where(lane < n_sel, (selm - 1.0) * MASK_BIG, 0.0), SEL_LANE0, 1)
    lhs = [(bh + negsel).astype(BF16) for bh in base]

    def process(jobs, first):
        ss = [post(_dot_nt(lhs[h], kt)) for (_, h, kt, _, post) in jobs]
        ps, alphas = [], []
        for (br, h, _, _, _), sh in zip(jobs, ss):
            mx = jnp.max(sh, axis=1, keepdims=True)
            if first:
                m_new = jnp.broadcast_to(mx, (TN, LANES))
                alphas.append(None)
            else:
                m_old = m_ref[br, h]
                m_new = jnp.maximum(m_old, mx)
                alphas.append(jnp.exp2(m_old - m_new))
            m_ref[br, h] = m_new
            ps.append(jnp.exp2(sh - jnp.concatenate([m_new] * (TN // LANES), axis=1)).astype(BF16))
        for (br, h, _, vt, _), p, alpha in zip(jobs, ps, alphas):
            pv = _dot(p, vt)
            acc_ref[br, h] = pv if first else alpha * acc_ref[br, h] + pv

    def finish(br):
        outs = []
        for h in range(hg):
            acc = acc_ref[br, h]
            outs.append(acc / pltpu.roll(acc, HEAD_DIM, 1))
        return outs

    kxs, vxs = (ksx_ref, kwx_ref), (vsx_ref, vwx_ref)
    causal = kcol <= krow
    inside = kcol > krow

    def tile_jobs(branches, off, post):
        return [(br, h, kxs[br][pl.ds(off, TN), :], vxs[br][pl.ds(off, TN), :], functools.partial(post, h))
                for br in branches for h in range(hg)]

    off_i = pl.multiple_of(i * TN, TN)
    process(tile_jobs((0, 1), off_i, lambda h, s: jnp.where(causal, s + tzd_ref[h, 0], NEG)), True)

    @pl.when(i >= 1)
    def _():
        off = pl.multiple_of(jnp.maximum(i - 1, 0) * TN, TN)
        process(tile_jobs((0, 1), off, lambda h, s: s + tzd_ref[h, 1]), False)

    @pl.when(i >= WINDOW // TN)
    def _():
        off = pl.multiple_of(jnp.maximum(i - WINDOW // TN, 0) * TN, TN)
        jobs = (tile_jobs((1,), off, lambda h, s: jnp.where(inside, s, NEG))
                + tile_jobs((0,), 0, lambda h, s: s))
        process(jobs, False)

    def sel_body(j, carry):
        process(tile_jobs((0,), pl.multiple_of(j * TN, TN), lambda h, s: s), False)
        return carry

    lax.fori_loop(1, jnp.maximum(i - 1, 1), sel_body, 0)
    o_sel = finish(0)
    o_win = finish(1)

    gates = 1.0 / (1.0 + jnp.exp(-_dot(g_ref[0], gexp_ref[0])))
    width = hg * HEAD_DIM
    for c in range(hg // 2):
        cl = slice(c * LANES, (c + 1) * LANES)
        g_cmp, g_sel, g_win = (gates[:, br * width + c * LANES:br * width + (c + 1) * LANES] for br in range(3))
        he, ho = 2 * c, 2 * c + 1
        out = (g_cmp * jnp.where(lo, o_cmp[hrows(he)], o_cmp[hrows(ho)])
               + g_sel * jnp.where(lo, o_sel[he], pltpu.roll(o_sel[ho], HEAD_DIM, 1))
               + g_win * jnp.where(lo, o_win[he], pltpu.roll(o_win[ho], HEAD_DIM, 1)))
        o_ref[0, :, cl] = out.astype(BF16)


def _nsa(p3, kvc, bias_c, tzd, ov, qaux, kaux, gexp):
    b, t, _ = p3.shape
    g = NSA_KV_HEADS
    nq = t // TN
    seq = lambda col: pl.BlockSpec((1, t, LANES), lambda bi, gi, i, col=col: (bi, 0, col))
    return pl.pallas_call(
        _nsa_kernel,
        grid=(b, g, nq),
        in_specs=[pl.BlockSpec((1, TN, 2 * LANES), lambda bi, gi, i: (bi, i, _C_QN // 2 + gi)),
                  seq(_C_KS), seq(_C_VS), seq(_C_KW), seq(_C_VW),
                  pl.BlockSpec((1, TN, LANES), lambda bi, gi, i: (bi, i, _C_GATE)),
                  pl.BlockSpec((1,) + gexp.shape[1:], lambda bi, gi, i: (gi, 0, 0)),
                  pl.BlockSpec((1, 1, LANES, LANES), lambda bi, gi, i: (bi, gi, 0, 0)),
                  pl.BlockSpec((1, 1, LANES, LANES), lambda bi, gi, i: (bi, 2 + gi, 0, 0)),
                  pl.BlockSpec((NSA_GROUP, 1, TN, LANES), lambda bi, gi, i: (gi, i, 0, 0)),
                  pl.BlockSpec((NSA_GROUP, 2, TN, TN), lambda bi, gi, i: (gi, 0, 0, 0)),
                  pl.BlockSpec((LANES, LANES), lambda bi, gi, i: (0, 0)),
                  pl.BlockSpec((NSA_GROUP, 1, LANES), lambda bi, gi, i: (gi, 0, 0)),
                  pl.BlockSpec((2, t, LANES), lambda bi, gi, i: (0, 0, 0))],
        out_specs=pl.BlockSpec((1, TN, 2 * LANES), lambda bi, gi, i: (bi, i, gi)),
        out_shape=jax.ShapeDtypeStruct((b, t, NSA_HEADS * HEAD_DIM), BF16),
        scratch_shapes=[pltpu.VMEM((t, LANES), BF16)] * 4
                       + [pltpu.VMEM((2, NSA_GROUP, TN, LANES), F32)] * 2,
        compiler_params=pltpu.CompilerParams(dimension_semantics=("parallel", "parallel", "arbitrary"),
                                             vmem_limit_bytes=VMEM_LIMIT),
        name="nsa",
    )(p3, p3, p3, p3, p3, p3, gexp, kvc, kvc, bias_c, tzd, ov, qaux, kaux)


def _sb_kernel(q_ref, k_ref, v_ref, u_ref, o_ref, acc_ref, carry_ref):
    i = pl.program_id(1)
    npair = SB_HEADS // 2
    lane = lax.broadcasted_iota(jnp.int32, (TS, LANES), 1)
    lo = lane < HEAD_DIM
    lane2 = lax.broadcasted_iota(jnp.int32, (2 * TS, TS), 1)
    row2 = lax.broadcasted_iota(jnp.int32, (2 * TS, TS), 0) & (TS - 1)
    causal = lane2 < row2

    qs = []
    for c in range(npair):
        q = q_ref[0, :, c * LANES:(c + 1) * LANES].astype(F32)
        qs.append(jnp.concatenate([jnp.where(lo, q, 0.0), jnp.where(lo, 0.0, q)], axis=0).astype(BF16))

    acc_ref[...] = jnp.zeros(acc_ref.shape, F32)
    carry_ref[...] = jnp.zeros(carry_ref.shape, F32)

    def tile(j, diag):
        off = pl.multiple_of(j * TS, TS)
        pairs = range(npair)
        cols = [slice(c * LANES, (c + 1) * LANES) for c in pairs]
        zs = [_dot_nt(qs[c], k_ref[0, pl.ds(off, TS), cols[c]]) for c in pairs]
        log_betas, log_keeps, k_bfs = [], [], []
        for c in pairs:
            z = zs[c]
            log_beta = jnp.minimum(z, 0.0) - jnp.log(1.0 + jnp.exp2(-jnp.abs(z))) * LOG2E
            log_keep = log_beta - z
            if diag:
                log_keep = jnp.where(causal, log_keep, 0.0)
            log_betas.append(log_beta)
            log_keeps.append(log_keep)
            k_bfs.append(log_keep.astype(BF16))
        laters = [_dot(k_bfs[c], u_ref[...]) for c in pairs]
        a_s = []
        for c in pairs:
            carry = carry_ref[c]
            a = jnp.exp2(log_betas[c] + laters[c] + jnp.concatenate([carry] * (TS // LANES), axis=1))
            if diag:
                a = jnp.where(causal, a, 0.0)
            a_s.append(a.astype(BF16))
            carry_ref[c] = carry + jnp.sum(log_keeps[c], axis=1, keepdims=True)
        for c in pairs:
            acc_ref[c] += _dot(a_s[c], v_ref[0, pl.ds(off, TS), cols[c]])

    tile(i, True)

    def body(n, c):
        tile(i - 1 - n, False)
        return c

    lax.fori_loop(0, i, body, 0)
    for c in range(npair):
        o = acc_ref[c]
        o_ref[0, :, c * LANES:(c + 1) * LANES] = jnp.where(lo, o[:TS], o[TS:]).astype(BF16)


def _sb(p3, u):
    b, t, _ = p3.shape
    width = SB_HEADS * HEAD_DIM
    return pl.pallas_call(
        _sb_kernel,
        grid=(b, t // TS),
        in_specs=[pl.BlockSpec((1, TS, width), lambda bi, i: (bi, i, _C_SBQ * LANES // width)),
                  pl.BlockSpec((1, t, width), lambda bi, i: (bi, 0, _C_SBK * LANES // width)),
                  pl.BlockSpec((1, t, width), lambda bi, i: (bi, 0, _C_SBV * LANES // width)),
                  pl.BlockSpec((TS, TS), lambda bi, i: (0, 0))],
        out_specs=pl.BlockSpec((1, TS, width), lambda bi, i: (bi, i, 0)),
        out_shape=jax.ShapeDtypeStruct((b, t, width), BF16),
        scratch_shapes=[pltpu.VMEM((SB_HEADS // 2, 2 * TS, LANES), F32),
                        pltpu.VMEM((SB_HEADS // 2, 2 * TS, LANES), F32)],
        compiler_params=pltpu.CompilerParams(dimension_semantics=("parallel", "arbitrary"),
                                             vmem_limit_bytes=VMEM_LIMIT),
        name="sb",
    )(p3, p3, p3, u)


def _tail_kernel(x_ref, on_ref, os_ref, ga_ref, wgm_ref, wun_ref, wus_ref, wo_ref, gm_ref,
                 w1_ref, w2_ref, gf_ref, o_ref, *, last_layer):
    x = x_ref[...]
    d = x.shape[1]
    h = _rms(x, ga_ref[...]).astype(BF16)
    gm = 1.0 / (1.0 + jnp.exp(-_dot(h, wgm_ref[...])))
    mixed = gm[:, :d] * _dot(on_ref[...], wun_ref[...]) + gm[:, d:] * _dot(os_ref[...], wus_ref[...])
    x1 = x + _dot(mixed.astype(BF16), wo_ref[...])
    h2 = _rms(x1, gm_ref[...]).astype(BF16)
    dff = w1_ref.shape[1]
    fc = 512
    acc = jnp.zeros_like(x1)
    for f in range(0, dff, fc):
        u = jnp.maximum(_dot(h2, w1_ref[:, f:f + fc]), 0.0)
        acc = acc + _dot((u * u).astype(BF16), w2_ref[f:f + fc, :])
    x2 = x1 + acc
    o_ref[...] = _rms(x2, gf_ref[...]) if last_layer else x2


def _tail(x2, o_nsa, o_sb, g_attn, wgm, wun, wus, wo, g_mlp, w1, w2, g_final, last_layer):
    m, d = x2.shape
    tm = 256
    tok = lambda w: pl.BlockSpec((tm, w), lambda i: (i, 0))
    const = lambda a: pl.BlockSpec(a.shape, lambda i: (0, 0), pipeline_mode=pl.Buffered(1))
    return pl.pallas_call(
        functools.partial(_tail_kernel, last_layer=last_layer),
        grid=(m // tm,),
        in_specs=[tok(d), tok(o_nsa.shape[1]), tok(o_sb.shape[1]), const(g_attn), const(wgm),
                  const(wun), const(wus), const(wo), const(g_mlp), const(w1), const(w2), const(g_final)],
        out_specs=tok(d),
        out_shape=jax.ShapeDtypeStruct((m, d), F32),
        compiler_params=pltpu.CompilerParams(dimension_semantics=("parallel",),
                                             vmem_limit_bytes=VMEM_LIMIT),
        name="tail",
    )(x2, o_nsa, o_sb, g_attn, wgm, wun, wus, wo, g_mlp, w1, w2, g_final)


def _t5_bucket_np(dist):
    n = np.maximum(dist, 0)
    max_exact = N_BUCKETS // 2
    nf = np.maximum(n, 1).astype(np.float32)
    large = max_exact + (np.log(nf / np.float32(max_exact)) / np.float32(math.log(MAX_DISTANCE / max_exact))
                         * np.float32(N_BUCKETS - max_exact)).astype(np.int32)
    return np.where(n < max_exact, n, np.minimum(large, N_BUCKETS - 1)).astype(np.int32)


def _bias_tables(rel_bias, t):
    tbl = rel_bias * LOG2E
    far_idx = int(_t5_bucket_np(np.asarray(MAX_DISTANCE)))
    far = tbl[far_idx]

    def lookup(idx, table):
        onehot = jnp.asarray(idx.reshape(-1)[:, None] == np.arange(N_BUCKETS)[None, :], F32)
        out = jnp.einsum("bh,nb->hn", table, onehot, precision=lax.Precision.HIGHEST)
        return out.reshape((table.shape[1],) + idx.shape)

    qi, kj = np.arange(TN)[:, None], np.arange(TN)[None, :]
    tzd = lookup(np.stack([_t5_bucket_np(r * TN + qi - kj) for r in range(2)]), tbl - far[None, :])
    dist_c = (np.arange(t).reshape(t // TN, TN, 1) - CMP_STRIDE * np.arange(LANES)[None, None, :]
              - (CMP_LEN - 1))
    bias_c = lookup(np.where(dist_c >= 0, _t5_bucket_np(dist_c), far_idx), tbl)
    far_hi = far.astype(BF16).astype(F32)
    qaux = jnp.zeros((rel_bias.shape[1], 1, LANES), F32)
    qaux = qaux.at[:, 0, BIAS_LANE].set(far_hi).at[:, 0, BIAS_LANE + 1].set(far - far_hi)
    return tzd, bias_c, qaux


def _const_tables(t):
    c_start = np.arange(LANES) * CMP_STRIDE
    s_start = np.arange(LANES) * SEL_LEN
    n_cmp = (t - CMP_LEN) // CMP_STRIDE + 1
    n_sel = t // SEL_LEN
    ov = ((c_start[:, None] < s_start[None, :] + SEL_LEN) & (c_start[:, None] + CMP_LEN > s_start[None, :])
          & (np.arange(LANES)[:, None] < n_cmp) & (np.arange(LANES)[None, :] < n_sel))
    kaux = np.zeros((2, t, LANES), np.float32)
    kaux[0, np.arange(t), SEL_LANE0 + np.arange(t) // SEL_LEN] = 1.0
    kaux[:, :, BIAS_LANE:BIAS_LANE + 2] = 1.0
    u = np.arange(TS)[:, None] > np.arange(TS)[None, :]
    gexp = np.zeros((NSA_KV_HEADS, LANES, 3 * NSA_GROUP * HEAD_DIM), np.float32)
    for h in range(NSA_HEADS):
        for br in range(3):
            c0 = (br * NSA_GROUP + h % NSA_GROUP) * HEAD_DIM
            gexp[h // NSA_GROUP, h * 3 + br, c0:c0 + HEAD_DIM] = 1.0
    return jnp.asarray(ov, BF16), jnp.asarray(kaux, BF16), jnp.asarray(u, BF16), jnp.asarray(gexp, BF16)


def _arrange_w_in(w):
    d = w.shape[0]
    nq, nkv, ng = NSA_HEADS * HEAD_DIM, 6 * NSA_KV_HEADS * HEAD_DIM, 3 * NSA_HEADS
    nsb = SB_HEADS * HEAD_DIM
    scale = HEAD_DIM ** -0.5 * LOG2E
    q = w[:, :nq] * scale
    kv = w[:, nq:nq + nkv]
    gl = w[:, nq + nkv:nq + nkv + ng]
    sb = w[:, nq + nkv + ng:nq + nkv + ng + 3 * nsb]
    gm = w[:, nq + nkv + ng + 3 * nsb:]
    kvw = NSA_KV_HEADS * HEAD_DIM
    assert kvw == LANES and ng <= LANES
    gates = jnp.pad(gl, ((0, 0), (0, LANES - ng)))
    cols = [sb[:, :nsb] * scale, sb[:, nsb:], q, kv[:, 2 * kvw:], gates, kv[:, :2 * kvw]]
    return jnp.concatenate(cols, axis=1).astype(BF16), gm.astype(BF16)


def kernel(x, norm_attn, w_in, rel_bias, cmp_k_pe, cmp_k_w1, cmp_k_b1, cmp_k_w2, cmp_v_pe, cmp_v_w1, cmp_v_b1, cmp_v_w2, w_up_nsa, w_up_sb, w_out, norm_mlp, w_ff1, w_ff2, norm_final):
    b, t, d = x.shape
    depth = w_in.shape[0]
    assert t % TN == 0 and t % TS == 0 and WINDOW == 2 * TN and MAX_DISTANCE <= TN
    assert t // SEL_LEN <= N_BUCKETS and (t - CMP_LEN) // CMP_STRIDE + 1 < LANES
    tzd, bias_c, qaux = _bias_tables(rel_bias, t)
    ov, kaux, u, gexp = _const_tables(t)
    half = CMP_LEN // 2 * HEAD_DIM
    xf = x.reshape(b * t, d)
    for layer in range(depth):
        w_arr, w_gm = _arrange_w_in(w_in[layer])
        assert w_arr.shape[1] == _N_CHUNKS * LANES
        proj = _inproj(xf, norm_attn[layer][None, :], w_arr).reshape(b, t, _N_CHUNKS * LANES)

        kvc = proj[:, :, _C_KVC * LANES:].reshape(b, t, 4, HEAD_DIM)
        kvc = jnp.transpose(kvc, (0, 2, 1, 3)).reshape(b, 4, t // CMP_STRIDE, CMP_STRIDE * HEAD_DIM)
        pe = jnp.stack([cmp_k_pe[layer], cmp_v_pe[layer]]).reshape(2, 1, CMP_LEN * HEAD_DIM)
        w1 = jnp.stack([cmp_k_w1[layer], cmp_v_w1[layer]]).astype(BF16)
        b1 = jnp.stack([cmp_k_b1[layer], cmp_v_b1[layer]])[:, None, :]
        w2 = jnp.stack([cmp_k_w2[layer], cmp_v_w2[layer]])
        w2d = jnp.concatenate([w2, w2], axis=2).astype(BF16)
        kvc = _compress(kvc, pe[:, :, :half], pe[:, :, half:], w1[:, :half], w1[:, half:], b1, w2d)

        o_nsa = _nsa(proj, kvc, bias_c, tzd, ov, qaux, kaux, gexp)
        o_sb = _sb(proj, u)
        xf = _tail(xf, o_nsa.reshape(b * t, -1), o_sb.reshape(b * t, -1), norm_attn[layer][None, :], w_gm,
                   w_up_nsa[layer].astype(BF16), w_up_sb[layer].astype(BF16), w_out[layer].astype(BF16),
                   norm_mlp[layer][None, :], w_ff1[layer].astype(BF16), w_ff2[layer].astype(BF16),
                   norm_final[None, :], layer == depth - 1)
    return xf.reshape(b, t, d)
```

```python
import functools
import math

import numpy as np
import jax
import jax.numpy as jnp
from jax import lax
from jax.experimental import pallas as pl
from jax.experimental.pallas import tpu as pltpu

HEAD_DIM = 64
NSA_HEADS = 8
NSA_KV_HEADS = 2
NSA_GROUP = NSA_HEADS // NSA_KV_HEADS
SB_HEADS = 8
CMP_LEN = 32
CMP_STRIDE = 16
CMP_HIDDEN = 256
SEL_LEN = 64
SEL_TOPK = 4
WINDOW = 512
N_BUCKETS = 32
MAX_DISTANCE = 128
EPS = 1e-6
NEG = -1e30
FORCE_BONUS = 1e4

LANES = 128
TN = 256
TS = 256
ROWS_INPROJ = 512
ROWS_TAIL = 512
FF_CHUNK = 512
MXU_COLS = 256
SETUP_ROWS = 512
VMEM_LIMIT = 56 * 1024 * 1024
LOG2E = math.log2(math.e)
MASK_BIG = 2.0 ** 30
SEL_LANE0 = HEAD_DIM
BIAS_LANE = HEAD_DIM + N_BUCKETS

F32 = jnp.float32
BF16 = jnp.bfloat16
SIGN_BIT = np.uint32(0x80000000)

_C_SBQ = 0
_C_SBK = 4
_C_SBV = 8
_C_QN = 12
_C_KS = 16
_C_VS = 17
_C_KW = 18
_C_VW = 19
_C_GATE = 20
_C_KVC = 21
_N_CHUNKS = 23


def _dot(a, b):
    return jnp.dot(a, b, preferred_element_type=F32)


def _dot_nt(a, b):
    return lax.dot_general(a, b, (((1,), (1,)), ((), ())), preferred_element_type=F32)


def _rms(x, g):
    return x * lax.rsqrt(jnp.mean(x * x, axis=-1, keepdims=True) + EPS) * g


def _inproj_kernel(x_ref, g_ref, w_ref, o_ref):
    h = _rms(x_ref[...], g_ref[...]).astype(BF16)
    n = o_ref.shape[1]
    for c in range(0, n, MXU_COLS):
        cs = slice(c, min(c + MXU_COLS, n))
        o_ref[:, cs] = _dot(h, w_ref[:, cs]).astype(BF16)


def _inproj(x2, g, w):
    m, d = x2.shape
    n = w.shape[1]
    tm = ROWS_INPROJ
    return pl.pallas_call(
        _inproj_kernel,
        grid=(m // tm,),
        in_specs=[pl.BlockSpec((tm, d), lambda i: (i, 0)),
                  pl.BlockSpec((1, d), lambda i: (0, 0)),
                  pl.BlockSpec((d, n), lambda i: (0, 0))],
        out_specs=pl.BlockSpec((tm, n), lambda i: (i, 0)),
        out_shape=jax.ShapeDtypeStruct((m, n), BF16),
        compiler_params=pltpu.CompilerParams(dimension_semantics=("parallel",),
                                             vmem_limit_bytes=VMEM_LIMIT),
        name="inproj",
    )(x2, g, w)


def _gelu_tanh(x):
    return 0.5 * x * (1.0 + jnp.tanh(math.sqrt(2.0 / math.pi) * (x + 0.044715 * (x * x * x))))


def _compress_kernel(x_ref, pea_ref, peb_ref, w1a_ref, w1b_ref, b1_ref, w2_ref, o_ref):
    for s in range(4):
        r = s // 2
        x = x_ref[0, s].astype(F32)
        a = _dot((x + pea_ref[r]).astype(BF16), w1a_ref[r])
        b = _dot((x + peb_ref[r]).astype(BF16), w1b_ref[r])
        hid = a + pltpu.roll(b, shift=b.shape[0] - 1, axis=0) + b1_ref[r]
        o_ref[0, s] = _dot(_gelu_tanh(hid).astype(BF16), w2_ref[r]).astype(BF16)


def _compress(xc, pea, peb, w1a, w1b, b1, w2d):
    b = xc.shape[0]
    nchunk, width = xc.shape[2], xc.shape[3]
    full = lambda shape: pl.BlockSpec(shape, lambda i: (0,) * len(shape))
    return pl.pallas_call(
        _compress_kernel,
        grid=(b,),
        in_specs=[pl.BlockSpec((1, 4, nchunk, width), lambda i: (i, 0, 0, 0)),
                  full(pea.shape), full(peb.shape), full(w1a.shape), full(w1b.shape),
                  full(b1.shape), full(w2d.shape)],
        out_specs=pl.BlockSpec((1, 4, nchunk, LANES), lambda i: (i, 0, 0, 0)),
        out_shape=jax.ShapeDtypeStruct((b, 4, nchunk, LANES), BF16),
        compiler_params=pltpu.CompilerParams(dimension_semantics=("parallel",),
                                             vmem_limit_bytes=VMEM_LIMIT),
        name="compress",
    )(xc, pea, peb, w1a, w1b, b1, w2d)


def _nsa_kernel(q_ref, ks_ref, vs_ref, kw_ref, vw_ref, g_ref, gexp_ref, kvc_ref,
                bc_ref, tzd_ref, ov_ref, qaux_ref, kaux_ref, o_ref,
                ksx_ref, vsx_ref, kwx_ref, vwx_ref, m_ref, acc_ref):
    i = pl.program_id(1)
    ng, hg, nh = NSA_KV_HEADS, NSA_GROUP, NSA_HEADS
    t = ks_ref.shape[1]
    lane = lax.broadcasted_iota(jnp.int32, (TN, LANES), 1)
    row = lax.broadcasted_iota(jnp.int32, (TN, LANES), 0)
    lo = lane < HEAD_DIM
    t_abs = i * TN + row
    krow = lax.broadcasted_iota(jnp.int32, (TN, TN), 0)
    kcol = lax.broadcasted_iota(jnp.int32, (TN, TN), 1)

    def hrows(h):
        return slice(h * TN, (h + 1) * TN)

    @pl.when(i == 0)
    def _():
        ch = SETUP_ROWS
        lo_c = lax.broadcasted_iota(jnp.int32, (ch, LANES), 1) < HEAD_DIM
        for r0 in range(0, t, ch):
            rs = slice(r0, r0 + ch)
            for g in range(ng):
                def own(ref):
                    x = ref[0, rs].astype(F32)
                    return x if g == 0 else pltpu.roll(x, HEAD_DIM, 1)

                ksx_ref[g, rs] = jnp.where(lo_c, own(ks_ref), kaux_ref[0, rs].astype(F32)).astype(BF16)
                kwx_ref[g, rs] = jnp.where(lo_c, own(kw_ref), kaux_ref[1, rs].astype(F32)).astype(BF16)
                vsx_ref[g, rs] = jnp.where(lo_c, own(vs_ref), 1.0).astype(BF16)
                vwx_ref[g, rs] = jnp.where(lo_c, own(vw_ref), 1.0).astype(BF16)

    base = []
    for c in range(nh // 2):
        qc = q_ref[0, :, c * LANES:(c + 1) * LANES].astype(F32)
        for h, qh in ((2 * c, qc), (2 * c + 1, pltpu.roll(qc, HEAD_DIM, 1))):
            base.append(jnp.where(lo, qh, qaux_ref[h]))

    lo_c = lax.broadcasted_iota(jnp.int32, (LANES, LANES), 1) < HEAD_DIM
    valid_c = (t_abs >= CMP_STRIDE * lane + (CMP_LEN - 1)) & (lane < LANES - 1)
    s_cmp = [_dot_nt(jnp.concatenate(base[g * hg:(g + 1) * hg], axis=0).astype(BF16),
                     jnp.where(lo_c, kvc_ref[0, g].astype(F32), 0.0).astype(BF16)) for g in range(ng)]
    ps = []
    for h in range(nh):
        sh = jnp.where(valid_c, s_cmp[h // hg][hrows(h % hg)] + bc_ref[h, 0], NEG)
        e = jnp.exp2(sh - jnp.max(sh, axis=1, keepdims=True))
        ps.append(jnp.where(valid_c, e, 0.0) / jnp.sum(e, axis=1, keepdims=True))
    o_cmp = [_dot(jnp.concatenate(ps[g * hg:(g + 1) * hg], axis=0).astype(BF16), kvc_ref[0, ng + g])
             for g in range(ng)]

    n_sel = t // SEL_LEN
    forced = (lane == t_abs // SEL_LEN) | (lane == 0)
    sel_ok = lane * SEL_LEN <= t_abs
    lane_f = lane.astype(F32)
    scores, selms = [], []
    for g in range(ng):
        pg = ps[g * hg:(g + 1) * hg]
        psum = (pg[0] + pg[1]) + (pg[2] + pg[3])
        p_hi = psum.astype(BF16)
        p_lo = (psum - p_hi.astype(F32)).astype(BF16)
        imp = _dot(p_hi, ov_ref[...]) + _dot(p_lo, ov_ref[...])
        scores.append(jnp.where(forced | (lane >= n_sel), -jnp.inf, jnp.where(sel_ok, imp, -FORCE_BONUS)))
        selms.append(jnp.where(forced, 1.0, 0.0))

    def process(jobs, first):
        ss = [post(_dot_nt(lhs[br][h], kt)) for (br, h, kt, _, post) in jobs]
        ps, alphas = [], []
        for (br, h, _, _, _), sh in zip(jobs, ss):
            mx = jnp.max(sh, axis=1, keepdims=True)
            if first:
                m_new = jnp.broadcast_to(mx, (TN, LANES))
                alphas.append(None)
            else:
                m_old = m_ref[br, h]
                m_new = jnp.maximum(m_old, mx)
                alphas.append(jnp.exp2(m_old - m_new))
            m_ref[br, h] = m_new
            ps.append(jnp.exp2(sh - jnp.concatenate([m_new] * (TN // LANES), axis=1)).astype(BF16))
        for (br, h, _, vt, _), p, alpha in zip(jobs, ps, alphas):
            pv = _dot(p, vt)
            acc_ref[br, h] = pv if first else alpha * acc_ref[br, h] + pv

    def pair_out(br, he, ho):
        acc_e, acc_o = acc_ref[br, he], acc_ref[br, ho]
        num = jnp.where(lo, acc_e, pltpu.roll(acc_o, HEAD_DIM, 1))
        den = jnp.where(lo, pltpu.roll(acc_e, HEAD_DIM, 1), acc_o)
        return num / den

    kxs, vxs = (ksx_ref, kwx_ref), (vsx_ref, vwx_ref)
    causal = kcol <= krow
    inside = kcol > krow

    def tile_jobs(branches, off, post):
        return [(br, h, kxs[br][h // hg, pl.ds(off, TN), :], vxs[br][h // hg, pl.ds(off, TN), :],
                 functools.partial(post, h)) for br in branches for h in range(nh)]

    off_i = pl.multiple_of(i * TN, TN)
    diag_post = lambda h, s: jnp.where(causal, s + tzd_ref[h, 0], NEG)
    lhs = [None, [bh.astype(BF16) for bh in base]]
    win_jobs = tile_jobs((1,), off_i, diag_post)
    per_round = -(-nh // (SEL_TOPK - 1))
    for k in range(SEL_TOPK - 1):
        for g in range(ng):
            mx = jnp.max(scores[g], axis=1, keepdims=True)
            idx = jnp.min(jnp.where(scores[g] == mx, lane_f, float(LANES)), axis=1, keepdims=True)
            hit = lane_f == idx
            if k == SEL_TOPK - 2:
                hit = hit & (t_abs < SEL_LEN)
            selms[g] = jnp.where(hit, 1.0, selms[g])
            scores[g] = jnp.where(hit, -jnp.inf, scores[g])
        process(win_jobs[k * per_round:(k + 1) * per_round], True)
    negsel = [pltpu.roll(jnp.where(lane < n_sel, (sm - 1.0) * MASK_BIG, 0.0), SEL_LANE0, 1) for sm in selms]
    lhs[0] = [(base[h] + negsel[h // hg]).astype(BF16) for h in range(nh)]
    process(tile_jobs((0,), off_i, diag_post), True)

    @pl.when(i >= 1)
    def _():
        off = pl.multiple_of(jnp.maximum(i - 1, 0) * TN, TN)
        process(tile_jobs((0, 1), off, lambda h, s: s + tzd_ref[h, 1]), False)

    @pl.when(i >= WINDOW // TN)
    def _():
        off = pl.multiple_of(jnp.maximum(i - WINDOW // TN, 0) * TN, TN)
        jobs = (tile_jobs((1,), off, lambda h, s: jnp.where(inside, s, NEG))
                + tile_jobs((0,), 0, lambda h, s: s))
        process(jobs, False)

    def sel_body(j, carry):
        process(tile_jobs((0,), pl.multiple_of(j * TN, TN), lambda h, s: s), False)
        return carry

    lax.fori_loop(1, jnp.maximum(i - 1, 1), sel_body, 0)

    width = hg * HEAD_DIM
    for g in range(ng):
        gates = 1.0 / (1.0 + jnp.exp(-_dot(g_ref[0], gexp_ref[g])))
        for c in range(hg // 2):
            g_cmp, g_sel, g_win = (gates[:, br * width + c * LANES:br * width + (c + 1) * LANES] for br in range(3))
            he, ho = g * hg + 2 * c, g * hg + 2 * c + 1
            out = (g_cmp * jnp.where(lo, o_cmp[g][hrows(2 * c)], o_cmp[g][hrows(2 * c + 1)])
                   + g_sel * pair_out(0, he, ho) + g_win * pair_out(1, he, ho))
            o_ref[0, :, (he // 2) * LANES:(he // 2 + 1) * LANES] = out.astype(BF16)


def _nsa(p3, kvc, bias_c, tzd, ov, qaux, kaux, gexp):
    b, t, _ = p3.shape
    nq = t // TN
    width = NSA_HEADS * HEAD_DIM
    seq = lambda col: pl.BlockSpec((1, t, LANES), lambda bi, i, col=col: (bi, 0, col))
    const = lambda a: pl.BlockSpec(a.shape, lambda bi, i: (0,) * a.ndim, pipeline_mode=pl.Buffered(1))
    return pl.pallas_call(
        _nsa_kernel,
        grid=(b, nq),
        in_specs=[pl.BlockSpec((1, TN, width), lambda bi, i: (bi, i, _C_QN * LANES // width)),
                  seq(_C_KS), seq(_C_VS), seq(_C_KW), seq(_C_VW),
                  pl.BlockSpec((1, TN, LANES), lambda bi, i: (bi, i, _C_GATE)),
                  const(gexp),
                  pl.BlockSpec((1,) + kvc.shape[1:], lambda bi, i: (bi, 0, 0, 0)),
                  pl.BlockSpec((NSA_HEADS, 1, TN, LANES), lambda bi, i: (0, i, 0, 0)),
                  const(tzd), const(ov), const(qaux), const(kaux)],
        out_specs=pl.BlockSpec((1, TN, width), lambda bi, i: (bi, i, 0)),
        out_shape=jax.ShapeDtypeStruct((b, t, width), BF16),
        scratch_shapes=[pltpu.VMEM((NSA_KV_HEADS, t, LANES), BF16)] * 4
                       + [pltpu.VMEM((2, NSA_HEADS, TN, LANES), F32)] * 2,
        compiler_params=pltpu.CompilerParams(dimension_semantics=("parallel", "arbitrary"),
                                             vmem_limit_bytes=VMEM_LIMIT),
        name="nsa",
    )(p3, p3, p3, p3, p3, p3, gexp, kvc, bias_c, tzd, ov, qaux, kaux)


def _sb_kernel(q_ref, k_ref, v_ref, u_ref, o_ref, acc_ref, carry_ref):
    i = pl.program_id(1)
    nb = q_ref.shape[0]
    chains = [(bb, c) for bb in range(nb) for c in range(SB_HEADS // 2)]
    cols = [slice(c * LANES, (c + 1) * LANES) for c in range(SB_HEADS // 2)]
    lane = lax.broadcasted_iota(jnp.int32, (TS, LANES), 1)
    lo = lane < HEAD_DIM
    lane2 = lax.broadcasted_iota(jnp.int32, (2 * TS, TS), 1)
    row2 = lax.broadcasted_iota(jnp.int32, (2 * TS, TS), 0) & (TS - 1)
    causal = lane2 < row2

    qs = []
    for bb, c in chains:
        q = q_ref[bb, :, cols[c]].astype(F32)
        qs.append(jnp.concatenate([jnp.where(lo, q, 0.0), jnp.where(lo, 0.0, q)], axis=0).astype(BF16))

    acc_ref[...] = jnp.zeros(acc_ref.shape, F32)
    carry_ref[...] = jnp.zeros(carry_ref.shape, F32)

    def tile(j, diag):
        off = pl.multiple_of(j * TS, TS)
        zs = [_dot_nt(qs[n], k_ref[bb, pl.ds(off, TS), cols[c]]) for n, (bb, c) in enumerate(chains)]
        log_betas, log_keeps, k_bfs = [], [], []
        for z in zs:
            neg_abs = lax.bitcast_convert_type(lax.bitcast_convert_type(z, jnp.uint32) | SIGN_BIT, F32)
            log_beta = jnp.minimum(z, 0.0) - jnp.log(1.0 + jnp.exp2(neg_abs)) * LOG2E
            log_keep = log_beta - z
            if diag:
                log_keep = jnp.where(causal, log_keep, 0.0)
            log_betas.append(log_beta)
            log_keeps.append(log_keep)
            k_bfs.append(log_keep.astype(BF16))
        laters = [_dot(k_bf, u_ref[...]) for k_bf in k_bfs]
        a_s = []
        for n, (bb, c) in enumerate(chains):
            carry = carry_ref[bb, c]
            a = jnp.exp2(log_betas[n] + laters[n] + jnp.concatenate([carry] * (TS // LANES), axis=1))
            if diag:
                a = jnp.where(causal, a, 0.0)
            a_s.append(a.astype(BF16))
            carry_ref[bb, c] = carry + jnp.sum(log_keeps[n], axis=1, keepdims=True)
        for n, (bb, c) in enumerate(chains):
            acc_ref[bb, c] += _dot(a_s[n], v_ref[bb, pl.ds(off, TS), cols[c]])

    tile(i, True)

    def body(n, carry):
        tile(i - 1 - n, False)
        return carry

    lax.fori_loop(0, i, body, 0)
    for bb, c in chains:
        o = acc_ref[bb, c]
        o_ref[bb, :, cols[c]] = jnp.where(lo, o[:TS], o[TS:]).astype(BF16)


def _sb(p3, u):
    b, t, _ = p3.shape
    width = SB_HEADS * HEAD_DIM
    nb = 2 if b % 2 == 0 else 1
    return pl.pallas_call(
        _sb_kernel,
        grid=(b // nb, t // TS),
        in_specs=[pl.BlockSpec((nb, TS, width), lambda bi, i: (bi, i, _C_SBQ * LANES // width)),
                  pl.BlockSpec((nb, t, width), lambda bi, i: (bi, 0, _C_SBK * LANES // width)),
                  pl.BlockSpec((nb, t, width), lambda bi, i: (bi, 0, _C_SBV * LANES // width)),
                  pl.BlockSpec((TS, TS), lambda bi, i: (0, 0))],
        out_specs=pl.BlockSpec((nb, TS, width), lambda bi, i: (bi, i, 0)),
        out_shape=jax.ShapeDtypeStruct((b, t, width), BF16),
        scratch_shapes=[pltpu.VMEM((nb, SB_HEADS // 2, 2 * TS, LANES), F32),
                        pltpu.VMEM((nb, SB_HEADS // 2, 2 * TS, LANES), F32)],
        compiler_params=pltpu.CompilerParams(dimension_semantics=("parallel", "arbitrary"),
                                             vmem_limit_bytes=VMEM_LIMIT),
        name="sb",
    )(p3, p3, p3, u)


def _tail_kernel(x_ref, on_ref, os_ref, ga_ref, wgm_ref, wun_ref, wus_ref, wo_ref, gm_ref,
                 w1_ref, w2_ref, gf_ref, o_ref, *, last_layer):
    x = x_ref[...]
    d = x.shape[1]
    h = _rms(x, ga_ref[...]).astype(BF16)
    gm = 1.0 / (1.0 + jnp.exp(-_dot(h, wgm_ref[...])))
    mixed = gm[:, :d] * _dot(on_ref[...], wun_ref[...]) + gm[:, d:] * _dot(os_ref[...], wus_ref[...])
    x1 = x + _dot(mixed.astype(BF16), wo_ref[...])
    h2 = _rms(x1, gm_ref[...]).astype(BF16)
    dff = w1_ref.shape[1]
    acc = jnp.zeros_like(x1)
    for f in range(0, dff, FF_CHUNK):
        u = jnp.maximum(_dot(h2, w1_ref[:, f:f + FF_CHUNK]), 0.0)
        acc = acc + _dot((u * u).astype(BF16), w2_ref[f:f + FF_CHUNK, :])
    x2 = x1 + acc
    o_ref[...] = _rms(x2, gf_ref[...]) if last_layer else x2


def _tail(x2, o_nsa, o_sb, g_attn, wgm, wun, wus, wo, g_mlp, w1, w2, g_final, last_layer):
    m, d = x2.shape
    tm = ROWS_TAIL
    tok = lambda w: pl.BlockSpec((tm, w), lambda i: (i, 0))
    const = lambda a: pl.BlockSpec(a.shape, lambda i: (0, 0), pipeline_mode=pl.Buffered(1))
    return pl.pallas_call(
        functools.partial(_tail_kernel, last_layer=last_layer),
        grid=(m // tm,),
        in_specs=[tok(d), tok(o_nsa.shape[1]), tok(o_sb.shape[1]), const(g_attn), const(wgm),
                  const(wun), const(wus), const(wo), const(g_mlp), const(w1), const(w2), const(g_final)],
        out_specs=tok(d),
        out_shape=jax.ShapeDtypeStruct((m, d), F32),
        compiler_params=pltpu.CompilerParams(dimension_semantics=("parallel",),
                                             vmem_limit_bytes=VMEM_LIMIT),
        name="tail",
    )(x2, o_nsa, o_sb, g_attn, wgm, wun, wus, wo, g_mlp, w1, w2, g_final)


def _t5_bucket_np(dist):
    n = np.maximum(dist, 0)
    max_exact = N_BUCKETS // 2
    nf = np.maximum(n, 1).astype(np.float32)
    large = max_exact + (np.log(nf / np.float32(max_exact)) / np.float32(math.log(MAX_DISTANCE / max_exact))
                         * np.float32(N_BUCKETS - max_exact)).astype(np.int32)
    return np.where(n < max_exact, n, np.minimum(large, N_BUCKETS - 1)).astype(np.int32)


def _bias_tables(rel_bias, t):
    tbl = rel_bias * LOG2E
    far_idx = int(_t5_bucket_np(np.asarray(MAX_DISTANCE)))
    far = tbl[far_idx]

    def lookup(idx, table):
        onehot = jnp.asarray(idx.reshape(-1)[:, None] == np.arange(N_BUCKETS)[None, :], F32)
        out = jnp.einsum("bh,nb->hn", table, onehot, precision=lax.Precision.HIGHEST)
        return out.reshape((table.shape[1],) + idx.shape)

    qi, kj = np.arange(TN)[:, None], np.arange(TN)[None, :]
    tzd = lookup(np.stack([_t5_bucket_np(r * TN + qi - kj) for r in range(2)]), tbl - far[None, :])
    dist_c = (np.arange(t).reshape(t // TN, TN, 1) - CMP_STRIDE * np.arange(LANES)[None, None, :]
              - (CMP_LEN - 1))
    bias_c = lookup(np.where(dist_c >= 0, _t5_bucket_np(dist_c), far_idx), tbl)
    far_hi = far.astype(BF16).astype(F32)
    qaux = jnp.zeros((rel_bias.shape[1], 1, LANES), F32)
    qaux = qaux.at[:, 0, BIAS_LANE].set(far_hi).at[:, 0, BIAS_LANE + 1].set(far - far_hi)
    return tzd, bias_c, qaux


def _const_tables(t):
    c_start = np.arange(LANES) * CMP_STRIDE
    s_start = np.arange(LANES) * SEL_LEN
    n_cmp = (t - CMP_LEN) // CMP_STRIDE + 1
    n_sel = t // SEL_LEN
    ov = ((c_start[:, None] < s_start[None, :] + SEL_LEN) & (c_start[:, None] + CMP_LEN > s_start[None, :])
          & (np.arange(LANES)[:, None] < n_cmp) & (np.arange(LANES)[None, :] < n_sel))
    kaux = np.zeros((2, t, LANES), np.float32)
    kaux[0, np.arange(t), SEL_LANE0 + np.arange(t) // SEL_LEN] = 1.0
    kaux[:, :, BIAS_LANE:BIAS_LANE + 2] = 1.0
    u = np.arange(TS)[:, None] > np.arange(TS)[None, :]
    gexp = np.zeros((NSA_KV_HEADS, LANES, 3 * NSA_GROUP * HEAD_DIM), np.float32)
    for h in range(NSA_HEADS):
        for br in range(3):
            c0 = (br * NSA_GROUP + h % NSA_GROUP) * HEAD_DIM
            gexp[h // NSA_GROUP, h * 3 + br, c0:c0 + HEAD_DIM] = 1.0
    return jnp.asarray(ov, BF16), jnp.asarray(kaux, BF16), jnp.asarray(u, BF16), jnp.asarray(gexp, BF16)


def _arrange_w_in(w):
    d = w.shape[0]
    nq, nkv, ng = NSA_HEADS * HEAD_DIM, 6 * NSA_KV_HEADS * HEAD_DIM, 3 * NSA_HEADS
    nsb = SB_HEADS * HEAD_DIM
    scale = HEAD_DIM ** -0.5 * LOG2E
    q = w[:, :nq] * scale
    kv = w[:, nq:nq + nkv]
    gl = w[:, nq + nkv:nq + nkv + ng]
    sb = w[:, nq + nkv + ng:nq + nkv + ng + 3 * nsb]
    gm = w[:, nq + nkv + ng + 3 * nsb:]
    kvw = NSA_KV_HEADS * HEAD_DIM
    assert kvw == LANES and ng <= LANES
    gates = jnp.pad(gl, ((0, 0), (0, LANES - ng)))
    cols = [sb[:, :nsb] * scale, sb[:, nsb:], q, kv[:, 2 * kvw:], gates, kv[:, :2 * kvw]]
    return jnp.concatenate(cols, axis=1).astype(BF16), gm.astype(BF16)


def kernel(x, norm_attn, w_in, rel_bias, cmp_k_pe, cmp_k_w1, cmp_k_b1, cmp_k_w2, cmp_v_pe, cmp_v_w1, cmp_v_b1, cmp_v_w2, w_up_nsa, w_up_sb, w_out, norm_mlp, w_ff1, w_ff2, norm_final):
    b, t, d = x.shape
    depth = w_in.shape[0]
    assert t % TN == 0 and t % TS == 0 and WINDOW == 2 * TN and MAX_DISTANCE <= TN
    assert t // SEL_LEN <= N_BUCKETS and (t - CMP_LEN) // CMP_STRIDE + 1 < LANES
    assert (b * t) % ROWS_INPROJ == 0 and (b * t) % ROWS_TAIL == 0
    tzd, bias_c, qaux = _bias_tables(rel_bias, t)
    ov, kaux, u, gexp = _const_tables(t)
    half = CMP_LEN // 2 * HEAD_DIM
    xf = x.reshape(b * t, d)
    for layer in range(depth):
        w_arr, w_gm = _arrange_w_in(w_in[layer])
        assert w_arr.shape[1] == _N_CHUNKS * LANES
        proj = _inproj(xf, norm_attn[layer][None, :], w_arr).reshape(b, t, _N_CHUNKS * LANES)

        kvc = proj[:, :, _C_KVC * LANES:].reshape(b, t, 4, HEAD_DIM)
        kvc = jnp.transpose(kvc, (0, 2, 1, 3)).reshape(b, 4, t // CMP_STRIDE, CMP_STRIDE * HEAD_DIM)
        pe = jnp.stack([cmp_k_pe[layer], cmp_v_pe[layer]]).reshape(2, 1, CMP_LEN * HEAD_DIM)
        w1 = jnp.stack([cmp_k_w1[layer], cmp_v_w1[layer]]).astype(BF16)
        b1 = jnp.stack([cmp_k_b1[layer], cmp_v_b1[layer]])[:, None, :]
        w2 = jnp.stack([cmp_k_w2[layer], cmp_v_w2[layer]])
        w2d = jnp.concatenate([w2, w2], axis=2).astype(BF16)
        kvc = _compress(kvc, pe[:, :, :half], pe[:, :, half:], w1[:, :half], w1[:, half:], b1, w2d)

        o_nsa = _nsa(proj, kvc, bias_c, tzd, ov, qaux, kaux, gexp)
        o_sb = _sb(proj, u)
        xf = _tail(xf, o_nsa.reshape(b * t, -1), o_sb.reshape(b * t, -1), norm_attn[layer][None, :], w_gm,
                   w_up_nsa[layer].astype(BF16), w_up_sb[layer].astype(BF16), w_out[layer].astype(BF16),
                   norm_mlp[layer][None, :], w_ff1[layer].astype(BF16), w_ff2[layer].astype(BF16),
                   norm_final[None, :], layer == depth - 1)
    return xf.reshape(b, t, d)
```

```python
import functools
import math

import numpy as np
import jax
import jax.numpy as jnp
from jax import lax
from jax.experimental import pallas as pl
from jax.experimental.pallas import tpu as pltpu

HEAD_DIM = 64
NSA_HEADS = 8
NSA_KV_HEADS = 2
NSA_GROUP = NSA_HEADS // NSA_KV_HEADS
SB_HEADS = 8
CMP_LEN = 32
CMP_STRIDE = 16
CMP_HIDDEN = 256
SEL_LEN = 64
SEL_TOPK = 4
WINDOW = 512
N_BUCKETS = 32
MAX_DISTANCE = 128
EPS = 1e-6
NEG = -1e30
FORCE_BONUS = 1e4

LANES = 128
TN = 256
TS = 256
ROWS_INPROJ = 512
ROWS_TAIL = 512
FF_CHUNK = 512
MXU_COLS = 256
SETUP_ROWS = 512
VMEM_LIMIT = 56 * 1024 * 1024
LOG2E = math.log2(math.e)
MASK_BIG = 2.0 ** 60
SEL_LANE0 = HEAD_DIM
BIAS_LANE = HEAD_DIM + N_BUCKETS

F32 = jnp.float32
BF16 = jnp.bfloat16
SIGN_BIT = np.uint32(0x80000000)

_C_SBQ = 0
_C_SBK = 4
_C_SBV = 8
_C_QN = 12
_C_KS = 16
_C_VS = 17
_C_KW = 18
_C_VW = 19
_C_GATE = 20
_C_KVC = 21
_N_CHUNKS = 23


def _dot(a, b):
    return jnp.dot(a, b, preferred_element_type=F32)


def _dot_nt(a, b):
    return lax.dot_general(a, b, (((1,), (1,)), ((), ())), preferred_element_type=F32)


def _rms(x, g):
    return x * lax.rsqrt(jnp.mean(x * x, axis=-1, keepdims=True) + EPS) * g


def _inproj_kernel(x_ref, g_ref, w_ref, o_ref):
    h = _rms(x_ref[...], g_ref[...]).astype(BF16)
    n = o_ref.shape[1]
    for c in range(0, n, MXU_COLS):
        cs = slice(c, min(c + MXU_COLS, n))
        o_ref[:, cs] = _dot(h, w_ref[:, cs]).astype(BF16)


def _inproj(x2, g, w):
    m, d = x2.shape
    n = w.shape[1]
    tm = ROWS_INPROJ
    return pl.pallas_call(
        _inproj_kernel,
        grid=(m // tm,),
        in_specs=[pl.BlockSpec((tm, d), lambda i: (i, 0)),
                  pl.BlockSpec((1, d), lambda i: (0, 0)),
                  pl.BlockSpec((d, n), lambda i: (0, 0))],
        out_specs=pl.BlockSpec((tm, n), lambda i: (i, 0)),
        out_shape=jax.ShapeDtypeStruct((m, n), BF16),
        compiler_params=pltpu.CompilerParams(dimension_semantics=("parallel",),
                                             vmem_limit_bytes=VMEM_LIMIT),
        name="inproj",
    )(x2, g, w)


def _gelu_tanh(x):
    return 0.5 * x * (1.0 + jnp.tanh(math.sqrt(2.0 / math.pi) * (x + 0.044715 * (x * x * x))))


def _compress_kernel(x_ref, pea_ref, peb_ref, w1a_ref, w1b_ref, b1_ref, w2_ref, o_ref):
    for s in range(4):
        r = s // 2
        x = x_ref[0, s].astype(F32)
        a = _dot((x + pea_ref[r]).astype(BF16), w1a_ref[r])
        b = _dot((x + peb_ref[r]).astype(BF16), w1b_ref[r])
        hid = a + pltpu.roll(b, shift=b.shape[0] - 1, axis=0) + b1_ref[r]
        o_ref[0, s] = _dot(_gelu_tanh(hid).astype(BF16), w2_ref[r]).astype(BF16)


def _compress(xc, pea, peb, w1a, w1b, b1, w2d):
    b = xc.shape[0]
    nchunk, width = xc.shape[2], xc.shape[3]
    full = lambda shape: pl.BlockSpec(shape, lambda i: (0,) * len(shape))
    return pl.pallas_call(
        _compress_kernel,
        grid=(b,),
        in_specs=[pl.BlockSpec((1, 4, nchunk, width), lambda i: (i, 0, 0, 0)),
                  full(pea.shape), full(peb.shape), full(w1a.shape), full(w1b.shape),
                  full(b1.shape), full(w2d.shape)],
        out_specs=pl.BlockSpec((1, 4, nchunk, LANES), lambda i: (i, 0, 0, 0)),
        out_shape=jax.ShapeDtypeStruct((b, 4, nchunk, LANES), BF16),
        compiler_params=pltpu.CompilerParams(dimension_semantics=("parallel",),
                                             vmem_limit_bytes=VMEM_LIMIT),
        name="compress",
    )(xc, pea, peb, w1a, w1b, b1, w2d)


def _nsa_kernel(q_ref, ks_ref, vs_ref, kw_ref, vw_ref, g_ref, gexp_ref, kvc_ref,
                bc_ref, tzd_ref, ovt_ref, qaux_ref, kaux_ref, o_ref,
                ksx_ref, vsx_ref, kwx_ref, vwx_ref, m_ref, acc_ref):
    i = pl.program_id(1)
    ng, hg, nh = NSA_KV_HEADS, NSA_GROUP, NSA_HEADS
    t = ks_ref.shape[1]
    lane = lax.broadcasted_iota(jnp.int32, (TN, LANES), 1)
    row = lax.broadcasted_iota(jnp.int32, (TN, LANES), 0)
    lo = lane < HEAD_DIM
    t_abs = i * TN + row
    krow = lax.broadcasted_iota(jnp.int32, (TN, TN), 0)
    kcol = lax.broadcasted_iota(jnp.int32, (TN, TN), 1)

    def hrows(h):
        return slice(h * TN, (h + 1) * TN)

    @pl.when(i == 0)
    def _():
        ch = SETUP_ROWS
        lo_c = lax.broadcasted_iota(jnp.int32, (ch, LANES), 1) < HEAD_DIM
        for r0 in range(0, t, ch):
            rs = slice(r0, r0 + ch)
            for g in range(ng):
                def own(ref):
                    x = ref[0, rs].astype(F32)
                    return x if g == 0 else pltpu.roll(x, HEAD_DIM, 1)

                ksx_ref[g, rs] = jnp.where(lo_c, own(ks_ref), kaux_ref[0, rs].astype(F32)).astype(BF16)
                kwx_ref[g, rs] = jnp.where(lo_c, own(kw_ref), kaux_ref[1, rs].astype(F32)).astype(BF16)
                vsx_ref[g, rs] = jnp.where(lo_c, own(vs_ref), 1.0).astype(BF16)
                vwx_ref[g, rs] = jnp.where(lo_c, own(vw_ref), 1.0).astype(BF16)

    base = []
    for c in range(nh // 2):
        qc = q_ref[0, :, c * LANES:(c + 1) * LANES].astype(F32)
        for h, qh in ((2 * c, qc), (2 * c + 1, pltpu.roll(qc, HEAD_DIM, 1))):
            base.append(jnp.where(lo, qh, qaux_ref[h]))

    lo_c = lax.broadcasted_iota(jnp.int32, (LANES, LANES), 1) < HEAD_DIM
    valid_c = (t_abs >= CMP_STRIDE * lane + (CMP_LEN - 1)) & (lane < LANES - 1)
    s_cmp = [_dot_nt(jnp.concatenate(base[g * hg:(g + 1) * hg], axis=0).astype(BF16),
                     jnp.where(lo_c, kvc_ref[0, g].astype(F32), 0.0).astype(BF16)) for g in range(ng)]
    ps = []
    for h in range(nh):
        sh = jnp.where(valid_c, s_cmp[h // hg][hrows(h % hg)] + bc_ref[h, 0], NEG)
        e = jnp.exp2(sh - jnp.max(sh, axis=1, keepdims=True))
        ps.append(jnp.where(valid_c, e, 0.0) / jnp.sum(e, axis=1, keepdims=True))
    o_cmp = [_dot(jnp.concatenate(ps[g * hg:(g + 1) * hg], axis=0).astype(BF16), kvc_ref[0, ng + g])
             for g in range(ng)]

    n_sel = t // SEL_LEN
    nidx = lax.broadcasted_iota(jnp.int32, (n_sel, TN), 0)
    tq = i * TN + lax.broadcasted_iota(jnp.int32, (n_sel, TN), 1)
    forced = (nidx == tq // SEL_LEN) | (nidx == 0)
    sel_ok = nidx * SEL_LEN <= tq
    nidx_f = nidx.astype(F32)
    scores, selms = [], []
    for g in range(ng):
        pg = ps[g * hg:(g + 1) * hg]
        psum = (pg[0] + pg[1]) + (pg[2] + pg[3])
        p_hi = psum.astype(BF16)
        p_lo = (psum - p_hi.astype(F32)).astype(BF16)
        imp = (_dot_nt(ovt_ref[...], p_hi) + _dot_nt(ovt_ref[...], p_lo))[:n_sel]
        scores.append(jnp.where(forced, -jnp.inf, jnp.where(sel_ok, imp, -FORCE_BONUS)))
        selms.append(jnp.where(forced, 1.0, 0.0))

    def process(jobs, first):
        ss = [post(_dot_nt(lhs[br][h], kt)) for (br, h, kt, _, post) in jobs]
        ps, alphas = [], []
        for (br, h, _, _, _), sh in zip(jobs, ss):
            mx = jnp.max(sh, axis=1, keepdims=True)
            if first:
                m_new = jnp.broadcast_to(mx, (TN, LANES))
                alphas.append(None)
            else:
                m_old = m_ref[br, h]
                m_new = jnp.maximum(m_old, mx)
                alphas.append(jnp.exp2(m_old - m_new))
            m_ref[br, h] = m_new
            ps.append(jnp.exp2(sh - jnp.concatenate([m_new] * (TN // LANES), axis=1)).astype(BF16))
        for (br, h, _, vt, _), p, alpha in zip(jobs, ps, alphas):
            pv = _dot(p, vt)
            acc_ref[br, h] = pv if first else alpha * acc_ref[br, h] + pv

    def pair_out(br, he, ho):
        acc_e, acc_o = acc_ref[br, he], acc_ref[br, ho]
        num = jnp.where(lo, acc_e, pltpu.roll(acc_o, HEAD_DIM, 1))
        den = jnp.where(lo, pltpu.roll(acc_e, HEAD_DIM, 1), acc_o)
        return num / den

    kxs, vxs = (ksx_ref, kwx_ref), (vsx_ref, vwx_ref)
    causal = kcol <= krow
    inside = kcol > krow

    def tile_jobs(branches, off, post):
        return [(br, h, kxs[br][h // hg, pl.ds(off, TN), :], vxs[br][h // hg, pl.ds(off, TN), :],
                 functools.partial(post, h)) for br in branches for h in range(nh)]

    off_i = pl.multiple_of(i * TN, TN)
    diag_post = lambda h, s: jnp.where(causal, s + tzd_ref[h, 0], NEG)
    lhs = [None, [bh.astype(BF16) for bh in base]]
    win_jobs = tile_jobs((1,), off_i, diag_post)
    per_round = -(-nh // (SEL_TOPK - 1))
    for k in range(SEL_TOPK - 1):
        for g in range(ng):
            mx = jnp.max(scores[g], axis=0, keepdims=True)
            idx = jnp.min(jnp.where(scores[g] == mx, nidx_f, float(LANES)), axis=0, keepdims=True)
            hit = nidx_f == idx
            if k == SEL_TOPK - 2:
                hit = hit & (tq < SEL_LEN)
            selms[g] = jnp.where(hit, 1.0, selms[g])
            scores[g] = jnp.where(hit, -jnp.inf, scores[g])
        process(win_jobs[k * per_round:(k + 1) * per_round], True)
    pad_lo = jnp.zeros((SEL_LANE0, TN), F32)
    pad_hi = jnp.zeros((LANES - SEL_LANE0 - n_sel, TN), F32)
    negsel = [jnp.transpose(jnp.concatenate([pad_lo, (sm - 1.0) * MASK_BIG, pad_hi], axis=0)) for sm in selms]
    lhs[0] = [(base[h] + negsel[h // hg]).astype(BF16) for h in range(nh)]
    process(tile_jobs((0,), off_i, diag_post), True)

    @pl.when(i >= 1)
    def _():
        off = pl.multiple_of(jnp.maximum(i - 1, 0) * TN, TN)
        process(tile_jobs((0, 1), off, lambda h, s: s + tzd_ref[h, 1]), False)

    @pl.when(i >= WINDOW // TN)
    def _():
        off = pl.multiple_of(jnp.maximum(i - WINDOW // TN, 0) * TN, TN)
        jobs = (tile_jobs((1,), off, lambda h, s: jnp.where(inside, s, NEG))
                + tile_jobs((0,), 0, lambda h, s: s))
        process(jobs, False)

    def sel_body(j, carry):
        process(tile_jobs((0,), pl.multiple_of(j * TN, TN), lambda h, s: s), False)
        return carry

    lax.fori_loop(1, jnp.maximum(i - 1, 1), sel_body, 0)

    width = hg * HEAD_DIM
    for g in range(ng):
        gates = 0.5 * jnp.tanh(0.5 * _dot(g_ref[0], gexp_ref[g])) + 0.5
        for c in range(hg // 2):
            g_cmp, g_sel, g_win = (gates[:, br * width + c * LANES:br * width + (c + 1) * LANES] for br in range(3))
            he, ho = g * hg + 2 * c, g * hg + 2 * c + 1
            out = (g_cmp * jnp.where(lo, o_cmp[g][hrows(2 * c)], o_cmp[g][hrows(2 * c + 1)])
                   + g_sel * pair_out(0, he, ho) + g_win * pair_out(1, he, ho))
            o_ref[0, :, (he // 2) * LANES:(he // 2 + 1) * LANES] = out.astype(BF16)


def _nsa(p3, kvc, bias_c, tzd, ovt, qaux, kaux, gexp):
    b, t, _ = p3.shape
    nq = t // TN
    width = NSA_HEADS * HEAD_DIM
    seq = lambda col: pl.BlockSpec((1, t, LANES), lambda bi, i, col=col: (bi, 0, col))
    const = lambda a: pl.BlockSpec(a.shape, lambda bi, i: (0,) * a.ndim, pipeline_mode=pl.Buffered(1))
    return pl.pallas_call(
        _nsa_kernel,
        grid=(b, nq),
        in_specs=[pl.BlockSpec((1, TN, width), lambda bi, i: (bi, i, _C_QN * LANES // width)),
                  seq(_C_KS), seq(_C_VS), seq(_C_KW), seq(_C_VW),
                  pl.BlockSpec((1, TN, LANES), lambda bi, i: (bi, i, _C_GATE)),
                  const(gexp),
                  pl.BlockSpec((1,) + kvc.shape[1:], lambda bi, i: (bi, 0, 0, 0)),
                  pl.BlockSpec((NSA_HEADS, 1, TN, LANES), lambda bi, i: (0, i, 0, 0)),
                  const(tzd), const(ovt), const(qaux), const(kaux)],
        out_specs=pl.BlockSpec((1, TN, width), lambda bi, i: (bi, i, 0)),
        out_shape=jax.ShapeDtypeStruct((b, t, width), BF16),
        scratch_shapes=[pltpu.VMEM((NSA_KV_HEADS, t, LANES), BF16)] * 4
                       + [pltpu.VMEM((2, NSA_HEADS, TN, LANES), F32)] * 2,
        compiler_params=pltpu.CompilerParams(dimension_semantics=("parallel", "arbitrary"),
                                             vmem_limit_bytes=VMEM_LIMIT),
        name="nsa",
    )(p3, p3, p3, p3, p3, p3, gexp, kvc, bias_c, tzd, ovt, qaux, kaux)


def _sb_kernel(q_ref, k_ref, v_ref, u_ref, o_ref, acc_ref, carry_ref):
    i = pl.program_id(1)
    nb = q_ref.shape[0]
    chains = [(bb, c) for bb in range(nb) for c in range(SB_HEADS // 2)]
    cols = [slice(c * LANES, (c + 1) * LANES) for c in range(SB_HEADS // 2)]
    lane = lax.broadcasted_iota(jnp.int32, (TS, LANES), 1)
    lo = lane < HEAD_DIM
    lane2 = lax.broadcasted_iota(jnp.int32, (2 * TS, TS), 1)
    row2 = lax.broadcasted_iota(jnp.int32, (2 * TS, TS), 0) & (TS - 1)
    causal = lane2 < row2

    qs = []
    for bb, c in chains:
        q = q_ref[bb, :, cols[c]].astype(F32)
        qs.append(jnp.concatenate([jnp.where(lo, q, 0.0), jnp.where(lo, 0.0, q)], axis=0).astype(BF16))

    acc_ref[...] = jnp.zeros(acc_ref.shape, F32)
    carry_ref[...] = jnp.zeros(carry_ref.shape, F32)

    def tile(j, diag):
        off = pl.multiple_of(j * TS, TS)
        zs = [_dot_nt(qs[n], k_ref[bb, pl.ds(off, TS), cols[c]]) for n, (bb, c) in enumerate(chains)]
        log_betas, log_keeps, k_bfs = [], [], []
        for z in zs:
            neg_abs = lax.bitcast_convert_type(lax.bitcast_convert_type(z, jnp.uint32) | SIGN_BIT, F32)
            log_beta = jnp.minimum(z, 0.0) - jnp.log(1.0 + jnp.exp2(neg_abs)) * LOG2E
            log_keep = log_beta - z
            if diag:
                log_keep = jnp.where(causal, log_keep, 0.0)
            log_betas.append(log_beta)
            log_keeps.append(log_keep)
            k_bfs.append(log_keep.astype(BF16))
        laters = [_dot(k_bf, u_ref[...]) for k_bf in k_bfs]
        a_s = []
        for n, (bb, c) in enumerate(chains):
            carry = carry_ref[bb, c]
            a = jnp.exp2(log_betas[n] + laters[n] + jnp.concatenate([carry] * (TS // LANES), axis=1))
            if diag:
                a = jnp.where(causal, a, 0.0)
            a_s.append(a.astype(BF16))
            carry_ref[bb, c] = carry + jnp.sum(log_keeps[n], axis=1, keepdims=True)
        for n, (bb, c) in enumerate(chains):
            acc_ref[bb, c] += _dot(a_s[n], v_ref[bb, pl.ds(off, TS), cols[c]])

    tile(i, True)

    def body(n, carry):
        tile(i - 1 - n, False)
        return carry

    lax.fori_loop(0, i, body, 0)
    for bb, c in chains:
        o = acc_ref[bb, c]
        o_ref[bb, :, cols[c]] = jnp.where(lo, o[:TS], o[TS:]).astype(BF16)


def _sb(p3, u):
    b, t, _ = p3.shape
    width = SB_HEADS * HEAD_DIM
    nb = 2 if b % 2 == 0 else 1
    return pl.pallas_call(
        _sb_kernel,
        grid=(b // nb, t // TS),
        in_specs=[pl.BlockSpec((nb, TS, width), lambda bi, i: (bi, i, _C_SBQ * LANES // width)),
                  pl.BlockSpec((nb, t, width), lambda bi, i: (bi, 0, _C_SBK * LANES // width)),
                  pl.BlockSpec((nb, t, width), lambda bi, i: (bi, 0, _C_SBV * LANES // width)),
                  pl.BlockSpec((TS, TS), lambda bi, i: (0, 0))],
        out_specs=pl.BlockSpec((nb, TS, width), lambda bi, i: (bi, i, 0)),
        out_shape=jax.ShapeDtypeStruct((b, t, width), BF16),
        scratch_shapes=[pltpu.VMEM((nb, SB_HEADS // 2, 2 * TS, LANES), F32),
                        pltpu.VMEM((nb, SB_HEADS // 2, 2 * TS, LANES), F32)],
        compiler_params=pltpu.CompilerParams(dimension_semantics=("parallel", "arbitrary"),
                                             vmem_limit_bytes=VMEM_LIMIT),
        name="sb",
    )(p3, p3, p3, u)


def _tail_kernel(x_ref, on_ref, os_ref, ga_ref, wgm_ref, wun_ref, wus_ref, wo_ref, gm_ref,
                 w1_ref, w2_ref, gf_ref, o_ref, *, last_layer):
    x = x_ref[...]
    d = x.shape[1]
    h = _rms(x, ga_ref[...]).astype(BF16)
    gm = 1.0 / (1.0 + jnp.exp(-_dot(h, wgm_ref[...])))
    mixed = gm[:, :d] * _dot(on_ref[...], wun_ref[...]) + gm[:, d:] * _dot(os_ref[...], wus_ref[...])
    x1 = x + _dot(mixed.astype(BF16), wo_ref[...])
    h2 = _rms(x1, gm_ref[...]).astype(BF16)
    dff = w1_ref.shape[1]
    acc = jnp.zeros_like(x1)
    for f in range(0, dff, FF_CHUNK):
        u = jnp.maximum(_dot(h2, w1_ref[:, f:f + FF_CHUNK]), 0.0)
        acc = acc + _dot((u * u).astype(BF16), w2_ref[f:f + FF_CHUNK, :])
    x2 = x1 + acc
    o_ref[...] = _rms(x2, gf_ref[...]) if last_layer else x2


def _tail(x2, o_nsa, o_sb, g_attn, wgm, wun, wus, wo, g_mlp, w1, w2, g_final, last_layer):
    m, d = x2.shape
    tm = ROWS_TAIL
    tok = lambda w: pl.BlockSpec((tm, w), lambda i: (i, 0))
    const = lambda a: pl.BlockSpec(a.shape, lambda i: (0, 0), pipeline_mode=pl.Buffered(1))
    return pl.pallas_call(
        functools.partial(_tail_kernel, last_layer=last_layer),
        grid=(m // tm,),
        in_specs=[tok(d), tok(o_nsa.shape[1]), tok(o_sb.shape[1]), const(g_attn), const(wgm),
                  const(wun), const(wus), const(wo), const(g_mlp), const(w1), const(w2), const(g_final)],
        out_specs=tok(d),
        out_shape=jax.ShapeDtypeStruct((m, d), F32),
        compiler_params=pltpu.CompilerParams(dimension_semantics=("parallel",),
                                             vmem_limit_bytes=VMEM_LIMIT),
        name="tail",
    )(x2, o_nsa, o_sb, g_attn, wgm, wun, wus, wo, g_mlp, w1, w2, g_final)


def _t5_bucket_np(dist):
    n = np.maximum(dist, 0)
    max_exact = N_BUCKETS // 2
    nf = np.maximum(n, 1).astype(np.float32)
    large = max_exact + (np.log(nf / np.float32(max_exact)) / np.float32(math.log(MAX_DISTANCE / max_exact))
                         * np.float32(N_BUCKETS - max_exact)).astype(np.int32)
    return np.where(n < max_exact, n, np.minimum(large, N_BUCKETS - 1)).astype(np.int32)


def _bias_tables(rel_bias, t):
    tbl = rel_bias * LOG2E
    far_idx = int(_t5_bucket_np(np.asarray(MAX_DISTANCE)))
    far = tbl[far_idx]

    def lookup(idx, table):
        onehot = jnp.asarray(idx.reshape(-1)[:, None] == np.arange(N_BUCKETS)[None, :], F32)
        out = jnp.einsum("bh,nb->hn", table, onehot, precision=lax.Precision.HIGHEST)
        return out.reshape((table.shape[1],) + idx.shape)

    qi, kj = np.arange(TN)[:, None], np.arange(TN)[None, :]
    tzd = lookup(np.stack([_t5_bucket_np(r * TN + qi - kj) for r in range(2)]), tbl - far[None, :])
    dist_c = (np.arange(t).reshape(t // TN, TN, 1) - CMP_STRIDE * np.arange(LANES)[None, None, :]
              - (CMP_LEN - 1))
    bias_c = lookup(np.where(dist_c >= 0, _t5_bucket_np(dist_c), far_idx), tbl)
    far_hi = far.astype(BF16).astype(F32)
    qaux = jnp.zeros((rel_bias.shape[1], 1, LANES), F32)
    qaux = qaux.at[:, 0, BIAS_LANE].set(far_hi).at[:, 0, BIAS_LANE + 1].set(far - far_hi)
    return tzd, bias_c, qaux


def _const_tables(t):
    c_start = np.arange(LANES) * CMP_STRIDE
    s_start = np.arange(LANES) * SEL_LEN
    n_cmp = (t - CMP_LEN) // CMP_STRIDE + 1
    n_sel = t // SEL_LEN
    ov = ((c_start[:, None] < s_start[None, :] + SEL_LEN) & (c_start[:, None] + CMP_LEN > s_start[None, :])
          & (np.arange(LANES)[:, None] < n_cmp) & (np.arange(LANES)[None, :] < n_sel))
    kaux = np.zeros((2, t, LANES), np.float32)
    kaux[0, np.arange(t), SEL_LANE0 + np.arange(t) // SEL_LEN] = 1.0
    kaux[:, :, BIAS_LANE:BIAS_LANE + 2] = 1.0
    u = np.arange(TS)[:, None] > np.arange(TS)[None, :]
    gexp = np.zeros((NSA_KV_HEADS, LANES, 3 * NSA_GROUP * HEAD_DIM), np.float32)
    for h in range(NSA_HEADS):
        for br in range(3):
            c0 = (br * NSA_GROUP + h % NSA_GROUP) * HEAD_DIM
            gexp[h // NSA_GROUP, h * 3 + br, c0:c0 + HEAD_DIM] = 1.0
    return jnp.asarray(ov.T, BF16), jnp.asarray(kaux, BF16), jnp.asarray(u, BF16), jnp.asarray(gexp, BF16)


def _arrange_w_in(w):
    d = w.shape[0]
    nq, nkv, ng = NSA_HEADS * HEAD_DIM, 6 * NSA_KV_HEADS * HEAD_DIM, 3 * NSA_HEADS
    nsb = SB_HEADS * HEAD_DIM
    scale = HEAD_DIM ** -0.5 * LOG2E
    q = w[:, :nq] * scale
    kv = w[:, nq:nq + nkv]
    gl = w[:, nq + nkv:nq + nkv + ng]
    sb = w[:, nq + nkv + ng:nq + nkv + ng + 3 * nsb]
    gm = w[:, nq + nkv + ng + 3 * nsb:]
    kvw = NSA_KV_HEADS * HEAD_DIM
    assert kvw == LANES and ng <= LANES
    gates = jnp.pad(gl, ((0, 0), (0, LANES - ng)))
    cols = [sb[:, :nsb] * scale, sb[:, nsb:], q, kv[:, 2 * kvw:], gates, kv[:, :2 * kvw]]
    return jnp.concatenate(cols, axis=1).astype(BF16), gm.astype(BF16)


def kernel(x, norm_attn, w_in, rel_bias, cmp_k_pe, cmp_k_w1, cmp_k_b1, cmp_k_w2, cmp_v_pe, cmp_v_w1, cmp_v_b1, cmp_v_w2, w_up_nsa, w_up_sb, w_out, norm_mlp, w_ff1, w_ff2, norm_final):
    b, t, d = x.shape
    depth = w_in.shape[0]
    assert t % TN == 0 and t % TS == 0 and WINDOW == 2 * TN and MAX_DISTANCE <= TN
    assert t // SEL_LEN <= N_BUCKETS and (t - CMP_LEN) // CMP_STRIDE + 1 < LANES
    assert (b * t) % ROWS_INPROJ == 0 and (b * t) % ROWS_TAIL == 0
    tzd, bias_c, qaux = _bias_tables(rel_bias, t)
    ovt, kaux, u, gexp = _const_tables(t)
    half = CMP_LEN // 2 * HEAD_DIM
    xf = x.reshape(b * t, d)
    for layer in range(depth):
        w_arr, w_gm = _arrange_w_in(w_in[layer])
        assert w_arr.shape[1] == _N_CHUNKS * LANES
        proj = _inproj(xf, norm_attn[layer][None, :], w_arr).reshape(b, t, _N_CHUNKS * LANES)

        kvc = proj[:, :, _C_KVC * LANES:].reshape(b, t, 4, HEAD_DIM)
        kvc = jnp.transpose(kvc, (0, 2, 1, 3)).reshape(b, 4, t // CMP_STRIDE, CMP_STRIDE * HEAD_DIM)
        pe = jnp.stack([cmp_k_pe[layer], cmp_v_pe[layer]]).reshape(2, 1, CMP_LEN * HEAD_DIM)
        w1 = jnp.stack([cmp_k_w1[layer], cmp_v_w1[layer]]).astype(BF16)
        b1 = jnp.stack([cmp_k_b1[layer], cmp_v_b1[layer]])[:, None, :]
        w2 = jnp.stack([cmp_k_w2[layer], cmp_v_w2[layer]])
        w2d = jnp.concatenate([w2, w2], axis=2).astype(BF16)
        kvc = _compress(kvc, pe[:, :, :half], pe[:, :, half:], w1[:, :half], w1[:, half:], b1, w2d)

        o_nsa = _nsa(proj, kvc, bias_c, tzd, ovt, qaux, kaux, gexp)
        o_sb = _sb(proj, u)
        xf = _tail(xf, o_nsa.reshape(b * t, -1), o_sb.reshape(b * t, -1), norm_attn[layer][None, :], w_gm,
                   w_up_nsa[layer].astype(BF16), w_up_sb[layer].astype(BF16), w_out[layer].astype(BF16),
                   norm_mlp[layer][None, :], w_ff1[layer].astype(BF16), w_ff2[layer].astype(BF16),
                   norm_final[None, :], layer == depth - 1)
    return xf.reshape(b, t, d)
```

```python
import functools
import math

import numpy as np
import jax
import jax.numpy as jnp
from jax import lax
from jax.experimental import pallas as pl
from jax.experimental.pallas import tpu as pltpu

HEAD_DIM = 64
NSA_HEADS = 8
NSA_KV_HEADS = 2
NSA_GROUP = NSA_HEADS // NSA_KV_HEADS
SB_HEADS = 8
CMP_LEN = 32
CMP_STRIDE = 16
CMP_HIDDEN = 256
SEL_LEN = 64
SEL_TOPK = 4
WINDOW = 512
N_BUCKETS = 32
MAX_DISTANCE = 128
EPS = 1e-6
NEG = -1e30
FORCE_BONUS = 1e4

LANES = 128
TN = 256
TS = 256
ROWS_INPROJ = 512
ROWS_TAIL = 512
FF_CHUNK = 512
MXU_COLS = 256
SETUP_ROWS = 512
VMEM_LIMIT = 56 * 1024 * 1024
LOG2E = math.log2(math.e)
MASK_BIG = 2.0 ** 60
SEL_LANE0 = HEAD_DIM
BIAS_LANE = HEAD_DIM + N_BUCKETS

F32 = jnp.float32
BF16 = jnp.bfloat16
SIGN_BIT = np.uint32(0x80000000)

_C_SBQ = 0
_C_SBK = 4
_C_SBV = 8
_C_QN = 12
_C_KS = 16
_C_VS = 17
_C_KW = 18
_C_VW = 19
_C_GATE = 20
_C_KVC = 21
_N_CHUNKS = 23


def _dot(a, b):
    return jnp.dot(a, b, preferred_element_type=F32)


def _dot_nt(a, b):
    return lax.dot_general(a, b, (((1,), (1,)), ((), ())), preferred_element_type=F32)


def _rms(x, g):
    return x * lax.rsqrt(jnp.mean(x * x, axis=-1, keepdims=True) + EPS) * g


def _inproj_kernel(x_ref, g_ref, w_ref, o_ref):
    h = _rms(x_ref[...], g_ref[...]).astype(BF16)
    n = o_ref.shape[1]
    for c in range(0, n, MXU_COLS):
        cs = slice(c, min(c + MXU_COLS, n))
        o_ref[:, cs] = _dot(h, w_ref[:, cs]).astype(BF16)


def _inproj(x2, g, w):
    m, d = x2.shape
    n = w.shape[1]
    tm = ROWS_INPROJ
    return pl.pallas_call(
        _inproj_kernel,
        grid=(m // tm,),
        in_specs=[pl.BlockSpec((tm, d), lambda i: (i, 0)),
                  pl.BlockSpec((1, d), lambda i: (0, 0)),
                  pl.BlockSpec((d, n), lambda i: (0, 0))],
        out_specs=pl.BlockSpec((tm, n), lambda i: (i, 0)),
        out_shape=jax.ShapeDtypeStruct((m, n), BF16),
        compiler_params=pltpu.CompilerParams(dimension_semantics=("parallel",),
                                             vmem_limit_bytes=VMEM_LIMIT),
        name="inproj",
    )(x2, g, w)


def _gelu_tanh(x):
    return 0.5 * x * (1.0 + jnp.tanh(math.sqrt(2.0 / math.pi) * (x + 0.044715 * (x * x * x))))


def _compress_kernel(x_ref, pea_ref, peb_ref, w1a_ref, w1b_ref, b1_ref, w2_ref, o_ref):
    for s in range(4):
        r = s // 2
        x = x_ref[0, s].astype(F32)
        a = _dot((x + pea_ref[r]).astype(BF16), w1a_ref[r])
        b = _dot((x + peb_ref[r]).astype(BF16), w1b_ref[r])
        hid = a + pltpu.roll(b, shift=b.shape[0] - 1, axis=0) + b1_ref[r]
        o_ref[0, s] = _dot(_gelu_tanh(hid).astype(BF16), w2_ref[r]).astype(BF16)


def _compress(xc, pea, peb, w1a, w1b, b1, w2d):
    b = xc.shape[0]
    nchunk, width = xc.shape[2], xc.shape[3]
    full = lambda shape: pl.BlockSpec(shape, lambda i: (0,) * len(shape))
    return pl.pallas_call(
        _compress_kernel,
        grid=(b,),
        in_specs=[pl.BlockSpec((1, 4, nchunk, width), lambda i: (i, 0, 0, 0)),
                  full(pea.shape), full(peb.shape), full(w1a.shape), full(w1b.shape),
                  full(b1.shape), full(w2d.shape)],
        out_specs=pl.BlockSpec((1, 4, nchunk, LANES), lambda i: (i, 0, 0, 0)),
        out_shape=jax.ShapeDtypeStruct((b, 4, nchunk, LANES), BF16),
        compiler_params=pltpu.CompilerParams(dimension_semantics=("parallel",),
                                             vmem_limit_bytes=VMEM_LIMIT),
        name="compress",
    )(xc, pea, peb, w1a, w1b, b1, w2d)


def _nsa_kernel(q_ref, ks_ref, vs_ref, kw_ref, vw_ref, g_ref, gexp_ref, kvc_ref,
                bc_ref, tzd_ref, ovt_ref, qaux_ref, kaux_ref, o_ref,
                ksx_ref, vsx_ref, kwx_ref, vwx_ref, m_ref, acc_ref):
    i = pl.program_id(1)
    ng, hg, nh = NSA_KV_HEADS, NSA_GROUP, NSA_HEADS
    t = ks_ref.shape[1]
    lane = lax.broadcasted_iota(jnp.int32, (TN, LANES), 1)
    row = lax.broadcasted_iota(jnp.int32, (TN, LANES), 0)
    lo = lane < HEAD_DIM
    t_abs = i * TN + row
    krow = lax.broadcasted_iota(jnp.int32, (TN, TN), 0)
    kcol = lax.broadcasted_iota(jnp.int32, (TN, TN), 1)

    def hrows(h):
        return slice(h * TN, (h + 1) * TN)

    @pl.when(i == 0)
    def _():
        ch = SETUP_ROWS
        lo_c = lax.broadcasted_iota(jnp.int32, (ch, LANES), 1) < HEAD_DIM
        for r0 in range(0, t, ch):
            rs = slice(r0, r0 + ch)
            for g in range(ng):
                def own(ref):
                    x = ref[0, rs].astype(F32)
                    return x if g == 0 else pltpu.roll(x, HEAD_DIM, 1)

                ksx_ref[g, rs] = jnp.where(lo_c, own(ks_ref), kaux_ref[0, rs].astype(F32)).astype(BF16)
                kwx_ref[g, rs] = jnp.where(lo_c, own(kw_ref), kaux_ref[1, rs].astype(F32)).astype(BF16)
                vsx_ref[g, rs] = jnp.where(lo_c, own(vs_ref), 1.0).astype(BF16)
                vwx_ref[g, rs] = jnp.where(lo_c, own(vw_ref), 1.0).astype(BF16)

    base = []
    for c in range(nh // 2):
        qc = q_ref[0, :, c * LANES:(c + 1) * LANES].astype(F32)
        for h, qh in ((2 * c, qc), (2 * c + 1, pltpu.roll(qc, HEAD_DIM, 1))):
            base.append(jnp.where(lo, qh, qaux_ref[h]))

    lo_c = lax.broadcasted_iota(jnp.int32, (LANES, LANES), 1) < HEAD_DIM
    valid_c = (t_abs >= CMP_STRIDE * lane + (CMP_LEN - 1)) & (lane < LANES - 1)
    s_cmp = [_dot_nt(jnp.concatenate(base[g * hg:(g + 1) * hg], axis=0).astype(BF16),
                     jnp.where(lo_c, kvc_ref[0, g].astype(F32), 0.0).astype(BF16)) for g in range(ng)]
    ps = []
    for h in range(nh):
        sh = jnp.where(valid_c, s_cmp[h // hg][hrows(h % hg)] + bc_ref[h, 0], NEG)
        e = jnp.exp2(sh - jnp.max(sh, axis=1, keepdims=True))
        ps.append(jnp.where(valid_c, e, 0.0) / jnp.sum(e, axis=1, keepdims=True))
    o_cmp = [_dot(jnp.concatenate(ps[g * hg:(g + 1) * hg], axis=0).astype(BF16), kvc_ref[0, ng + g])
             for g in range(ng)]

    n_sel = t // SEL_LEN
    nidx = lax.broadcasted_iota(jnp.int32, (n_sel, TN), 0)
    tq = i * TN + lax.broadcasted_iota(jnp.int32, (n_sel, TN), 1)
    forced = (nidx == tq // SEL_LEN) | (nidx == 0)
    sel_ok = nidx * SEL_LEN <= tq
    nidx_f = nidx.astype(F32)
    scores, selms = [], []
    for g in range(ng):
        pg = ps[g * hg:(g + 1) * hg]
        psum = (pg[0] + pg[1]) + (pg[2] + pg[3])
        p_hi = psum.astype(BF16)
        p_lo = (psum - p_hi.astype(F32)).astype(BF16)
        imp = (_dot_nt(ovt_ref[...], p_hi) + _dot_nt(ovt_ref[...], p_lo))[:n_sel]
        scores.append(jnp.where(forced, -jnp.inf, jnp.where(sel_ok, imp, -FORCE_BONUS)))
        selms.append(jnp.where(forced, 1.0, 0.0))

    def process(jobs, first):
        ss = [post(_dot_nt(lhs[br][h], kt)) for (br, h, kt, _, post) in jobs]
        ps, alphas = [], []
        for (br, h, _, _, _), sh in zip(jobs, ss):
            mx = jnp.max(sh, axis=1, keepdims=True)
            if first:
                m_new = jnp.broadcast_to(mx, (TN, LANES))
                alphas.append(None)
            else:
                m_old = m_ref[br, h]
                m_new = jnp.maximum(m_old, mx)
                alphas.append(jnp.exp2(m_old - m_new))
            m_ref[br, h] = m_new
            ps.append(jnp.exp2(sh - jnp.concatenate([m_new] * (sh.shape[1] // LANES), axis=1)).astype(BF16))
        for (br, h, _, vt, _), p, alpha in zip(jobs, ps, alphas):
            pv = _dot(p, vt)
            acc_ref[br, h] = pv if first else alpha * acc_ref[br, h] + pv

    def pair_out(br, he, ho):
        acc_e, acc_o = acc_ref[br, he], acc_ref[br, ho]
        num = jnp.where(lo, acc_e, pltpu.roll(acc_o, HEAD_DIM, 1))
        den = jnp.where(lo, pltpu.roll(acc_e, HEAD_DIM, 1), acc_o)
        return num / den

    kxs, vxs = (ksx_ref, kwx_ref), (vsx_ref, vwx_ref)
    causal = kcol <= krow
    inside = kcol > krow

    def tile_jobs(branches, off, post, width=TN):
        return [(br, h, kxs[br][h // hg, pl.ds(off, width), :], vxs[br][h // hg, pl.ds(off, width), :],
                 functools.partial(post, h)) for br in branches for h in range(nh)]

    off_i = pl.multiple_of(i * TN, TN)
    diag_post = lambda h, s: jnp.where(causal, s + tzd_ref[h, 0], NEG)
    lhs = [None, [bh.astype(BF16) for bh in base]]
    win_jobs = tile_jobs((1,), off_i, diag_post)
    per_round = -(-nh // (SEL_TOPK - 1))
    for k in range(SEL_TOPK - 1):
        for g in range(ng):
            mx = jnp.max(scores[g], axis=0, keepdims=True)
            idx = jnp.min(jnp.where(scores[g] == mx, nidx_f, float(LANES)), axis=0, keepdims=True)
            hit = nidx_f == idx
            if k == SEL_TOPK - 2:
                hit = hit & (tq < SEL_LEN)
            selms[g] = jnp.where(hit, 1.0, selms[g])
            scores[g] = jnp.where(hit, -jnp.inf, scores[g])
        process(win_jobs[k * per_round:(k + 1) * per_round], True)
    pad_lo = jnp.zeros((SEL_LANE0, TN), F32)
    pad_hi = jnp.zeros((LANES - SEL_LANE0 - n_sel, TN), F32)
    negsel = [jnp.transpose(jnp.concatenate([pad_lo, (sm - 1.0) * MASK_BIG, pad_hi], axis=0)) for sm in selms]
    lhs[0] = [(base[h] + negsel[h // hg]).astype(BF16) for h in range(nh)]
    process(tile_jobs((0,), off_i, diag_post), True)

    near_post = lambda h, s: s + tzd_ref[h, 1]
    back = WINDOW // TN

    @pl.when(i == back - 1)
    def _():
        process(tile_jobs((0, 1), 0, near_post), False)

    @pl.when(i >= back)
    def _():
        off = pl.multiple_of(jnp.maximum(i - back, 0) * TN, TN)
        win_post = lambda h, s: jnp.concatenate([jnp.where(inside, s[:, :TN], NEG), near_post(h, s[:, TN:])], axis=1)
        sel_post = lambda h, s: jnp.concatenate([s[:, :TN], near_post(h, s[:, TN:])], axis=1)
        process(tile_jobs((1,), off, win_post, 2 * TN) + tile_jobs((0,), off, sel_post, 2 * TN), False)

    n_far = jnp.maximum(i - back, 0)

    def sel_body(jj, carry):
        process(tile_jobs((0,), pl.multiple_of(2 * jj * TN, TN), lambda h, s: s, 2 * TN), False)
        return carry

    lax.fori_loop(0, n_far // 2, sel_body, 0)

    @pl.when(n_far % 2 == 1)
    def _():
        process(tile_jobs((0,), pl.multiple_of(jnp.maximum(n_far - 1, 0) * TN, TN), lambda h, s: s), False)

    width = hg * HEAD_DIM
    for g in range(ng):
        gates = 0.5 * jnp.tanh(0.5 * _dot(g_ref[0], gexp_ref[g])) + 0.5
        for c in range(hg // 2):
            g_cmp, g_sel, g_win = (gates[:, br * width + c * LANES:br * width + (c + 1) * LANES] for br in range(3))
            he, ho = g * hg + 2 * c, g * hg + 2 * c + 1
            out = (g_cmp * jnp.where(lo, o_cmp[g][hrows(2 * c)], o_cmp[g][hrows(2 * c + 1)])
                   + g_sel * pair_out(0, he, ho) + g_win * pair_out(1, he, ho))
            o_ref[0, :, (he // 2) * LANES:(he // 2 + 1) * LANES] = out.astype(BF16)


def _nsa(p3, kvc, bias_c, tzd, ovt, qaux, kaux, gexp):
    b, t, _ = p3.shape
    nq = t // TN
    width = NSA_HEADS * HEAD_DIM
    seq = lambda col: pl.BlockSpec((1, t, LANES), lambda bi, i, col=col: (bi, 0, col))
    const = lambda a: pl.BlockSpec(a.shape, lambda bi, i: (0,) * a.ndim, pipeline_mode=pl.Buffered(1))
    return pl.pallas_call(
        _nsa_kernel,
        grid=(b, nq),
        in_specs=[pl.BlockSpec((1, TN, width), lambda bi, i: (bi, i, _C_QN * LANES // width)),
                  seq(_C_KS), seq(_C_VS), seq(_C_KW), seq(_C_VW),
                  pl.BlockSpec((1, TN, LANES), lambda bi, i: (bi, i, _C_GATE)),
                  const(gexp),
                  pl.BlockSpec((1,) + kvc.shape[1:], lambda bi, i: (bi, 0, 0, 0)),
                  pl.BlockSpec((NSA_HEADS, 1, TN, LANES), lambda bi, i: (0, i, 0, 0)),
                  const(tzd), const(ovt), const(qaux), const(kaux)],
        out_specs=pl.BlockSpec((1, TN, width), lambda bi, i: (bi, i, 0)),
        out_shape=jax.ShapeDtypeStruct((b, t, width), BF16),
        scratch_shapes=[pltpu.VMEM((NSA_KV_HEADS, t, LANES), BF16)] * 4
                       + [pltpu.VMEM((2, NSA_HEADS, TN, LANES), F32)] * 2,
        compiler_params=pltpu.CompilerParams(dimension_semantics=("parallel", "arbitrary"),
                                             vmem_limit_bytes=VMEM_LIMIT),
        name="nsa",
    )(p3, p3, p3, p3, p3, p3, gexp, kvc, bias_c, tzd, ovt, qaux, kaux)


def _sb_kernel(q_ref, k_ref, v_ref, u_ref, o_ref, acc_ref, carry_ref):
    i = pl.program_id(1)
    nb = q_ref.shape[0]
    chains = [(bb, c) for bb in range(nb) for c in range(SB_HEADS // 2)]
    cols = [slice(c * LANES, (c + 1) * LANES) for c in range(SB_HEADS // 2)]
    lane = lax.broadcasted_iota(jnp.int32, (TS, LANES), 1)
    lo = lane < HEAD_DIM
    lane2 = lax.broadcasted_iota(jnp.int32, (2 * TS, TS), 1)
    row2 = lax.broadcasted_iota(jnp.int32, (2 * TS, TS), 0) & (TS - 1)
    causal = lane2 < row2

    qs = []
    for bb, c in chains:
        q = q_ref[bb, :, cols[c]].astype(F32)
        qs.append(jnp.concatenate([jnp.where(lo, q, 0.0), jnp.where(lo, 0.0, q)], axis=0).astype(BF16))

    acc_ref[...] = jnp.zeros(acc_ref.shape, F32)
    carry_ref[...] = jnp.zeros(carry_ref.shape, F32)

    def tile(j, diag):
        off = pl.multiple_of(j * TS, TS)
        zs = [_dot_nt(qs[n], k_ref[bb, pl.ds(off, TS), cols[c]]) for n, (bb, c) in enumerate(chains)]
        log_betas, log_keeps, k_bfs = [], [], []
        for z in zs:
            neg_abs = lax.bitcast_convert_type(lax.bitcast_convert_type(z, jnp.uint32) | SIGN_BIT, F32)
            log_beta = jnp.minimum(z, 0.0) - jnp.log(1.0 + jnp.exp2(neg_abs)) * LOG2E
            log_keep = log_beta - z
            if diag:
                log_keep = jnp.where(causal, log_keep, 0.0)
            log_betas.append(log_beta)
            log_keeps.append(log_keep)
            k_bfs.append(log_keep.astype(BF16))
        laters = [_dot(k_bf, u_ref[...]) for k_bf in k_bfs]
        a_s = []
        for n, (bb, c) in enumerate(chains):
            carry = carry_ref[bb, c]
            a = jnp.exp2(log_betas[n] + laters[n] + jnp.concatenate([carry] * (TS // LANES), axis=1))
            if diag:
                a = jnp.where(causal, a, 0.0)
            a_s.append(a.astype(BF16))
            carry_ref[bb, c] = carry + jnp.sum(log_keeps[n], axis=1, keepdims=True)
        for n, (bb, c) in enumerate(chains):
            acc_ref[bb, c] += _dot(a_s[n], v_ref[bb, pl.ds(off, TS), cols[c]])

    tile(i, True)

    def body(n, carry):
        tile(i - 1 - n, False)
        return carry

    lax.fori_loop(0, i, body, 0)
    for bb, c in chains:
        o = acc_ref[bb, c]
        o_ref[bb, :, cols[c]] = jnp.where(lo, o[:TS], o[TS:]).astype(BF16)


def _sb(p3, u):
    b, t, _ = p3.shape
    width = SB_HEADS * HEAD_DIM
    nb = 2 if b % 2 == 0 else 1
    return pl.pallas_call(
        _sb_kernel,
        grid=(b // nb, t // TS),
        in_specs=[pl.BlockSpec((nb, TS, width), lambda bi, i: (bi, i, _C_SBQ * LANES // width)),
                  pl.BlockSpec((nb, t, width), lambda bi, i: (bi, 0, _C_SBK * LANES // width)),
                  pl.BlockSpec((nb, t, width), lambda bi, i: (bi, 0, _C_SBV * LANES // width)),
                  pl.BlockSpec((TS, TS), lambda bi, i: (0, 0))],
        out_specs=pl.BlockSpec((nb, TS, width), lambda bi, i: (bi, i, 0)),
        out_shape=jax.ShapeDtypeStruct((b, t, width), BF16),
        scratch_shapes=[pltpu.VMEM((nb, SB_HEADS // 2, 2 * TS, LANES), F32),
                        pltpu.VMEM((nb, SB_HEADS // 2, 2 * TS, LANES), F32)],
        compiler_params=pltpu.CompilerParams(dimension_semantics=("parallel", "arbitrary"),
                                             vmem_limit_bytes=VMEM_LIMIT),
        name="sb",
    )(p3, p3, p3, u)


def _tail_kernel(x_ref, on_ref, os_ref, ga_ref, wgm_ref, wun_ref, wus_ref, wo_ref, gm_ref,
                 w1_ref, w2_ref, gf_ref, o_ref, *, last_layer):
    x = x_ref[...]
    d = x.shape[1]
    h = _rms(x, ga_ref[...]).astype(BF16)
    gm = 1.0 / (1.0 + jnp.exp(-_dot(h, wgm_ref[...])))
    mixed = gm[:, :d] * _dot(on_ref[...], wun_ref[...]) + gm[:, d:] * _dot(os_ref[...], wus_ref[...])
    x1 = x + _dot(mixed.astype(BF16), wo_ref[...])
    h2 = _rms(x1, gm_ref[...]).astype(BF16)
    dff = w1_ref.shape[1]
    acc = jnp.zeros_like(x1)
    for f in range(0, dff, FF_CHUNK):
        u = jnp.maximum(_dot(h2, w1_ref[:, f:f + FF_CHUNK]), 0.0)
        acc = acc + _dot((u * u).astype(BF16), w2_ref[f:f + FF_CHUNK, :])
    x2 = x1 + acc
    o_ref[...] = _rms(x2, gf_ref[...]) if last_layer else x2


def _tail(x2, o_nsa, o_sb, g_attn, wgm, wun, wus, wo, g_mlp, w1, w2, g_final, last_layer):
    m, d = x2.shape
    tm = ROWS_TAIL
    tok = lambda w: pl.BlockSpec((tm, w), lambda i: (i, 0))
    const = lambda a: pl.BlockSpec(a.shape, lambda i: (0, 0), pipeline_mode=pl.Buffered(1))
    return pl.pallas_call(
        functools.partial(_tail_kernel, last_layer=last_layer),
        grid=(m // tm,),
        in_specs=[tok(d), tok(o_nsa.shape[1]), tok(o_sb.shape[1]), const(g_attn), const(wgm),
                  const(wun), const(wus), const(wo), const(g_mlp), const(w1), const(w2), const(g_final)],
        out_specs=tok(d),
        out_shape=jax.ShapeDtypeStruct((m, d), F32),
        compiler_params=pltpu.CompilerParams(dimension_semantics=("parallel",),
                                             vmem_limit_bytes=VMEM_LIMIT),
        name="tail",
    )(x2, o_nsa, o_sb, g_attn, wgm, wun, wus, wo, g_mlp, w1, w2, g_final)


def _t5_bucket_np(dist):
    n = np.maximum(dist, 0)
    max_exact = N_BUCKETS // 2
    nf = np.maximum(n, 1).astype(np.float32)
    large = max_exact + (np.log(nf / np.float32(max_exact)) / np.float32(math.log(MAX_DISTANCE / max_exact))
                         * np.float32(N_BUCKETS - max_exact)).astype(np.int32)
    return np.where(n < max_exact, n, np.minimum(large, N_BUCKETS - 1)).astype(np.int32)


def _bias_tables(rel_bias, t):
    tbl = rel_bias * LOG2E
    far_idx = int(_t5_bucket_np(np.asarray(MAX_DISTANCE)))
    far = tbl[far_idx]

    def lookup(idx, table):
        onehot = jnp.asarray(idx.reshape(-1)[:, None] == np.arange(N_BUCKETS)[None, :], F32)
        out = jnp.einsum("bh,nb->hn", table, onehot, precision=lax.Precision.HIGHEST)
        return out.reshape((table.shape[1],) + idx.shape)

    qi, kj = np.arange(TN)[:, None], np.arange(TN)[None, :]
    tzd = lookup(np.stack([_t5_bucket_np(r * TN + qi - kj) for r in range(2)]), tbl - far[None, :])
    dist_c = (np.arange(t).reshape(t // TN, TN, 1) - CMP_STRIDE * np.arange(LANES)[None, None, :]
              - (CMP_LEN - 1))
    bias_c = lookup(np.where(dist_c >= 0, _t5_bucket_np(dist_c), far_idx), tbl)
    far_hi = far.astype(BF16).astype(F32)
    qaux = jnp.zeros((rel_bias.shape[1], 1, LANES), F32)
    qaux = qaux.at[:, 0, BIAS_LANE].set(far_hi).at[:, 0, BIAS_LANE + 1].set(far - far_hi)
    return tzd, bias_c, qaux


def _const_tables(t):
    c_start = np.arange(LANES) * CMP_STRIDE
    s_start = np.arange(LANES) * SEL_LEN
    n_cmp = (t - CMP_LEN) // CMP_STRIDE + 1
    n_sel = t // SEL_LEN
    ov = ((c_start[:, None] < s_start[None, :] + SEL_LEN) & (c_start[:, None] + CMP_LEN > s_start[None, :])
          & (np.arange(LANES)[:, None] < n_cmp) & (np.arange(LANES)[None, :] < n_sel))
    kaux = np.zeros((2, t, LANES), np.float32)
    kaux[0, np.arange(t), SEL_LANE0 + np.arange(t) // SEL_LEN] = 1.0
    kaux[:, :, BIAS_LANE:BIAS_LANE + 2] = 1.0
    u = np.arange(TS)[:, None] > np.arange(TS)[None, :]
    gexp = np.zeros((NSA_KV_HEADS, LANES, 3 * NSA_GROUP * HEAD_DIM), np.float32)
    for h in range(NSA_HEADS):
        for br in range(3):
            c0 = (br * NSA_GROUP + h % NSA_GROUP) * HEAD_DIM
            gexp[h // NSA_GROUP, h * 3 + br, c0:c0 + HEAD_DIM] = 1.0
    return jnp.asarray(ov.T, BF16), jnp.asarray(kaux, BF16), jnp.asarray(u, BF16), jnp.asarray(gexp, BF16)


def _arrange_w_in(w):
    d = w.shape[0]
    nq, nkv, ng = NSA_HEADS * HEAD_DIM, 6 * NSA_KV_HEADS * HEAD_DIM, 3 * NSA_HEADS
    nsb = SB_HEADS * HEAD_DIM
    scale = HEAD_DIM ** -0.5 * LOG2E
    q = w[:, :nq] * scale
    kv = w[:, nq:nq + nkv]
    gl = w[:, nq + nkv:nq + nkv + ng]
    sb = w[:, nq + nkv + ng:nq + nkv + ng + 3 * nsb]
    gm = w[:, nq + nkv + ng + 3 * nsb:]
    kvw = NSA_KV_HEADS * HEAD_DIM
    assert kvw == LANES and ng <= LANES
    gates = jnp.pad(gl, ((0, 0), (0, LANES - ng)))
    cols = [sb[:, :nsb] * scale, sb[:, nsb:], q, kv[:, 2 * kvw:], gates, kv[:, :2 * kvw]]
    return jnp.concatenate(cols, axis=1).astype(BF16), gm.astype(BF16)


def kernel(x, norm_attn, w_in, rel_bias, cmp_k_pe, cmp_k_w1, cmp_k_b1, cmp_k_w2, cmp_v_pe, cmp_v_w1, cmp_v_b1, cmp_v_w2, w_up_nsa, w_up_sb, w_out, norm_mlp, w_ff1, w_ff2, norm_final):
    b, t, d = x.shape
    depth = w_in.shape[0]
    assert t % TN == 0 and t % TS == 0 and WINDOW == 2 * TN and MAX_DISTANCE <= TN
    assert t // SEL_LEN <= N_BUCKETS and (t - CMP_LEN) // CMP_STRIDE + 1 < LANES
    assert (b * t) % ROWS_INPROJ == 0 and (b * t) % ROWS_TAIL == 0
    tzd, bias_c, qaux = _bias_tables(rel_bias, t)
    ovt, kaux, u, gexp = _const_tables(t)
    half = CMP_LEN // 2 * HEAD_DIM
    xf = x.reshape(b * t, d)
    for layer in range(depth):
        w_arr, w_gm = _arrange_w_in(w_in[layer])
        assert w_arr.shape[1] == _N_CHUNKS * LANES
        proj = _inproj(xf, norm_attn[layer][None, :], w_arr).reshape(b, t, _N_CHUNKS * LANES)

        kvc = proj[:, :, _C_KVC * LANES:].reshape(b, t, 4, HEAD_DIM)
        kvc = jnp.transpose(kvc, (0, 2, 1, 3)).reshape(b, 4, t // CMP_STRIDE, CMP_STRIDE * HEAD_DIM)
        pe = jnp.stack([cmp_k_pe[layer], cmp_v_pe[layer]]).reshape(2, 1, CMP_LEN * HEAD_DIM)
        w1 = jnp.stack([cmp_k_w1[layer], cmp_v_w1[layer]]).astype(BF16)
        b1 = jnp.stack([cmp_k_b1[layer], cmp_v_b1[layer]])[:, None, :]
        w2 = jnp.stack([cmp_k_w2[layer], cmp_v_w2[layer]])
        w2d = jnp.concatenate([w2, w2], axis=2).astype(BF16)
        kvc = _compress(kvc, pe[:, :, :half], pe[:, :, half:], w1[:, :half], w1[:, half:], b1, w2d)

        o_nsa = _nsa(proj, kvc, bias_c, tzd, ovt, qaux, kaux, gexp)
        o_sb = _sb(proj, u)
        xf = _tail(xf, o_nsa.reshape(b * t, -1), o_sb.reshape(b * t, -1), norm_attn[layer][None, :], w_gm,
                   w_up_nsa[layer].astype(BF16), w_up_sb[layer].astype(BF16), w_out[layer].astype(BF16),
                   norm_mlp[layer][None, :], w_ff1[layer].astype(BF16), w_ff2[layer].astype(BF16),
                   norm_final[None, :], layer == depth - 1)
    return xf.reshape(b, t, d)
```

```python
import functools
import math

import numpy as np
import jax
import jax.numpy as jnp
from jax import lax
from jax.experimental import pallas as pl
from jax.experimental.pallas import tpu as pltpu

HEAD_DIM = 64
NSA_HEADS = 8
NSA_KV_HEADS = 2
NSA_GROUP = NSA_HEADS // NSA_KV_HEADS
SB_HEADS = 8
CMP_LEN = 32
CMP_STRIDE = 16
CMP_HIDDEN = 256
SEL_LEN = 64
SEL_TOPK = 4
WINDOW = 512
N_BUCKETS = 32
MAX_DISTANCE = 128
EPS = 1e-6
NEG = -1e30
FORCE_BONUS = 1e4

LANES = 128
TN = 256
TS = 256
ROWS_INPROJ = 512
ROWS_TAIL = 512
FF_CHUNK = 512
MXU_COLS = 256
SETUP_ROWS = 512
VMEM_LIMIT = 56 * 1024 * 1024
LOG2E = math.log2(math.e)
MASK_BIG = 2.0 ** 60
SEL_LANE0 = HEAD_DIM
BIAS_LANE = HEAD_DIM + N_BUCKETS

F32 = jnp.float32
BF16 = jnp.bfloat16

_C_SBQ = 0
_C_SBK = 4
_C_SBV = 8
_C_QN = 12
_C_KS = 16
_C_VS = 17
_C_KW = 18
_C_VW = 19
_C_GATE = 20
_C_KVC = 21
_N_CHUNKS = 23


def _dot(a, b):
    return jnp.dot(a, b, preferred_element_type=F32)


def _dot_nt(a, b):
    return lax.dot_general(a, b, (((1,), (1,)), ((), ())), preferred_element_type=F32)


def _rms(x, g):
    return x * lax.rsqrt(jnp.mean(x * x, axis=-1, keepdims=True) + EPS) * g


def _inproj_kernel(x_ref, g_ref, w_ref, o_ref):
    h = _rms(x_ref[...], g_ref[...]).astype(BF16)
    n = o_ref.shape[1]
    for c in range(0, n, MXU_COLS):
        cs = slice(c, min(c + MXU_COLS, n))
        o_ref[:, cs] = _dot(h, w_ref[:, cs]).astype(BF16)


def _inproj(x2, g, w):
    m, d = x2.shape
    n = w.shape[1]
    tm = ROWS_INPROJ
    return pl.pallas_call(
        _inproj_kernel,
        grid=(m // tm,),
        in_specs=[pl.BlockSpec((tm, d), lambda i: (i, 0)),
                  pl.BlockSpec((1, d), lambda i: (0, 0)),
                  pl.BlockSpec((d, n), lambda i: (0, 0))],
        out_specs=pl.BlockSpec((tm, n), lambda i: (i, 0)),
        out_shape=jax.ShapeDtypeStruct((m, n), BF16),
        compiler_params=pltpu.CompilerParams(dimension_semantics=("parallel",),
                                             vmem_limit_bytes=VMEM_LIMIT),
        name="inproj",
    )(x2, g, w)


def _gelu_tanh(x):
    return 0.5 * x * (1.0 + jnp.tanh(math.sqrt(2.0 / math.pi) * (x + 0.044715 * (x * x * x))))


def _compress_kernel(x_ref, pea_ref, peb_ref, w1a_ref, w1b_ref, b1_ref, w2_ref, o_ref):
    for s in range(4):
        r = s // 2
        x = x_ref[0, s].astype(F32)
        a = _dot((x + pea_ref[r]).astype(BF16), w1a_ref[r])
        b = _dot((x + peb_ref[r]).astype(BF16), w1b_ref[r])
        hid = a + pltpu.roll(b, shift=b.shape[0] - 1, axis=0) + b1_ref[r]
        o_ref[0, s] = _dot(_gelu_tanh(hid).astype(BF16), w2_ref[r]).astype(BF16)


def _compress(xc, pea, peb, w1a, w1b, b1, w2d):
    b = xc.shape[0]
    nchunk, width = xc.shape[2], xc.shape[3]
    full = lambda shape: pl.BlockSpec(shape, lambda i: (0,) * len(shape))
    return pl.pallas_call(
        _compress_kernel,
        grid=(b,),
        in_specs=[pl.BlockSpec((1, 4, nchunk, width), lambda i: (i, 0, 0, 0)),
                  full(pea.shape), full(peb.shape), full(w1a.shape), full(w1b.shape),
                  full(b1.shape), full(w2d.shape)],
        out_specs=pl.BlockSpec((1, 4, nchunk, LANES), lambda i: (i, 0, 0, 0)),
        out_shape=jax.ShapeDtypeStruct((b, 4, nchunk, LANES), BF16),
        compiler_params=pltpu.CompilerParams(dimension_semantics=("parallel",),
                                             vmem_limit_bytes=VMEM_LIMIT),
        name="compress",
    )(xc, pea, peb, w1a, w1b, b1, w2d)


def _nsa_kernel(q_ref, ks_ref, vs_ref, kw_ref, vw_ref, g_ref, gexp_ref, kvc_ref,
                bc_ref, tzd_ref, ovt_ref, qaux_ref, kaux_ref, o_ref,
                ksx_ref, vsx_ref, kwx_ref, vwx_ref, m_ref, acc_ref):
    i = pl.program_id(1)
    ng, hg, nh = NSA_KV_HEADS, NSA_GROUP, NSA_HEADS
    t = ks_ref.shape[1]
    lane = lax.broadcasted_iota(jnp.int32, (TN, LANES), 1)
    row = lax.broadcasted_iota(jnp.int32, (TN, LANES), 0)
    lo = lane < HEAD_DIM
    t_abs = i * TN + row
    krow = lax.broadcasted_iota(jnp.int32, (TN, TN), 0)
    kcol = lax.broadcasted_iota(jnp.int32, (TN, TN), 1)

    def hrows(h):
        return slice(h * TN, (h + 1) * TN)

    @pl.when(i == 0)
    def _():
        ch = SETUP_ROWS
        lo_c = lax.broadcasted_iota(jnp.int32, (ch, LANES), 1) < HEAD_DIM
        for r0 in range(0, t, ch):
            rs = slice(r0, r0 + ch)
            for g in range(ng):
                def own(ref):
                    x = ref[0, rs].astype(F32)
                    return x if g == 0 else pltpu.roll(x, HEAD_DIM, 1)

                ksx_ref[g, rs] = jnp.where(lo_c, own(ks_ref), kaux_ref[0, rs].astype(F32)).astype(BF16)
                kwx_ref[g, rs] = jnp.where(lo_c, own(kw_ref), kaux_ref[1, rs].astype(F32)).astype(BF16)
                vsx_ref[g, rs] = jnp.where(lo_c, own(vs_ref), 1.0).astype(BF16)
                vwx_ref[g, rs] = jnp.where(lo_c, own(vw_ref), 1.0).astype(BF16)

    base = []
    for c in range(nh // 2):
        qc = q_ref[0, :, c * LANES:(c + 1) * LANES].astype(F32)
        for h, qh in ((2 * c, qc), (2 * c + 1, pltpu.roll(qc, HEAD_DIM, 1))):
            base.append(jnp.where(lo, qh, qaux_ref[h]))

    lo_c = lax.broadcasted_iota(jnp.int32, (LANES, LANES), 1) < HEAD_DIM
    valid_c = (t_abs >= CMP_STRIDE * lane + (CMP_LEN - 1)) & (lane < LANES - 1)
    s_cmp = [_dot_nt(jnp.concatenate(base[g * hg:(g + 1) * hg], axis=0).astype(BF16),
                     jnp.where(lo_c, kvc_ref[0, g].astype(F32), 0.0).astype(BF16)) for g in range(ng)]
    ps = []
    for h in range(nh):
        sh = jnp.where(valid_c, s_cmp[h // hg][hrows(h % hg)] + bc_ref[h, 0], NEG)
        e = jnp.exp2(sh - jnp.max(sh, axis=1, keepdims=True))
        ps.append(jnp.where(valid_c, e, 0.0) / jnp.sum(e, axis=1, keepdims=True))
    o_cmp = [_dot(jnp.concatenate(ps[g * hg:(g + 1) * hg], axis=0).astype(BF16), kvc_ref[0, ng + g])
             for g in range(ng)]

    n_sel = t // SEL_LEN
    nidx = lax.broadcasted_iota(jnp.int32, (n_sel, TN), 0)
    tq = i * TN + lax.broadcasted_iota(jnp.int32, (n_sel, TN), 1)
    forced = (nidx == tq // SEL_LEN) | (nidx == 0)
    sel_ok = nidx * SEL_LEN <= tq
    nidx_f = nidx.astype(F32)
    scores, selms = [], []
    for g in range(ng):
        pg = ps[g * hg:(g + 1) * hg]
        psum = (pg[0] + pg[1]) + (pg[2] + pg[3])
        p_hi = psum.astype(BF16)
        p_lo = (psum - p_hi.astype(F32)).astype(BF16)
        imp = (_dot_nt(ovt_ref[...], p_hi) + _dot_nt(ovt_ref[...], p_lo))[:n_sel]
        scores.append(jnp.where(forced, -jnp.inf, jnp.where(sel_ok, imp, -FORCE_BONUS)))
        selms.append(jnp.where(forced, 1.0, 0.0))

    def process(jobs, first):
        ss = [post(_dot_nt(lhs[br][h], kt)) for (br, h, kt, _, post) in jobs]
        ps, alphas = [], []
        for (br, h, _, _, _), sh in zip(jobs, ss):
            mx = jnp.max(sh, axis=1, keepdims=True)
            if first:
                m_new = jnp.broadcast_to(mx, (TN, LANES))
                alphas.append(None)
            else:
                m_old = m_ref[br, h]
                m_new = jnp.maximum(m_old, mx)
                alphas.append(jnp.exp2(m_old - m_new))
            m_ref[br, h] = m_new
            ps.append(jnp.exp2(sh - jnp.concatenate([m_new] * (sh.shape[1] // LANES), axis=1)).astype(BF16))
        for (br, h, _, vt, _), p, alpha in zip(jobs, ps, alphas):
            pv = _dot(p, vt)
            acc_ref[br, h] = pv if first else alpha * acc_ref[br, h] + pv

    def pair_out(br, he, ho):
        acc_e, acc_o = acc_ref[br, he], acc_ref[br, ho]
        num = jnp.where(lo, acc_e, pltpu.roll(acc_o, HEAD_DIM, 1))
        den = jnp.where(lo, pltpu.roll(acc_e, HEAD_DIM, 1), acc_o)
        return num / den

    kxs, vxs = (ksx_ref, kwx_ref), (vsx_ref, vwx_ref)
    causal = kcol <= krow
    inside = kcol > krow

    def tile_jobs(branches, off, post, width=TN):
        return [(br, h, kxs[br][h // hg, pl.ds(off, width), :], vxs[br][h // hg, pl.ds(off, width), :],
                 functools.partial(post, h)) for br in branches for h in range(nh)]

    for k in range(SEL_TOPK - 1):
        for g in range(ng):
            mx = jnp.max(scores[g], axis=0, keepdims=True)
            idx = jnp.min(jnp.where(scores[g] == mx, nidx_f, float(LANES)), axis=0, keepdims=True)
            hit = nidx_f == idx
            if k == SEL_TOPK - 2:
                hit = hit & (tq < SEL_LEN)
            selms[g] = jnp.where(hit, 1.0, selms[g])
            scores[g] = jnp.where(hit, -jnp.inf, scores[g])
    pad_lo = jnp.zeros((SEL_LANE0, TN), F32)
    pad_hi = jnp.zeros((LANES - SEL_LANE0 - n_sel, TN), F32)
    negsel = [jnp.transpose(jnp.concatenate([pad_lo, (sm - 1.0) * MASK_BIG, pad_hi], axis=0)) for sm in selms]
    lhs = [[(base[h] + negsel[h // hg]).astype(BF16) for h in range(nh)],
           [bh.astype(BF16) for bh in base]]

    back = WINDOW // TN

    def near_post(n_tiles, br, h, s):
        chunks = []
        for c in range(n_tiles):
            r, sc = n_tiles - 1 - c, s[:, c * TN:(c + 1) * TN]
            if r == 0:
                sc = jnp.where(causal, sc + tzd_ref[h, 0], NEG)
            elif r == 1:
                sc = sc + tzd_ref[h, 1]
            elif br == 1:
                sc = jnp.where(inside, sc, NEG)
            chunks.append(sc)
        return jnp.concatenate(chunks, axis=1)

    def near_jobs(n_tiles):
        off = pl.multiple_of(jnp.maximum(i - (n_tiles - 1), 0) * TN, TN)
        return [job for br in (1, 0)
                for job in tile_jobs((br,), off, functools.partial(near_post, n_tiles, br), n_tiles * TN)]

    for n_tiles in range(1, back + 2):
        @pl.when((i == n_tiles - 1) if n_tiles <= back else (i >= back))
        def _():
            process(near_jobs(n_tiles), True)

    n_far = jnp.maximum(i - back, 0)

    def sel_body(jj, carry):
        process(tile_jobs((0,), pl.multiple_of(2 * jj * TN, TN), lambda h, s: s, 2 * TN), False)
        return carry

    lax.fori_loop(0, n_far // 2, sel_body, 0)

    @pl.when(n_far % 2 == 1)
    def _():
        process(tile_jobs((0,), pl.multiple_of(jnp.maximum(n_far - 1, 0) * TN, TN), lambda h, s: s), False)

    width = hg * HEAD_DIM
    for g in range(ng):
        gates = 0.5 * jnp.tanh(0.5 * _dot(g_ref[0], gexp_ref[g])) + 0.5
        for c in range(hg // 2):
            g_cmp, g_sel, g_win = (gates[:, br * width + c * LANES:br * width + (c + 1) * LANES] for br in range(3))
            he, ho = g * hg + 2 * c, g * hg + 2 * c + 1
            out = (g_cmp * jnp.where(lo, o_cmp[g][hrows(2 * c)], o_cmp[g][hrows(2 * c + 1)])
                   + g_sel * pair_out(0, he, ho) + g_win * pair_out(1, he, ho))
            o_ref[0, :, (he // 2) * LANES:(he // 2 + 1) * LANES] = out.astype(BF16)


def _nsa(p3, kvc, bias_c, tzd, ovt, qaux, kaux, gexp):
    b, t, _ = p3.shape
    nq = t // TN
    width = NSA_HEADS * HEAD_DIM
    seq = lambda col: pl.BlockSpec((1, t, LANES), lambda bi, i, col=col: (bi, 0, col))
    const = lambda a: pl.BlockSpec(a.shape, lambda bi, i: (0,) * a.ndim, pipeline_mode=pl.Buffered(1))
    return pl.pallas_call(
        _nsa_kernel,
        grid=(b, nq),
        in_specs=[pl.BlockSpec((1, TN, width), lambda bi, i: (bi, i, _C_QN * LANES // width)),
                  seq(_C_KS), seq(_C_VS), seq(_C_KW), seq(_C_VW),
                  pl.BlockSpec((1, TN, LANES), lambda bi, i: (bi, i, _C_GATE)),
                  const(gexp),
                  pl.BlockSpec((1,) + kvc.shape[1:], lambda bi, i: (bi, 0, 0, 0)),
                  pl.BlockSpec((NSA_HEADS, 1, TN, LANES), lambda bi, i: (0, i, 0, 0)),
                  const(tzd), const(ovt), const(qaux), const(kaux)],
        out_specs=pl.BlockSpec((1, TN, width), lambda bi, i: (bi, i, 0)),
        out_shape=jax.ShapeDtypeStruct((b, t, width), BF16),
        scratch_shapes=[pltpu.VMEM((NSA_KV_HEADS, t, LANES), BF16)] * 4
                       + [pltpu.VMEM((2, NSA_HEADS, TN, LANES), F32)] * 2,
        compiler_params=pltpu.CompilerParams(dimension_semantics=("parallel", "arbitrary"),
                                             vmem_limit_bytes=VMEM_LIMIT),
        name="nsa",
    )(p3, p3, p3, p3, p3, p3, gexp, kvc, bias_c, tzd, ovt, qaux, kaux)


def _sb_kernel(q_ref, k_ref, v_ref, u_ref, o_ref, acc_ref, carry_ref):
    i = pl.program_id(1)
    nb = q_ref.shape[0]
    chains = [(bb, c) for bb in range(nb) for c in range(SB_HEADS // 2)]
    cols = [slice(c * LANES, (c + 1) * LANES) for c in range(SB_HEADS // 2)]
    lane = lax.broadcasted_iota(jnp.int32, (TS, LANES), 1)
    lo = lane < HEAD_DIM
    lane2 = lax.broadcasted_iota(jnp.int32, (2 * TS, TS), 1)
    row2 = lax.broadcasted_iota(jnp.int32, (2 * TS, TS), 0) & (TS - 1)
    causal = lane2 < row2

    qs = []
    for bb, c in chains:
        q = q_ref[bb, :, cols[c]].astype(F32)
        qs.append(jnp.concatenate([jnp.where(lo, q, 0.0), jnp.where(lo, 0.0, q)], axis=0).astype(BF16))

    acc_ref[...] = jnp.zeros(acc_ref.shape, F32)
    carry_ref[...] = jnp.zeros(carry_ref.shape, F32)

    def tile(j, diag):
        off = pl.multiple_of(j * TS, TS)
        zs = [_dot_nt(qs[n], k_ref[bb, pl.ds(off, TS), cols[c]]) for n, (bb, c) in enumerate(chains)]
        log_betas, log_keeps, k_bfs = [], [], []
        for z in zs:
            z_neg = jnp.minimum(z, 0.0)
            log_beta = z_neg - jnp.log(1.0 + jnp.exp2((z_neg + z_neg) - z)) * LOG2E
            log_keep = log_beta - z
            if diag:
                log_keep = jnp.where(causal, log_keep, 0.0)
            log_betas.append(log_beta)
            log_keeps.append(log_keep)
            k_bfs.append(log_keep.astype(BF16))
        laters = [_dot(k_bf, u_ref[...]) for k_bf in k_bfs]
        a_s = []
        for n, (bb, c) in enumerate(chains):
            carry = carry_ref[bb, c]
            a = jnp.exp2(log_betas[n] + laters[n] + jnp.concatenate([carry] * (TS // LANES), axis=1))
            if diag:
                a = jnp.where(causal, a, 0.0)
            a_s.append(a.astype(BF16))
            carry_ref[bb, c] = carry + jnp.sum(log_keeps[n], axis=1, keepdims=True)
        for n, (bb, c) in enumerate(chains):
            acc_ref[bb, c] += _dot(a_s[n], v_ref[bb, pl.ds(off, TS), cols[c]])

    tile(i, True)

    def body(n, carry):
        tile(i - 1 - n, False)
        return carry

    lax.fori_loop(0, i, body, 0)
    for bb, c in chains:
        o = acc_ref[bb, c]
        o_ref[bb, :, cols[c]] = jnp.where(lo, o[:TS], o[TS:]).astype(BF16)


def _sb(p3, u):
    b, t, _ = p3.shape
    width = SB_HEADS * HEAD_DIM
    nb = 2 if b % 2 == 0 else 1
    return pl.pallas_call(
        _sb_kernel,
        grid=(b // nb, t // TS),
        in_specs=[pl.BlockSpec((nb, TS, width), lambda bi, i: (bi, i, _C_SBQ * LANES // width)),
                  pl.BlockSpec((nb, t, width), lambda bi, i: (bi, 0, _C_SBK * LANES // width)),
                  pl.BlockSpec((nb, t, width), lambda bi, i: (bi, 0, _C_SBV * LANES // width)),
                  pl.BlockSpec((TS, TS), lambda bi, i: (0, 0))],
        out_specs=pl.BlockSpec((nb, TS, width), lambda bi, i: (bi, i, 0)),
        out_shape=jax.ShapeDtypeStruct((b, t, width), BF16),
        scratch_shapes=[pltpu.VMEM((nb, SB_HEADS // 2, 2 * TS, LANES), F32),
                        pltpu.VMEM((nb, SB_HEADS // 2, 2 * TS, LANES), F32)],
        compiler_params=pltpu.CompilerParams(dimension_semantics=("parallel", "arbitrary"),
                                             vmem_limit_bytes=VMEM_LIMIT),
        name="sb",
    )(p3, p3, p3, u)


def _tail_kernel(x_ref, on_ref, os_ref, ga_ref, wgm_ref, wun_ref, wus_ref, wo_ref, gm_ref,
                 w1_ref, w2_ref, gf_ref, o_ref, *, last_layer):
    x = x_ref[...]
    d = x.shape[1]
    h = _rms(x, ga_ref[...]).astype(BF16)
    gm = 1.0 / (1.0 + jnp.exp(-_dot(h, wgm_ref[...])))
    mixed = gm[:, :d] * _dot(on_ref[...], wun_ref[...]) + gm[:, d:] * _dot(os_ref[...], wus_ref[...])
    x1 = x + _dot(mixed.astype(BF16), wo_ref[...])
    h2 = _rms(x1, gm_ref[...]).astype(BF16)
    dff = w1_ref.shape[1]
    acc = jnp.zeros_like(x1)
    for f in range(0, dff, FF_CHUNK):
        u = jnp.maximum(_dot(h2, w1_ref[:, f:f + FF_CHUNK]), 0.0)
        acc = acc + _dot((u * u).astype(BF16), w2_ref[f:f + FF_CHUNK, :])
    x2 = x1 + acc
    o_ref[...] = _rms(x2, gf_ref[...]) if last_layer else x2


def _tail(x2, o_nsa, o_sb, g_attn, wgm, wun, wus, wo, g_mlp, w1, w2, g_final, last_layer):
    m, d = x2.shape
    tm = ROWS_TAIL
    tok = lambda w: pl.BlockSpec((tm, w), lambda i: (i, 0))
    const = lambda a: pl.BlockSpec(a.shape, lambda i: (0, 0), pipeline_mode=pl.Buffered(1))
    return pl.pallas_call(
        functools.partial(_tail_kernel, last_layer=last_layer),
        grid=(m // tm,),
        in_specs=[tok(d), tok(o_nsa.shape[1]), tok(o_sb.shape[1]), const(g_attn), const(wgm),
                  const(wun), const(wus), const(wo), const(g_mlp), const(w1), const(w2), const(g_final)],
        out_specs=tok(d),
        out_shape=jax.ShapeDtypeStruct((m, d), F32),
        compiler_params=pltpu.CompilerParams(dimension_semantics=("parallel",),
                                             vmem_limit_bytes=VMEM_LIMIT),
        name="tail",
    )(x2, o_nsa, o_sb, g_attn, wgm, wun, wus, wo, g_mlp, w1, w2, g_final)


def _t5_bucket_np(dist):
    n = np.maximum(dist, 0)
    max_exact = N_BUCKETS // 2
    nf = np.maximum(n, 1).astype(np.float32)
    large = max_exact + (np.log(nf / np.float32(max_exact)) / np.float32(math.log(MAX_DISTANCE / max_exact))
                         * np.float32(N_BUCKETS - max_exact)).astype(np.int32)
    return np.where(n < max_exact, n, np.minimum(large, N_BUCKETS - 1)).astype(np.int32)


def _bias_tables(rel_bias, t):
    tbl = rel_bias * LOG2E
    far_idx = int(_t5_bucket_np(np.asarray(MAX_DISTANCE)))
    far = tbl[far_idx]

    def lookup(idx, table):
        onehot = jnp.asarray(idx.reshape(-1)[:, None] == np.arange(N_BUCKETS)[None, :], F32)
        out = jnp.einsum("bh,nb->hn", table, onehot, precision=lax.Precision.HIGHEST)
        return out.reshape((table.shape[1],) + idx.shape)

    qi, kj = np.arange(TN)[:, None], np.arange(TN)[None, :]
    tzd = lookup(np.stack([_t5_bucket_np(r * TN + qi - kj) for r in range(2)]), tbl - far[None, :])
    dist_c = (np.arange(t).reshape(t // TN, TN, 1) - CMP_STRIDE * np.arange(LANES)[None, None, :]
              - (CMP_LEN - 1))
    bias_c = lookup(np.where(dist_c >= 0, _t5_bucket_np(dist_c), far_idx), tbl)
    far_hi = far.astype(BF16).astype(F32)
    qaux = jnp.zeros((rel_bias.shape[1], 1, LANES), F32)
    qaux = qaux.at[:, 0, BIAS_LANE].set(far_hi).at[:, 0, BIAS_LANE + 1].set(far - far_hi)
    return tzd, bias_c, qaux


def _const_tables(t):
    c_start = np.arange(LANES) * CMP_STRIDE
    s_start = np.arange(LANES) * SEL_LEN
    n_cmp = (t - CMP_LEN) // CMP_STRIDE + 1
    n_sel = t // SEL_LEN
    ov = ((c_start[:, None] < s_start[None, :] + SEL_LEN) & (c_start[:, None] + CMP_LEN > s_start[None, :])
          & (np.arange(LANES)[:, None] < n_cmp) & (np.arange(LANES)[None, :] < n_sel))
    kaux = np.zeros((2, t, LANES), np.float32)
    kaux[0, np.arange(t), SEL_LANE0 + np.arange(t) // SEL_LEN] = 1.0
    kaux[:, :, BIAS_LANE:BIAS_LANE + 2] = 1.0
    u = np.arange(TS)[:, None] > np.arange(TS)[None, :]
    gexp = np.zeros((NSA_KV_HEADS, LANES, 3 * NSA_GROUP * HEAD_DIM), np.float32)
    for h in range(NSA_HEADS):
        for br in range(3):
            c0 = (br * NSA_GROUP + h % NSA_GROUP) * HEAD_DIM
            gexp[h // NSA_GROUP, h * 3 + br, c0:c0 + HEAD_DIM] = 1.0
    return jnp.asarray(ov.T, BF16), jnp.asarray(kaux, BF16), jnp.asarray(u, BF16), jnp.asarray(gexp, BF16)


def _arrange_w_in(w):
    d = w.shape[0]
    nq, nkv, ng = NSA_HEADS * HEAD_DIM, 6 * NSA_KV_HEADS * HEAD_DIM, 3 * NSA_HEADS
    nsb = SB_HEADS * HEAD_DIM
    scale = HEAD_DIM ** -0.5 * LOG2E
    q = w[:, :nq] * scale
    kv = w[:, nq:nq + nkv]
    gl = w[:, nq + nkv:nq + nkv + ng]
    sb = w[:, nq + nkv + ng:nq + nkv + ng + 3 * nsb]
    gm = w[:, nq + nkv + ng + 3 * nsb:]
    kvw = NSA_KV_HEADS * HEAD_DIM
    assert kvw == LANES and ng <= LANES
    gates = jnp.pad(gl, ((0, 0), (0, LANES - ng)))
    cols = [sb[:, :nsb] * scale, sb[:, nsb:], q, kv[:, 2 * kvw:], gates, kv[:, :2 * kvw]]
    return jnp.concatenate(cols, axis=1).astype(BF16), gm.astype(BF16)


def kernel(x, norm_attn, w_in, rel_bias, cmp_k_pe, cmp_k_w1, cmp_k_b1, cmp_k_w2, cmp_v_pe, cmp_v_w1, cmp_v_b1, cmp_v_w2, w_up_nsa, w_up_sb, w_out, norm_mlp, w_ff1, w_ff2, norm_final):
    b, t, d = x.shape
    depth = w_in.shape[0]
    assert t % TN == 0 and t % TS == 0 and WINDOW == 2 * TN and MAX_DISTANCE <= TN
    assert t // SEL_LEN <= N_BUCKETS and (t - CMP_LEN) // CMP_STRIDE + 1 < LANES
    assert (b * t) % ROWS_INPROJ == 0 and (b * t) % ROWS_TAIL == 0
    tzd, bias_c, qaux = _bias_tables(rel_bias, t)
    ovt, kaux, u, gexp = _const_tables(t)
    half = CMP_LEN // 2 * HEAD_DIM
    xf = x.reshape(b * t, d)
    for layer in range(depth):
        w_arr, w_gm = _arrange_w_in(w_in[layer])
        assert w_arr.shape[1] == _N_CHUNKS * LANES
        proj = _inproj(xf, norm_attn[layer][None, :], w_arr).reshape(b, t, _N_CHUNKS * LANES)

        kvc = proj[:, :, _C_KVC * LANES:].reshape(b, t, 4, HEAD_DIM)
        kvc = jnp.transpose(kvc, (0, 2, 1, 3)).reshape(b, 4, t // CMP_STRIDE, CMP_STRIDE * HEAD_DIM)
        pe = jnp.stack([cmp_k_pe[layer], cmp_v_pe[layer]]).reshape(2, 1, CMP_LEN * HEAD_DIM)
        w1 = jnp.stack([cmp_k_w1[layer], cmp_v_w1[layer]]).astype(BF16)
        b1 = jnp.stack([cmp_k_b1[layer], cmp_v_b1[layer]])[:, None, :]
        w2 = jnp.stack([cmp_k_w2[layer], cmp_v_w2[layer]])
        w2d = jnp.concatenate([w2, w2], axis=2).astype(BF16)
        kvc = _compress(kvc, pe[:, :, :half], pe[:, :, half:], w1[:, :half], w1[:, half:], b1, w2d)

        o_nsa = _nsa(proj, kvc, bias_c, tzd, ovt, qaux, kaux, gexp)
        o_sb = _sb(proj, u)
        xf = _tail(xf, o_nsa.reshape(b * t, -1), o_sb.reshape(b * t, -1), norm_attn[layer][None, :], w_gm,
                   w_up_nsa[layer].astype(BF16), w_up_sb[layer].astype(BF16), w_out[layer].astype(BF16),
                   norm_mlp[layer][None, :], w_ff1[layer].astype(BF16), w_ff2[layer].astype(BF16),
                   norm_final[None, :], layer == depth - 1)
    return xf.reshape(b, t, d)
```

```python
import functools
import math

import numpy as np
import jax
import jax.numpy as jnp
from jax import lax
from jax.experimental import pallas as pl
from jax.experimental.pallas import tpu as pltpu

HEAD_DIM = 64
NSA_HEADS = 8
NSA_KV_HEADS = 2
NSA_GROUP = NSA_HEADS // NSA_KV_HEADS
SB_HEADS = 8
CMP_LEN = 32
CMP_STRIDE = 16
CMP_HIDDEN = 256
SEL_LEN = 64
SEL_TOPK = 4
WINDOW = 512
N_BUCKETS = 32
MAX_DISTANCE = 128
EPS = 1e-6
NEG = -1e30
FORCE_BONUS = 1e4

LANES = 128
TN = 256
TS = 256
ROWS_INPROJ = 1024
ROWS_TAIL = 512
FF_CHUNK = 512
MXU_COLS = 256
SETUP_ROWS = 512
VMEM_LIMIT = 56 * 1024 * 1024
LOG2E = math.log2(math.e)
MASK_BIG = 2.0 ** 60
SEL_LANE0 = HEAD_DIM
BIAS_LANE = HEAD_DIM + N_BUCKETS

F32 = jnp.float32
BF16 = jnp.bfloat16

_C_SBQ = 0
_C_SBK = 4
_C_SBV = 8
_C_QN = 12
_C_KS = 16
_C_VS = 17
_C_KW = 18
_C_VW = 19
_C_GATE = 20
_C_KVC = 21
_N_CHUNKS = 23


def _dot(a, b):
    return jnp.dot(a, b, preferred_element_type=F32)


def _dot_nt(a, b):
    return lax.dot_general(a, b, (((1,), (1,)), ((), ())), preferred_element_type=F32)


def _rms(x, g):
    return x * lax.rsqrt(jnp.mean(x * x, axis=-1, keepdims=True) + EPS) * g


def _inproj_kernel(x_ref, g_ref, w_ref, o_ref, kvc_ref):
    h = _rms(x_ref[...], g_ref[...]).astype(BF16)
    n = o_ref.shape[1]
    for c in range(0, n, MXU_COLS):
        cs = slice(c, min(c + MXU_COLS, n))
        o_ref[:, cs] = _dot(h, w_ref[:, cs]).astype(BF16)
    kvc_ref[...] = _dot(h, w_ref[:, n:]).astype(BF16)


def _inproj(x2, g, w):
    m, d = x2.shape
    n = _C_KVC * LANES
    nc = w.shape[1] - n
    tm = ROWS_INPROJ
    return pl.pallas_call(
        _inproj_kernel,
        grid=(m // tm,),
        in_specs=[pl.BlockSpec((tm, d), lambda i: (i, 0)),
                  pl.BlockSpec((1, d), lambda i: (0, 0)),
                  pl.BlockSpec(w.shape, lambda i: (0, 0), pipeline_mode=pl.Buffered(1))],
        out_specs=[pl.BlockSpec((tm, n), lambda i: (i, 0)), pl.BlockSpec((tm, nc), lambda i: (i, 0))],
        out_shape=[jax.ShapeDtypeStruct((m, n), BF16), jax.ShapeDtypeStruct((m, nc), BF16)],
        compiler_params=pltpu.CompilerParams(dimension_semantics=("parallel",),
                                             vmem_limit_bytes=VMEM_LIMIT),
        name="inproj",
    )(x2, g, w)


def _gelu_tanh(x):
    return 0.5 * x * (1.0 + jnp.tanh(math.sqrt(2.0 / math.pi) * (x + 0.044715 * (x * x * x))))


def _compress_kernel(x_ref, pea_ref, peb_ref, w1a_ref, w1b_ref, b1_ref, w2_ref, o_ref):
    for s in range(4):
        r = s // 2
        x = x_ref[0, s].astype(F32)
        a = _dot((x + pea_ref[r]).astype(BF16), w1a_ref[r])
        b = _dot((x + peb_ref[r]).astype(BF16), w1b_ref[r])
        hid = a + pltpu.roll(b, shift=b.shape[0] - 1, axis=0) + b1_ref[r]
        o_ref[0, s] = _dot(_gelu_tanh(hid).astype(BF16), w2_ref[r]).astype(BF16)


def _compress(xc, pea, peb, w1a, w1b, b1, w2d):
    b = xc.shape[0]
    nchunk, width = xc.shape[2], xc.shape[3]
    full = lambda shape: pl.BlockSpec(shape, lambda i: (0,) * len(shape))
    return pl.pallas_call(
        _compress_kernel,
        grid=(b,),
        in_specs=[pl.BlockSpec((1, 4, nchunk, width), lambda i: (i, 0, 0, 0)),
                  full(pea.shape), full(peb.shape), full(w1a.shape), full(w1b.shape),
                  full(b1.shape), full(w2d.shape)],
        out_specs=pl.BlockSpec((1, 4, nchunk, LANES), lambda i: (i, 0, 0, 0)),
        out_shape=jax.ShapeDtypeStruct((b, 4, nchunk, LANES), BF16),
        compiler_params=pltpu.CompilerParams(dimension_semantics=("parallel",),
                                             vmem_limit_bytes=VMEM_LIMIT),
        name="compress",
    )(xc, pea, peb, w1a, w1b, b1, w2d)


def _nsa_kernel(q_ref, ks_ref, vs_ref, kw_ref, vw_ref, g_ref, gexp_ref, kvc_ref,
                bc_ref, tzd_ref, ovt_ref, qaux_ref, kaux_ref, o_ref,
                ksx_ref, vsx_ref, kwx_ref, vwx_ref, m_ref, acc_ref):
    i = pl.program_id(1)
    ng, hg, nh = NSA_KV_HEADS, NSA_GROUP, NSA_HEADS
    t = ks_ref.shape[1]
    lane = lax.broadcasted_iota(jnp.int32, (TN, LANES), 1)
    row = lax.broadcasted_iota(jnp.int32, (TN, LANES), 0)
    lo = lane < HEAD_DIM
    t_abs = i * TN + row
    krow = lax.broadcasted_iota(jnp.int32, (TN, TN), 0)
    kcol = lax.broadcasted_iota(jnp.int32, (TN, TN), 1)

    def hrows(h):
        return slice(h * TN, (h + 1) * TN)

    @pl.when(i == 0)
    def _():
        ch = SETUP_ROWS
        lo_c = lax.broadcasted_iota(jnp.int32, (ch, LANES), 1) < HEAD_DIM
        for r0 in range(0, t, ch):
            rs = slice(r0, r0 + ch)
            for g in range(ng):
                def own(ref):
                    x = ref[0, rs].astype(F32)
                    return x if g == 0 else pltpu.roll(x, HEAD_DIM, 1)

                ksx_ref[g, rs] = jnp.where(lo_c, own(ks_ref), kaux_ref[0, rs].astype(F32)).astype(BF16)
                kwx_ref[g, rs] = jnp.where(lo_c, own(kw_ref), kaux_ref[1, rs].astype(F32)).astype(BF16)
                vsx_ref[g, rs] = jnp.where(lo_c, own(vs_ref), 1.0).astype(BF16)
                vwx_ref[g, rs] = jnp.where(lo_c, own(vw_ref), 1.0).astype(BF16)

    base = []
    for c in range(nh // 2):
        qc = q_ref[0, :, c * LANES:(c + 1) * LANES].astype(F32)
        for h, qh in ((2 * c, qc), (2 * c + 1, pltpu.roll(qc, HEAD_DIM, 1))):
            base.append(jnp.where(lo, qh, qaux_ref[h]))

    lo_c = lax.broadcasted_iota(jnp.int32, (LANES, LANES), 1) < HEAD_DIM
    valid_c = (t_abs >= CMP_STRIDE * lane + (CMP_LEN - 1)) & (lane < LANES - 1)
    s_cmp = [_dot_nt(jnp.concatenate(base[g * hg:(g + 1) * hg], axis=0).astype(BF16),
                     jnp.where(lo_c, kvc_ref[0, g].astype(F32), 0.0).astype(BF16)) for g in range(ng)]
    ps = []
    for h in range(nh):
        sh = jnp.where(valid_c, s_cmp[h // hg][hrows(h % hg)] + bc_ref[h, 0], NEG)
        e = jnp.exp2(sh - jnp.max(sh, axis=1, keepdims=True))
        ps.append(jnp.where(valid_c, e, 0.0) / jnp.sum(e, axis=1, keepdims=True))
    o_cmp = [_dot(jnp.concatenate(ps[g * hg:(g + 1) * hg], axis=0).astype(BF16), kvc_ref[0, ng + g])
             for g in range(ng)]

    n_sel = t // SEL_LEN
    nidx = lax.broadcasted_iota(jnp.int32, (n_sel, TN), 0)
    tq = i * TN + lax.broadcasted_iota(jnp.int32, (n_sel, TN), 1)
    forced = (nidx == tq // SEL_LEN) | (nidx == 0)
    sel_ok = nidx * SEL_LEN <= tq
    nidx_f = nidx.astype(F32)
    scores, selms = [], []
    for g in range(ng):
        pg = ps[g * hg:(g + 1) * hg]
        psum = (pg[0] + pg[1]) + (pg[2] + pg[3])
        p_hi = psum.astype(BF16)
        p_lo = (psum - p_hi.astype(F32)).astype(BF16)
        imp = (_dot_nt(ovt_ref[...], p_hi) + _dot_nt(ovt_ref[...], p_lo))[:n_sel]
        scores.append(jnp.where(forced, -jnp.inf, jnp.where(sel_ok, imp, -FORCE_BONUS)))
        selms.append(jnp.where(forced, 1.0, 0.0))

    def process(jobs, first):
        ss = [post(_dot_nt(lhs[br][h], kt)) for (br, h, kt, _, post) in jobs]
        ps, alphas = [], []
        for (br, h, _, _, _), sh in zip(jobs, ss):
            mx = jnp.max(sh, axis=1, keepdims=True)
            if first:
                m_new = jnp.broadcast_to(mx, (TN, LANES))
                alphas.append(None)
            else:
                m_old = m_ref[br, h]
                m_new = jnp.maximum(m_old, mx)
                alphas.append(jnp.exp2(m_old - m_new))
            m_ref[br, h] = m_new
            ps.append(jnp.exp2(sh - jnp.concatenate([m_new] * (sh.shape[1] // LANES), axis=1)).astype(BF16))
        for (br, h, _, vt, _), p, alpha in zip(jobs, ps, alphas):
            pv = _dot(p, vt)
            acc_ref[br, h] = pv if first else alpha * acc_ref[br, h] + pv

    def pair_out(br, he, ho):
        acc_e, acc_o = acc_ref[br, he], acc_ref[br, ho]
        num = jnp.where(lo, acc_e, pltpu.roll(acc_o, HEAD_DIM, 1))
        den = jnp.where(lo, pltpu.roll(acc_e, HEAD_DIM, 1), acc_o)
        return num / den

    kxs, vxs = (ksx_ref, kwx_ref), (vsx_ref, vwx_ref)
    causal = kcol <= krow
    inside = kcol > krow

    def tile_jobs(branches, off, post, width=TN):
        return [(br, h, kxs[br][h // hg, pl.ds(off, width), :], vxs[br][h // hg, pl.ds(off, width), :],
                 functools.partial(post, h)) for br in branches for h in range(nh)]

    for k in range(SEL_TOPK - 1):
        for g in range(ng):
            mx = jnp.max(scores[g], axis=0, keepdims=True)
            idx = jnp.min(jnp.where(scores[g] == mx, nidx_f, float(LANES)), axis=0, keepdims=True)
            hit = nidx_f == idx
            if k == SEL_TOPK - 2:
                hit = hit & (tq < SEL_LEN)
            selms[g] = jnp.where(hit, 1.0, selms[g])
            scores[g] = jnp.where(hit, -jnp.inf, scores[g])
    pad_lo = jnp.zeros((SEL_LANE0, TN), F32)
    pad_hi = jnp.zeros((LANES - SEL_LANE0 - n_sel, TN), F32)
    negsel = [jnp.transpose(jnp.concatenate([pad_lo, (sm - 1.0) * MASK_BIG, pad_hi], axis=0)) for sm in selms]
    lhs = [[(base[h] + negsel[h // hg]).astype(BF16) for h in range(nh)],
           [bh.astype(BF16) for bh in base]]

    back = WINDOW // TN

    def near_post(n_tiles, br, h, s):
        chunks = []
        for c in range(n_tiles):
            r, sc = n_tiles - 1 - c, s[:, c * TN:(c + 1) * TN]
            if r == 0:
                sc = jnp.where(causal, sc + tzd_ref[h, 0], NEG)
            elif r == 1:
                sc = sc + tzd_ref[h, 1]
            elif br == 1:
                sc = jnp.where(inside, sc, NEG)
            chunks.append(sc)
        return jnp.concatenate(chunks, axis=1)

    def near_jobs(n_tiles):
        off = pl.multiple_of(jnp.maximum(i - (n_tiles - 1), 0) * TN, TN)
        return [job for br in (1, 0)
                for job in tile_jobs((br,), off, functools.partial(near_post, n_tiles, br), n_tiles * TN)]

    for n_tiles in range(1, back + 2):
        @pl.when((i == n_tiles - 1) if n_tiles <= back else (i >= back))
        def _():
            process(near_jobs(n_tiles), True)

    n_far = jnp.maximum(i - back, 0)

    def sel_body(jj, carry):
        process(tile_jobs((0,), pl.multiple_of(2 * jj * TN, TN), lambda h, s: s, 2 * TN), False)
        return carry

    lax.fori_loop(0, n_far // 2, sel_body, 0)

    @pl.when(n_far % 2 == 1)
    def _():
        process(tile_jobs((0,), pl.multiple_of(jnp.maximum(n_far - 1, 0) * TN, TN), lambda h, s: s), False)

    width = hg * HEAD_DIM
    for g in range(ng):
        gates = 0.5 * jnp.tanh(0.5 * _dot(g_ref[0], gexp_ref[g])) + 0.5
        for c in range(hg // 2):
            g_cmp, g_sel, g_win = (gates[:, br * width + c * LANES:br * width + (c + 1) * LANES] for br in range(3))
            he, ho = g * hg + 2 * c, g * hg + 2 * c + 1
            out = (g_cmp * jnp.where(lo, o_cmp[g][hrows(2 * c)], o_cmp[g][hrows(2 * c + 1)])
                   + g_sel * pair_out(0, he, ho) + g_win * pair_out(1, he, ho))
            o_ref[0, :, (he // 2) * LANES:(he // 2 + 1) * LANES] = out.astype(BF16)


def _nsa(p3, kvc, bias_c, tzd, ovt, qaux, kaux, gexp):
    b, t, _ = p3.shape
    nq = t // TN
    width = NSA_HEADS * HEAD_DIM
    seq = lambda col: pl.BlockSpec((1, t, LANES), lambda bi, i, col=col: (bi, 0, col))
    const = lambda a: pl.BlockSpec(a.shape, lambda bi, i: (0,) * a.ndim, pipeline_mode=pl.Buffered(1))
    return pl.pallas_call(
        _nsa_kernel,
        grid=(b, nq),
        in_specs=[pl.BlockSpec((1, TN, width), lambda bi, i: (bi, i, _C_QN * LANES // width)),
                  seq(_C_KS), seq(_C_VS), seq(_C_KW), seq(_C_VW),
                  pl.BlockSpec((1, TN, LANES), lambda bi, i: (bi, i, _C_GATE)),
                  const(gexp),
                  pl.BlockSpec((1,) + kvc.shape[1:], lambda bi, i: (bi, 0, 0, 0)),
                  pl.BlockSpec((NSA_HEADS, 1, TN, LANES), lambda bi, i: (0, i, 0, 0)),
                  const(tzd), const(ovt), const(qaux), const(kaux)],
        out_specs=pl.BlockSpec((1, TN, width), lambda bi, i: (bi, i, 0)),
        out_shape=jax.ShapeDtypeStruct((b, t, width), BF16),
        scratch_shapes=[pltpu.VMEM((NSA_KV_HEADS, t, LANES), BF16)] * 4
                       + [pltpu.VMEM((2, NSA_HEADS, TN, LANES), F32)] * 2,
        compiler_params=pltpu.CompilerParams(dimension_semantics=("parallel", "arbitrary"),
                                             vmem_limit_bytes=VMEM_LIMIT),
        name="nsa",
    )(p3, p3, p3, p3, p3, p3, gexp, kvc, bias_c, tzd, ovt, qaux, kaux)


def _sb_kernel(q_ref, k_ref, v_ref, u_ref, o_ref, acc_ref, carry_ref):
    i = pl.program_id(1)
    nb = q_ref.shape[0]
    chains = [(bb, c) for bb in range(nb) for c in range(SB_HEADS // 2)]
    cols = [slice(c * LANES, (c + 1) * LANES) for c in range(SB_HEADS // 2)]
    lane = lax.broadcasted_iota(jnp.int32, (TS, LANES), 1)
    lo = lane < HEAD_DIM
    lane2 = lax.broadcasted_iota(jnp.int32, (2 * TS, TS), 1)
    row2 = lax.broadcasted_iota(jnp.int32, (2 * TS, TS), 0) & (TS - 1)
    causal = lane2 < row2

    qs = []
    for bb, c in chains:
        q = q_ref[bb, :, cols[c]].astype(F32)
        qs.append(jnp.concatenate([jnp.where(lo, q, 0.0), jnp.where(lo, 0.0, q)], axis=0).astype(BF16))

    acc_ref[...] = jnp.zeros(acc_ref.shape, F32)
    carry_ref[...] = jnp.zeros(carry_ref.shape, F32)

    def tile(j, diag):
        off = pl.multiple_of(j * TS, TS)
        zs = [_dot_nt(qs[n], k_ref[bb, pl.ds(off, TS), cols[c]]) for n, (bb, c) in enumerate(chains)]
        log_betas, log_keeps, k_bfs = [], [], []
        for z in zs:
            z_neg = jnp.minimum(z, 0.0)
            log_beta = z_neg - jnp.log(1.0 + jnp.exp2((z_neg + z_neg) - z)) * LOG2E
            log_keep = log_beta - z
            if diag:
                log_keep = jnp.where(causal, log_keep, 0.0)
            log_betas.append(log_beta)
            log_keeps.append(log_keep)
            k_bfs.append(log_keep.astype(BF16))
        laters = [_dot(k_bf, u_ref[...]) for k_bf in k_bfs]
        a_s = []
        for n, (bb, c) in enumerate(chains):
            carry = carry_ref[bb, c]
            a = jnp.exp2(log_betas[n] + laters[n] + jnp.concatenate([carry] * (TS // LANES), axis=1))
            if diag:
                a = jnp.where(causal, a, 0.0)
            a_s.append(a.astype(BF16))
            carry_ref[bb, c] = carry + jnp.sum(log_keeps[n], axis=1, keepdims=True)
        for n, (bb, c) in enumerate(chains):
            acc_ref[bb, c] += _dot(a_s[n], v_ref[bb, pl.ds(off, TS), cols[c]])

    tile(i, True)

    def body(n, carry):
        tile(i - 1 - n, False)
        return carry

    lax.fori_loop(0, i, body, 0)
    for bb, c in chains:
        o = acc_ref[bb, c]
        o_ref[bb, :, cols[c]] = jnp.where(lo, o[:TS], o[TS:]).astype(BF16)


def _sb(p3, u):
    b, t, _ = p3.shape
    width = SB_HEADS * HEAD_DIM
    nb = 2 if b % 2 == 0 else 1
    return pl.pallas_call(
        _sb_kernel,
        grid=(b // nb, t // TS),
        in_specs=[pl.BlockSpec((nb, TS, width), lambda bi, i: (bi, i, _C_SBQ * LANES // width)),
                  pl.BlockSpec((nb, t, width), lambda bi, i: (bi, 0, _C_SBK * LANES // width)),
                  pl.BlockSpec((nb, t, width), lambda bi, i: (bi, 0, _C_SBV * LANES // width)),
                  pl.BlockSpec((TS, TS), lambda bi, i: (0, 0))],
        out_specs=pl.BlockSpec((nb, TS, width), lambda bi, i: (bi, i, 0)),
        out_shape=jax.ShapeDtypeStruct((b, t, width), BF16),
        scratch_shapes=[pltpu.VMEM((nb, SB_HEADS // 2, 2 * TS, LANES), F32),
                        pltpu.VMEM((nb, SB_HEADS // 2, 2 * TS, LANES), F32)],
        compiler_params=pltpu.CompilerParams(dimension_semantics=("parallel", "arbitrary"),
                                             vmem_limit_bytes=VMEM_LIMIT),
        name="sb",
    )(p3, p3, p3, u)


def _tail_kernel(x_ref, on_ref, os_ref, ga_ref, wgm_ref, wun_ref, wus_ref, wo_ref, gm_ref,
                 w1_ref, w2_ref, gf_ref, o_ref, *, last_layer):
    x = x_ref[...]
    d = x.shape[1]
    h = _rms(x, ga_ref[...]).astype(BF16)
    gm = 1.0 / (1.0 + jnp.exp(-_dot(h, wgm_ref[...])))
    mixed = gm[:, :d] * _dot(on_ref[...], wun_ref[...]) + gm[:, d:] * _dot(os_ref[...], wus_ref[...])
    x1 = x + _dot(mixed.astype(BF16), wo_ref[...])
    h2 = _rms(x1, gm_ref[...]).astype(BF16)
    dff = w1_ref.shape[1]
    acc = jnp.zeros_like(x1)
    for f in range(0, dff, FF_CHUNK):
        u = jnp.maximum(_dot(h2, w1_ref[:, f:f + FF_CHUNK]), 0.0)
        acc = acc + _dot((u * u).astype(BF16), w2_ref[f:f + FF_CHUNK, :])
    x2 = x1 + acc
    o_ref[...] = _rms(x2, gf_ref[...]) if last_layer else x2


def _tail(x2, o_nsa, o_sb, g_attn, wgm, wun, wus, wo, g_mlp, w1, w2, g_final, last_layer):
    m, d = x2.shape
    tm = ROWS_TAIL
    tok = lambda w: pl.BlockSpec((tm, w), lambda i: (i, 0))
    const = lambda a: pl.BlockSpec(a.shape, lambda i: (0, 0), pipeline_mode=pl.Buffered(1))
    return pl.pallas_call(
        functools.partial(_tail_kernel, last_layer=last_layer),
        grid=(m // tm,),
        in_specs=[tok(d), tok(o_nsa.shape[1]), tok(o_sb.shape[1]), const(g_attn), const(wgm),
                  const(wun), const(wus), const(wo), const(g_mlp), const(w1), const(w2), const(g_final)],
        out_specs=tok(d),
        out_shape=jax.ShapeDtypeStruct((m, d), F32),
        compiler_params=pltpu.CompilerParams(dimension_semantics=("parallel",),
                                             vmem_limit_bytes=VMEM_LIMIT),
        name="tail",
    )(x2, o_nsa, o_sb, g_attn, wgm, wun, wus, wo, g_mlp, w1, w2, g_final)


def _t5_bucket_np(dist):
    n = np.maximum(dist, 0)
    max_exact = N_BUCKETS // 2
    nf = np.maximum(n, 1).astype(np.float32)
    large = max_exact + (np.log(nf / np.float32(max_exact)) / np.float32(math.log(MAX_DISTANCE / max_exact))
                         * np.float32(N_BUCKETS - max_exact)).astype(np.int32)
    return np.where(n < max_exact, n, np.minimum(large, N_BUCKETS - 1)).astype(np.int32)


def _bias_tables(rel_bias, t):
    tbl = rel_bias * LOG2E
    far_idx = int(_t5_bucket_np(np.asarray(MAX_DISTANCE)))
    far = tbl[far_idx]

    def lookup(idx, table):
        onehot = jnp.asarray(idx.reshape(-1)[:, None] == np.arange(N_BUCKETS)[None, :], BF16)
        out = jnp.einsum("bh,nb->hn", table, onehot, precision=lax.Precision.HIGHEST)
        return out.reshape((table.shape[1],) + idx.shape)

    qi, kj = np.arange(TN)[:, None], np.arange(TN)[None, :]
    tzd = lookup(np.stack([_t5_bucket_np(r * TN + qi - kj) for r in range(2)]), tbl - far[None, :])
    dist_c = (np.arange(t).reshape(t // TN, TN, 1) - CMP_STRIDE * np.arange(LANES)[None, None, :]
              - (CMP_LEN - 1))
    bias_c = lookup(np.where(dist_c >= 0, _t5_bucket_np(dist_c), far_idx), tbl)
    far_hi = far.astype(BF16).astype(F32)
    qaux = jnp.zeros((rel_bias.shape[1], 1, LANES), F32)
    qaux = qaux.at[:, 0, BIAS_LANE].set(far_hi).at[:, 0, BIAS_LANE + 1].set(far - far_hi)
    return tzd, bias_c, qaux


def _const_tables(t):
    c_start = np.arange(LANES) * CMP_STRIDE
    s_start = np.arange(LANES) * SEL_LEN
    n_cmp = (t - CMP_LEN) // CMP_STRIDE + 1
    n_sel = t // SEL_LEN
    ov = ((c_start[:, None] < s_start[None, :] + SEL_LEN) & (c_start[:, None] + CMP_LEN > s_start[None, :])
          & (np.arange(LANES)[:, None] < n_cmp) & (np.arange(LANES)[None, :] < n_sel))
    kaux = np.zeros((2, t, LANES), np.float32)
    kaux[0, np.arange(t), SEL_LANE0 + np.arange(t) // SEL_LEN] = 1.0
    kaux[:, :, BIAS_LANE:BIAS_LANE + 2] = 1.0
    u = np.arange(TS)[:, None] > np.arange(TS)[None, :]
    gexp = np.zeros((NSA_KV_HEADS, LANES, 3 * NSA_GROUP * HEAD_DIM), np.float32)
    for h in range(NSA_HEADS):
        for br in range(3):
            c0 = (br * NSA_GROUP + h % NSA_GROUP) * HEAD_DIM
            gexp[h // NSA_GROUP, h * 3 + br, c0:c0 + HEAD_DIM] = 1.0
    return jnp.asarray(ov.T, BF16), jnp.asarray(kaux, BF16), jnp.asarray(u, BF16), jnp.asarray(gexp, BF16)


def _arrange_w_in(w):
    d = w.shape[0]
    nq, nkv, ng = NSA_HEADS * HEAD_DIM, 6 * NSA_KV_HEADS * HEAD_DIM, 3 * NSA_HEADS
    nsb = SB_HEADS * HEAD_DIM
    scale = HEAD_DIM ** -0.5 * LOG2E
    q = w[:, :nq] * scale
    kv = w[:, nq:nq + nkv]
    gl = w[:, nq + nkv:nq + nkv + ng]
    sb = w[:, nq + nkv + ng:nq + nkv + ng + 3 * nsb]
    gm = w[:, nq + nkv + ng + 3 * nsb:]
    kvw = NSA_KV_HEADS * HEAD_DIM
    assert kvw == LANES and ng <= LANES
    gates = jnp.pad(gl, ((0, 0), (0, LANES - ng)))
    cols = [sb[:, :nsb] * scale, sb[:, nsb:], q, kv[:, 2 * kvw:], gates, kv[:, :2 * kvw]]
    return jnp.concatenate(cols, axis=1).astype(BF16), gm.astype(BF16)


def kernel(x, norm_attn, w_in, rel_bias, cmp_k_pe, cmp_k_w1, cmp_k_b1, cmp_k_w2, cmp_v_pe, cmp_v_w1, cmp_v_b1, cmp_v_w2, w_up_nsa, w_up_sb, w_out, norm_mlp, w_ff1, w_ff2, norm_final):
    b, t, d = x.shape
    depth = w_in.shape[0]
    assert t % TN == 0 and t % TS == 0 and WINDOW == 2 * TN and MAX_DISTANCE <= TN
    assert t // SEL_LEN <= N_BUCKETS and (t - CMP_LEN) // CMP_STRIDE + 1 < LANES
    assert (b * t) % ROWS_INPROJ == 0 and (b * t) % ROWS_TAIL == 0
    tzd, bias_c, qaux = _bias_tables(rel_bias, t)
    ovt, kaux, u, gexp = _const_tables(t)
    half = CMP_LEN // 2 * HEAD_DIM
    xf = x.reshape(b * t, d)
    for layer in range(depth):
        w_arr, w_gm = _arrange_w_in(w_in[layer])
        assert w_arr.shape[1] == _N_CHUNKS * LANES
        proj, kvc = _inproj(xf, norm_attn[layer][None, :], w_arr)
        proj = proj.reshape(b, t, -1)

        kvc = kvc.reshape(b, t, 4, HEAD_DIM)
        kvc = jnp.transpose(kvc, (0, 2, 1, 3)).reshape(b, 4, t // CMP_STRIDE, CMP_STRIDE * HEAD_DIM)
        pe = jnp.stack([cmp_k_pe[layer], cmp_v_pe[layer]]).reshape(2, 1, CMP_LEN * HEAD_DIM)
        w1 = jnp.stack([cmp_k_w1[layer], cmp_v_w1[layer]]).astype(BF16)
        b1 = jnp.stack([cmp_k_b1[layer], cmp_v_b1[layer]])[:, None, :]
        w2 = jnp.stack([cmp_k_w2[layer], cmp_v_w2[layer]])
        w2d = jnp.concatenate([w2, w2], axis=2).astype(BF16)
        kvc = _compress(kvc, pe[:, :, :half], pe[:, :, half:], w1[:, :half], w1[:, half:], b1, w2d)

        o_nsa = _nsa(proj, kvc, bias_c, tzd, ovt, qaux, kaux, gexp)
        o_sb = _sb(proj, u)
        xf = _tail(xf, o_nsa.reshape(b * t, -1), o_sb.reshape(b * t, -1), norm_attn[layer][None, :], w_gm,
                   w_up_nsa[layer].astype(BF16), w_up_sb[layer].astype(BF16), w_out[layer].astype(BF16),
                   norm_mlp[layer][None, :], w_ff1[layer].astype(BF16), w_ff2[layer].astype(BF16),
                   norm_final[None, :], layer == depth - 1)
    return xf.reshape(b, t, d)
```

```python
import functools
import math

import numpy as np
import jax
import jax.numpy as jnp
from jax import lax
from jax.experimental import pallas as pl
from jax.experimental.pallas import tpu as pltpu

HEAD_DIM = 64
NSA_HEADS = 8
NSA_KV_HEADS = 2
NSA_GROUP = NSA_HEADS // NSA_KV_HEADS
SB_HEADS = 8
CMP_LEN = 32
CMP_STRIDE = 16
CMP_HIDDEN = 256
SEL_LEN = 64
SEL_TOPK = 4
WINDOW = 512
N_BUCKETS = 32
MAX_DISTANCE = 128
EPS = 1e-6
NEG = -1e30
FORCE_BONUS = 1e4

LANES = 128
TN = 256
TS = 256
ROWS_INPROJ = 1024
ROWS_TAIL = 512
FF_CHUNK = 512
MXU_COLS = 256
SETUP_ROWS = 512
VMEM_LIMIT = 56 * 1024 * 1024
LOG2E = math.log2(math.e)
MASK_BIG = 2.0 ** 60
SEL_LANE0 = HEAD_DIM
BIAS_LANE = HEAD_DIM + N_BUCKETS

F32 = jnp.float32
BF16 = jnp.bfloat16

_C_SBQ = 0
_C_SBK = 4
_C_SBV = 8
_C_QN = 12
_C_KS = 16
_C_VS = 17
_C_KW = 18
_C_VW = 19
_C_GATE = 20
_C_KVC = 21
_N_CHUNKS = 23


def _dot(a, b):
    return jnp.dot(a, b, preferred_element_type=F32)


def _dot_nt(a, b):
    return lax.dot_general(a, b, (((1,), (1,)), ((), ())), preferred_element_type=F32)


def _rms(x, g):
    return x * lax.rsqrt(jnp.mean(x * x, axis=-1, keepdims=True) + EPS) * g


def _inproj_kernel(x_ref, g_ref, w_ref, o_ref, kvc_ref):
    h = _rms(x_ref[...], g_ref[...]).astype(BF16)
    n = o_ref.shape[1]
    for c in range(0, n, MXU_COLS):
        cs = slice(c, min(c + MXU_COLS, n))
        o_ref[:, cs] = _dot(h, w_ref[:, cs]).astype(BF16)
    kvc_ref[...] = _dot(h, w_ref[:, n:]).astype(BF16)


def _inproj(x2, g, w):
    m, d = x2.shape
    n = _C_KVC * LANES
    nc = w.shape[1] - n
    tm = ROWS_INPROJ
    return pl.pallas_call(
        _inproj_kernel,
        grid=(m // tm,),
        in_specs=[pl.BlockSpec((tm, d), lambda i: (i, 0)),
                  pl.BlockSpec((1, d), lambda i: (0, 0)),
                  pl.BlockSpec(w.shape, lambda i: (0, 0), pipeline_mode=pl.Buffered(1))],
        out_specs=[pl.BlockSpec((tm, n), lambda i: (i, 0)), pl.BlockSpec((tm, nc), lambda i: (i, 0))],
        out_shape=[jax.ShapeDtypeStruct((m, n), BF16), jax.ShapeDtypeStruct((m, nc), BF16)],
        compiler_params=pltpu.CompilerParams(dimension_semantics=("parallel",),
                                             vmem_limit_bytes=VMEM_LIMIT),
        name="inproj",
    )(x2, g, w)


def _gelu_tanh(x):
    return 0.5 * x * (1.0 + jnp.tanh(math.sqrt(2.0 / math.pi) * (x + 0.044715 * (x * x * x))))


def _compress_kernel(x_ref, pea_ref, peb_ref, w1a_ref, w1b_ref, b1_ref, w2_ref, o_ref):
    for s in range(4):
        r = s // 2
        x = x_ref[0, s].astype(F32)
        a = _dot((x + pea_ref[r]).astype(BF16), w1a_ref[r])
        b = _dot((x + peb_ref[r]).astype(BF16), w1b_ref[r])
        hid = a + pltpu.roll(b, shift=b.shape[0] - 1, axis=0) + b1_ref[r]
        o_ref[0, s] = _dot(_gelu_tanh(hid).astype(BF16), w2_ref[r]).astype(BF16)


def _compress(xc, pea, peb, w1a, w1b, b1, w2d):
    b = xc.shape[0]
    nchunk, width = xc.shape[2], xc.shape[3]
    full = lambda shape: pl.BlockSpec(shape, lambda i: (0,) * len(shape))
    return pl.pallas_call(
        _compress_kernel,
        grid=(b,),
        in_specs=[pl.BlockSpec((1, 4, nchunk, width), lambda i: (i, 0, 0, 0)),
                  full(pea.shape), full(peb.shape), full(w1a.shape), full(w1b.shape),
                  full(b1.shape), full(w2d.shape)],
        out_specs=pl.BlockSpec((1, 4, nchunk, LANES), lambda i: (i, 0, 0, 0)),
        out_shape=jax.ShapeDtypeStruct((b, 4, nchunk, LANES), BF16),
        compiler_params=pltpu.CompilerParams(dimension_semantics=("parallel",),
                                             vmem_limit_bytes=VMEM_LIMIT),
        name="compress",
    )(xc, pea, peb, w1a, w1b, b1, w2d)


def _nsa_kernel(q_ref, ks_ref, vs_ref, kw_ref, vw_ref, g_ref, gexp_ref, kvc_ref,
                bc_ref, tzd_ref, ovt_ref, qaux_ref, kaux_ref, o_ref,
                ksx_ref, vsx_ref, kwx_ref, vwx_ref, m_ref, acc_ref):
    i = pl.program_id(1)
    ng, hg, nh = NSA_KV_HEADS, NSA_GROUP, NSA_HEADS
    t = ks_ref.shape[1]
    lane = lax.broadcasted_iota(jnp.int32, (TN, LANES), 1)
    row = lax.broadcasted_iota(jnp.int32, (TN, LANES), 0)
    lo = lane < HEAD_DIM
    t_abs = i * TN + row
    krow = lax.broadcasted_iota(jnp.int32, (TN, TN), 0)
    kcol = lax.broadcasted_iota(jnp.int32, (TN, TN), 1)

    def hrows(h):
        return slice(h * TN, (h + 1) * TN)

    @pl.when(i == 0)
    def _():
        ch = SETUP_ROWS
        lo_c = lax.broadcasted_iota(jnp.int32, (ch, LANES), 1) < HEAD_DIM
        for r0 in range(0, t, ch):
            rs = slice(r0, r0 + ch)
            for g in range(ng):
                def own(ref):
                    x = ref[0, rs].astype(F32)
                    return x if g == 0 else pltpu.roll(x, HEAD_DIM, 1)

                ksx_ref[g, rs] = jnp.where(lo_c, own(ks_ref), kaux_ref[0, rs].astype(F32)).astype(BF16)
                kwx_ref[g, rs] = jnp.where(lo_c, own(kw_ref), kaux_ref[1, rs].astype(F32)).astype(BF16)
                vsx_ref[g, rs] = jnp.where(lo_c, own(vs_ref), 1.0).astype(BF16)
                vwx_ref[g, rs] = jnp.where(lo_c, own(vw_ref), 1.0).astype(BF16)

    base = []
    for c in range(nh // 2):
        qc = q_ref[0, :, c * LANES:(c + 1) * LANES].astype(F32)
        for h, qh in ((2 * c, qc), (2 * c + 1, pltpu.roll(qc, HEAD_DIM, 1))):
            base.append(jnp.where(lo, qh, qaux_ref[h]))

    lo_c = lax.broadcasted_iota(jnp.int32, (LANES, LANES), 1) < HEAD_DIM
    valid_c = (t_abs >= CMP_STRIDE * lane + (CMP_LEN - 1)) & (lane < LANES - 1)
    s_cmp = [_dot_nt(jnp.concatenate(base[g * hg:(g + 1) * hg], axis=0).astype(BF16),
                     jnp.where(lo_c, kvc_ref[0, g].astype(F32), 0.0).astype(BF16)) for g in range(ng)]
    ps = []
    for h in range(nh):
        sh = jnp.where(valid_c, s_cmp[h // hg][hrows(h % hg)] + bc_ref[h, 0], NEG)
        e = jnp.exp2(sh - jnp.max(sh, axis=1, keepdims=True))
        ps.append(jnp.where(valid_c, e, 0.0) / jnp.sum(e, axis=1, keepdims=True))
    o_cmp = [_dot(jnp.concatenate(ps[g * hg:(g + 1) * hg], axis=0).astype(BF16), kvc_ref[0, ng + g])
             for g in range(ng)]

    n_sel = t // SEL_LEN
    nidx = lax.broadcasted_iota(jnp.int32, (n_sel, TN), 0)
    tq = i * TN + lax.broadcasted_iota(jnp.int32, (n_sel, TN), 1)
    forced = (nidx == tq // SEL_LEN) | (nidx == 0)
    sel_ok = nidx * SEL_LEN <= tq
    nidx_f = nidx.astype(F32)
    scores, selms = [], []
    for g in range(ng):
        pg = ps[g * hg:(g + 1) * hg]
        psum = (pg[0] + pg[1]) + (pg[2] + pg[3])
        p_hi = psum.astype(BF16)
        p_lo = (psum - p_hi.astype(F32)).astype(BF16)
        imp = (_dot_nt(ovt_ref[...], p_hi) + _dot_nt(ovt_ref[...], p_lo))[:n_sel]
        scores.append(jnp.where(forced, -jnp.inf, jnp.where(sel_ok, imp, -FORCE_BONUS)))
        selms.append(jnp.where(forced, 1.0, 0.0))

    def process(jobs, first):
        ss = [post(_dot_nt(lhs[br][h], kt)) for (br, h, kt, _, post) in jobs]
        ps, alphas = [], []
        for (br, h, _, _, _), sh in zip(jobs, ss):
            mx = jnp.max(sh, axis=1, keepdims=True)
            if first:
                m_new = jnp.broadcast_to(mx, (TN, LANES))
                alphas.append(None)
            else:
                m_old = m_ref[br, h]
                m_new = jnp.maximum(m_old, mx)
                alphas.append(jnp.exp2(m_old - m_new))
            m_ref[br, h] = m_new
            ps.append(jnp.exp2(sh - jnp.concatenate([m_new] * (sh.shape[1] // LANES), axis=1)).astype(BF16))
        for (br, h, _, vt, _), p, alpha in zip(jobs, ps, alphas):
            pv = _dot(p, vt)
            acc_ref[br, h] = pv if first else alpha * acc_ref[br, h] + pv

    def pair_out(br, he, ho):
        acc_e, acc_o = acc_ref[br, he], acc_ref[br, ho]
        num = jnp.where(lo, acc_e, pltpu.roll(acc_o, HEAD_DIM, 1))
        den = jnp.where(lo, pltpu.roll(acc_e, HEAD_DIM, 1), acc_o)
        return num / den

    kxs, vxs = (ksx_ref, kwx_ref), (vsx_ref, vwx_ref)
    causal = kcol <= krow
    inside = kcol > krow

    def tile_jobs(branches, off, post, width=TN):
        return [(br, h, kxs[br][h // hg, pl.ds(off, width), :], vxs[br][h // hg, pl.ds(off, width), :],
                 functools.partial(post, h)) for br in branches for h in range(nh)]

    for k in range(SEL_TOPK - 1):
        for g in range(ng):
            mx = jnp.max(scores[g], axis=0, keepdims=True)
            idx = jnp.min(jnp.where(scores[g] == mx, nidx_f, float(LANES)), axis=0, keepdims=True)
            hit = nidx_f == idx
            if k == SEL_TOPK - 2:
                hit = hit & (tq < SEL_LEN)
            selms[g] = jnp.where(hit, 1.0, selms[g])
            scores[g] = jnp.where(hit, -jnp.inf, scores[g])
    pad_lo = jnp.zeros((SEL_LANE0, TN), F32)
    pad_hi = jnp.zeros((LANES - SEL_LANE0 - n_sel, TN), F32)
    negsel = [jnp.transpose(jnp.concatenate([pad_lo, (sm - 1.0) * MASK_BIG, pad_hi], axis=0)) for sm in selms]
    lhs = [[(base[h] + negsel[h // hg]).astype(BF16) for h in range(nh)],
           [bh.astype(BF16) for bh in base]]

    back = WINDOW // TN

    def near_post(n_tiles, br, h, s):
        chunks = []
        for c in range(n_tiles):
            r, sc = n_tiles - 1 - c, s[:, c * TN:(c + 1) * TN]
            if r == 0:
                sc = jnp.where(causal, sc + tzd_ref[h, 0], NEG)
            elif r == 1:
                sc = sc + tzd_ref[h, 1]
            elif br == 1:
                sc = jnp.where(inside, sc, NEG)
            chunks.append(sc)
        return jnp.concatenate(chunks, axis=1)

    def near_jobs(n_tiles):
        off = pl.multiple_of(jnp.maximum(i - (n_tiles - 1), 0) * TN, TN)
        return [job for br in (1, 0)
                for job in tile_jobs((br,), off, functools.partial(near_post, n_tiles, br), n_tiles * TN)]

    for n_tiles in range(1, back + 2):
        @pl.when((i == n_tiles - 1) if n_tiles <= back else (i >= back))
        def _():
            process(near_jobs(n_tiles), True)

    n_far = jnp.maximum(i - back, 0)

    def sel_body(jj, carry):
        process(tile_jobs((0,), pl.multiple_of(2 * jj * TN, TN), lambda h, s: s, 2 * TN), False)
        return carry

    lax.fori_loop(0, n_far // 2, sel_body, 0)

    @pl.when(n_far % 2 == 1)
    def _():
        process(tile_jobs((0,), pl.multiple_of(jnp.maximum(n_far - 1, 0) * TN, TN), lambda h, s: s), False)

    width = hg * HEAD_DIM
    for g in range(ng):
        gates = 0.5 * jnp.tanh(0.5 * _dot(g_ref[0], gexp_ref[g])) + 0.5
        for c in range(hg // 2):
            g_cmp, g_sel, g_win = (gates[:, br * width + c * LANES:br * width + (c + 1) * LANES] for br in range(3))
            he, ho = g * hg + 2 * c, g * hg + 2 * c + 1
            out = (g_cmp * jnp.where(lo, o_cmp[g][hrows(2 * c)], o_cmp[g][hrows(2 * c + 1)])
                   + g_sel * pair_out(0, he, ho) + g_win * pair_out(1, he, ho))
            o_ref[0, :, (he // 2) * LANES:(he // 2 + 1) * LANES] = out.astype(BF16)


def _nsa(p3, kvc, bias_c, tzd, ovt, qaux, kaux, gexp):
    b, t, _ = p3.shape
    nq = t // TN
    width = NSA_HEADS * HEAD_DIM
    seq = lambda col: pl.BlockSpec((1, t, LANES), lambda bi, i, col=col: (bi, 0, col))
    const = lambda a: pl.BlockSpec(a.shape, lambda bi, i: (0,) * a.ndim, pipeline_mode=pl.Buffered(1))
    return pl.pallas_call(
        _nsa_kernel,
        grid=(b, nq),
        in_specs=[pl.BlockSpec((1, TN, width), lambda bi, i: (bi, i, _C_QN * LANES // width)),
                  seq(_C_KS), seq(_C_VS), seq(_C_KW), seq(_C_VW),
                  pl.BlockSpec((1, TN, LANES), lambda bi, i: (bi, i, _C_GATE)),
                  const(gexp),
                  pl.BlockSpec((1,) + kvc.shape[1:], lambda bi, i: (bi, 0, 0, 0)),
                  pl.BlockSpec((NSA_HEADS, 1, TN, LANES), lambda bi, i: (0, i, 0, 0)),
                  const(tzd), const(ovt), const(qaux), const(kaux)],
        out_specs=pl.BlockSpec((1, TN, width), lambda bi, i: (bi, i, 0)),
        out_shape=jax.ShapeDtypeStruct((b, t, width), BF16),
        scratch_shapes=[pltpu.VMEM((NSA_KV_HEADS, t, LANES), BF16)] * 4
                       + [pltpu.VMEM((2, NSA_HEADS, TN, LANES), F32)] * 2,
        compiler_params=pltpu.CompilerParams(dimension_semantics=("parallel", "arbitrary"),
                                             vmem_limit_bytes=VMEM_LIMIT),
        name="nsa",
    )(p3, p3, p3, p3, p3, p3, gexp, kvc, bias_c, tzd, ovt, qaux, kaux)


def _sb_kernel(q_ref, *refs):
    npair = SB_HEADS // 2
    k_refs, v_refs = refs[:npair], refs[npair:2 * npair]
    u_ref, o_ref, acc_ref, carry_ref = refs[2 * npair:]
    i = pl.program_id(1)
    nb = q_ref.shape[0]
    chains = [(bb, c) for bb in range(nb) for c in range(npair)]
    cols = [slice(c * LANES, (c + 1) * LANES) for c in range(SB_HEADS // 2)]
    lane = lax.broadcasted_iota(jnp.int32, (TS, LANES), 1)
    lo = lane < HEAD_DIM
    lane2 = lax.broadcasted_iota(jnp.int32, (2 * TS, TS), 1)
    row2 = lax.broadcasted_iota(jnp.int32, (2 * TS, TS), 0) & (TS - 1)
    causal = lane2 < row2

    qs = []
    for bb, c in chains:
        q = q_ref[bb, :, cols[c]].astype(F32)
        qs.append(jnp.concatenate([jnp.where(lo, q, 0.0), jnp.where(lo, 0.0, q)], axis=0).astype(BF16))

    acc_ref[...] = jnp.zeros(acc_ref.shape, F32)
    carry_ref[...] = jnp.zeros(carry_ref.shape, F32)

    def tile(j, diag):
        off = pl.multiple_of(j * TS, TS)
        zs = [_dot_nt(qs[n], k_refs[c][bb, pl.ds(off, TS), :]) for n, (bb, c) in enumerate(chains)]
        log_betas, log_keeps, k_bfs = [], [], []
        for z in zs:
            z_neg = jnp.minimum(z, 0.0)
            log_beta = z_neg - jnp.log(1.0 + jnp.exp2((z_neg + z_neg) - z)) * LOG2E
            log_keep = log_beta - z
            if diag:
                log_keep = jnp.where(causal, log_keep, 0.0)
            log_betas.append(log_beta)
            log_keeps.append(log_keep)
            k_bfs.append(log_keep.astype(BF16))
        laters = [_dot(k_bf, u_ref[...]) for k_bf in k_bfs]
        a_s = []
        for n, (bb, c) in enumerate(chains):
            carry = carry_ref[bb, c]
            a = jnp.exp2(log_betas[n] + laters[n] + jnp.concatenate([carry] * (TS // LANES), axis=1))
            if diag:
                a = jnp.where(causal, a, 0.0)
            a_s.append(a.astype(BF16))
            carry_ref[bb, c] = carry + jnp.sum(log_keeps[n], axis=1, keepdims=True)
        for n, (bb, c) in enumerate(chains):
            acc_ref[bb, c] += _dot(a_s[n], v_refs[c][bb, pl.ds(off, TS), :])

    tile(i, True)

    def body(n, carry):
        tile(i - 1 - n, False)
        return carry

    lax.fori_loop(0, i, body, 0)
    for bb, c in chains:
        o = acc_ref[bb, c]
        o_ref[bb, :, cols[c]] = jnp.where(lo, o[:TS], o[TS:]).astype(BF16)


def _sb(p3, u):
    b, t, _ = p3.shape
    width = SB_HEADS * HEAD_DIM
    npair = SB_HEADS // 2
    nb = 2 if b % 2 == 0 else 1
    return pl.pallas_call(
        _sb_kernel,
        grid=(b // nb, t // TS),
        in_specs=[pl.BlockSpec((nb, TS, width), lambda bi, i: (bi, i, _C_SBQ * LANES // width))]
                 + [pl.BlockSpec((nb, t, LANES), lambda bi, i, col=col0 + c: (bi, 0, col))
                    for col0 in (_C_SBK, _C_SBV) for c in range(npair)]
                 + [pl.BlockSpec((TS, TS), lambda bi, i: (0, 0))],
        out_specs=pl.BlockSpec((nb, TS, width), lambda bi, i: (bi, i, 0)),
        out_shape=jax.ShapeDtypeStruct((b, t, width), BF16),
        scratch_shapes=[pltpu.VMEM((nb, SB_HEADS // 2, 2 * TS, LANES), F32),
                        pltpu.VMEM((nb, SB_HEADS // 2, 2 * TS, LANES), F32)],
        compiler_params=pltpu.CompilerParams(dimension_semantics=("parallel", "arbitrary"),
                                             vmem_limit_bytes=VMEM_LIMIT),
        name="sb",
    )(p3, *([p3] * (2 * npair)), u)


def _tail_kernel(x_ref, on_ref, os_ref, ga_ref, wgm_ref, wun_ref, wus_ref, wo_ref, gm_ref,
                 w1_ref, w2_ref, gf_ref, o_ref, *, last_layer):
    x = x_ref[...]
    d = x.shape[1]
    h = _rms(x, ga_ref[...]).astype(BF16)
    gm = 1.0 / (1.0 + jnp.exp(-_dot(h, wgm_ref[...])))
    mixed = gm[:, :d] * _dot(on_ref[...], wun_ref[...]) + gm[:, d:] * _dot(os_ref[...], wus_ref[...])
    x1 = x + _dot(mixed.astype(BF16), wo_ref[...])
    h2 = _rms(x1, gm_ref[...]).astype(BF16)
    acc = jnp.zeros_like(x1)
    for f in range(w1_ref.shape[0]):
        u = jnp.maximum(_dot(h2, w1_ref[f]), 0.0)
        acc = acc + _dot((u * u).astype(BF16), w2_ref[f])
    x2 = x1 + acc
    o_ref[...] = _rms(x2, gf_ref[...]) if last_layer else x2


def _tail(x2, o_nsa, o_sb, g_attn, wgm, wun, wus, wo, g_mlp, w1, w2, g_final, last_layer):
    m, d = x2.shape
    tm = ROWS_TAIL
    tok = lambda w: pl.BlockSpec((tm, w), lambda i: (i, 0))
    const = lambda a: pl.BlockSpec(a.shape, lambda i: (0,) * a.ndim, pipeline_mode=pl.Buffered(1))
    return pl.pallas_call(
        functools.partial(_tail_kernel, last_layer=last_layer),
        grid=(m // tm,),
        in_specs=[tok(d), tok(o_nsa.shape[1]), tok(o_sb.shape[1]), const(g_attn), const(wgm),
                  const(wun), const(wus), const(wo), const(g_mlp), const(w1), const(w2), const(g_final)],
        out_specs=tok(d),
        out_shape=jax.ShapeDtypeStruct((m, d), F32),
        compiler_params=pltpu.CompilerParams(dimension_semantics=("parallel",),
                                             vmem_limit_bytes=VMEM_LIMIT),
        name="tail",
    )(x2, o_nsa, o_sb, g_attn, wgm, wun, wus, wo, g_mlp, w1, w2, g_final)


def _t5_bucket_np(dist):
    n = np.maximum(dist, 0)
    max_exact = N_BUCKETS // 2
    nf = np.maximum(n, 1).astype(np.float32)
    large = max_exact + (np.log(nf / np.float32(max_exact)) / np.float32(math.log(MAX_DISTANCE / max_exact))
                         * np.float32(N_BUCKETS - max_exact)).astype(np.int32)
    return np.where(n < max_exact, n, np.minimum(large, N_BUCKETS - 1)).astype(np.int32)


def _bias_tables(rel_bias, t):
    tbl = rel_bias * LOG2E
    far_idx = int(_t5_bucket_np(np.asarray(MAX_DISTANCE)))
    far = tbl[far_idx]

    def lookup(idx, table):
        onehot = jnp.asarray(idx.reshape(-1)[:, None] == np.arange(N_BUCKETS)[None, :], BF16)
        out = jnp.einsum("bh,nb->hn", table, onehot, precision=lax.Precision.HIGHEST)
        return out.reshape((table.shape[1],) + idx.shape)

    qi, kj = np.arange(TN)[:, None], np.arange(TN)[None, :]
    tzd = lookup(np.stack([_t5_bucket_np(r * TN + qi - kj) for r in range(2)]), tbl - far[None, :])
    dist_c = (np.arange(t).reshape(t // TN, TN, 1) - CMP_STRIDE * np.arange(LANES)[None, None, :]
              - (CMP_LEN - 1))
    bias_c = lookup(np.where(dist_c >= 0, _t5_bucket_np(dist_c), far_idx), tbl)
    far_hi = far.astype(BF16).astype(F32)
    qaux = jnp.zeros((rel_bias.shape[1], 1, LANES), F32)
    qaux = qaux.at[:, 0, BIAS_LANE].set(far_hi).at[:, 0, BIAS_LANE + 1].set(far - far_hi)
    return tzd, bias_c, qaux


def _const_tables(t):
    c_start = np.arange(LANES) * CMP_STRIDE
    s_start = np.arange(LANES) * SEL_LEN
    n_cmp = (t - CMP_LEN) // CMP_STRIDE + 1
    n_sel = t // SEL_LEN
    ov = ((c_start[:, None] < s_start[None, :] + SEL_LEN) & (c_start[:, None] + CMP_LEN > s_start[None, :])
          & (np.arange(LANES)[:, None] < n_cmp) & (np.arange(LANES)[None, :] < n_sel))
    kaux = np.zeros((2, t, LANES), np.float32)
    kaux[0, np.arange(t), SEL_LANE0 + np.arange(t) // SEL_LEN] = 1.0
    kaux[:, :, BIAS_LANE:BIAS_LANE + 2] = 1.0
    u = np.arange(TS)[:, None] > np.arange(TS)[None, :]
    gexp = np.zeros((NSA_KV_HEADS, LANES, 3 * NSA_GROUP * HEAD_DIM), np.float32)
    for h in range(NSA_HEADS):
        for br in range(3):
            c0 = (br * NSA_GROUP + h % NSA_GROUP) * HEAD_DIM
            gexp[h // NSA_GROUP, h * 3 + br, c0:c0 + HEAD_DIM] = 1.0
    return jnp.asarray(ov.T, BF16), jnp.asarray(kaux, BF16), jnp.asarray(u, BF16), jnp.asarray(gexp, BF16)


def _arrange_w_in(w):
    d = w.shape[0]
    nq, nkv, ng = NSA_HEADS * HEAD_DIM, 6 * NSA_KV_HEADS * HEAD_DIM, 3 * NSA_HEADS
    nsb = SB_HEADS * HEAD_DIM
    scale = HEAD_DIM ** -0.5 * LOG2E
    q = w[:, :nq] * scale
    kv = w[:, nq:nq + nkv]
    gl = w[:, nq + nkv:nq + nkv + ng]
    sb = w[:, nq + nkv + ng:nq + nkv + ng + 3 * nsb]
    gm = w[:, nq + nkv + ng + 3 * nsb:]
    kvw = NSA_KV_HEADS * HEAD_DIM
    assert kvw == LANES and ng <= LANES
    gates = jnp.pad(gl, ((0, 0), (0, LANES - ng)))
    cols = [sb[:, :nsb] * scale, sb[:, nsb:], q, kv[:, 2 * kvw:], gates, kv[:, :2 * kvw]]
    return jnp.concatenate(cols, axis=1).astype(BF16), gm.astype(BF16)


def kernel(x, norm_attn, w_in, rel_bias, cmp_k_pe, cmp_k_w1, cmp_k_b1, cmp_k_w2, cmp_v_pe, cmp_v_w1, cmp_v_b1, cmp_v_w2, w_up_nsa, w_up_sb, w_out, norm_mlp, w_ff1, w_ff2, norm_final):
    b, t, d = x.shape
    depth = w_in.shape[0]
    assert t % TN == 0 and t % TS == 0 and WINDOW == 2 * TN and MAX_DISTANCE <= TN
    assert t // SEL_LEN <= N_BUCKETS and (t - CMP_LEN) // CMP_STRIDE + 1 < LANES
    assert (b * t) % ROWS_INPROJ == 0 and (b * t) % ROWS_TAIL == 0
    tzd, bias_c, qaux = _bias_tables(rel_bias, t)
    ovt, kaux, u, gexp = _const_tables(t)
    half = CMP_LEN // 2 * HEAD_DIM
    xf = x.reshape(b * t, d)
    for layer in range(depth):
        w_arr, w_gm = _arrange_w_in(w_in[layer])
        assert w_arr.shape[1] == _N_CHUNKS * LANES
        proj, kvc = _inproj(xf, norm_attn[layer][None, :], w_arr)
        proj = proj.reshape(b, t, -1)

        kvc = kvc.reshape(b, t, 4, HEAD_DIM)
        kvc = jnp.transpose(kvc, (0, 2, 1, 3)).reshape(b, 4, t // CMP_STRIDE, CMP_STRIDE * HEAD_DIM)
        pe = jnp.stack([cmp_k_pe[layer], cmp_v_pe[layer]]).reshape(2, 1, CMP_LEN * HEAD_DIM)
        w1 = jnp.stack([cmp_k_w1[layer], cmp_v_w1[layer]]).astype(BF16)
        b1 = jnp.stack([cmp_k_b1[layer], cmp_v_b1[layer]])[:, None, :]
        w2 = jnp.stack([cmp_k_w2[layer], cmp_v_w2[layer]])
        w2d = jnp.concatenate([w2, w2], axis=2).astype(BF16)
        kvc = _compress(kvc, pe[:, :, :half], pe[:, :, half:], w1[:, :half], w1[:, half:], b1, w2d)

        o_nsa = _nsa(proj, kvc, bias_c, tzd, ovt, qaux, kaux, gexp)
        dff = w_ff1.shape[2]
        w1c = jnp.transpose(w_ff1[layer].astype(BF16).reshape(d, dff // FF_CHUNK, FF_CHUNK), (1, 0, 2))
        w2c = w_ff2[layer].astype(BF16).reshape(dff // FF_CHUNK, FF_CHUNK, d)
        o_sb = _sb(proj, u)
        xf = _tail(xf, o_nsa.reshape(b * t, -1), o_sb.reshape(b * t, -1), norm_attn[layer][None, :], w_gm,
                   w_up_nsa[layer].astype(BF16), w_up_sb[layer].astype(BF16), w_out[layer].astype(BF16),
                   norm_mlp[layer][None, :], w1c, w2c, norm_final[None, :], layer == depth - 1)
    return xf.reshape(b, t, d)
```

```python
import functools
import math

import numpy as np
import jax
import jax.numpy as jnp
from jax import lax
from jax.experimental import pallas as pl
from jax.experimental.pallas import tpu as pltpu

HEAD_DIM = 64
NSA_HEADS = 8
NSA_KV_HEADS = 2
NSA_GROUP = NSA_HEADS // NSA_KV_HEADS
SB_HEADS = 8
CMP_LEN = 32
CMP_STRIDE = 16
CMP_HIDDEN = 256
SEL_LEN = 64
SEL_TOPK = 4
WINDOW = 512
N_BUCKETS = 32
MAX_DISTANCE = 128
EPS = 1e-6
NEG = -1e30
FORCE_BONUS = 1e4

LANES = 128
TN = 256
TS = 256
ROWS_INPROJ = 1024
ROWS_TAIL = 512
FF_CHUNK = 512
MXU_COLS = 256
SETUP_ROWS = 512
VMEM_LIMIT = 56 * 1024 * 1024
LOG2E = math.log2(math.e)
MASK_BIG = 2.0 ** 60
SEL_LANE0 = HEAD_DIM
BIAS_LANE = HEAD_DIM + N_BUCKETS

F32 = jnp.float32
BF16 = jnp.bfloat16

_C_SBQ = 0
_C_SBK = 4
_C_SBV = 8
_C_QN = 12
_C_KS = 16
_C_VS = 17
_C_KW = 18
_C_VW = 19
_C_GATE = 20
_C_KVC = 21
_N_CHUNKS = 23


def _dot(a, b):
    return jnp.dot(a, b, preferred_element_type=F32)


def _dot_nt(a, b):
    return lax.dot_general(a, b, (((1,), (1,)), ((), ())), preferred_element_type=F32)


def _rms(x, g):
    return x * lax.rsqrt(jnp.mean(x * x, axis=-1, keepdims=True) + EPS) * g


def _inproj_kernel(x_ref, g_ref, w_ref, o_ref, kvc_ref):
    h = _rms(x_ref[...], g_ref[...]).astype(BF16)
    n = o_ref.shape[1]
    for c in range(0, n, MXU_COLS):
        cs = slice(c, min(c + MXU_COLS, n))
        o_ref[:, cs] = _dot(h, w_ref[:, cs]).astype(BF16)
    kvc_ref[...] = _dot(h, w_ref[:, n:]).astype(BF16)


def _inproj(x2, g, w):
    m, d = x2.shape
    n = _C_KVC * LANES
    nc = w.shape[1] - n
    tm = ROWS_INPROJ
    return pl.pallas_call(
        _inproj_kernel,
        grid=(m // tm,),
        in_specs=[pl.BlockSpec((tm, d), lambda i: (i, 0)),
                  pl.BlockSpec((1, d), lambda i: (0, 0)),
                  pl.BlockSpec(w.shape, lambda i: (0, 0), pipeline_mode=pl.Buffered(1))],
        out_specs=[pl.BlockSpec((tm, n), lambda i: (i, 0)), pl.BlockSpec((tm, nc), lambda i: (i, 0))],
        out_shape=[jax.ShapeDtypeStruct((m, n), BF16), jax.ShapeDtypeStruct((m, nc), BF16)],
        compiler_params=pltpu.CompilerParams(dimension_semantics=("parallel",),
                                             vmem_limit_bytes=VMEM_LIMIT),
        name="inproj",
    )(x2, g, w)


def _gelu_tanh(x):
    return 0.5 * x * (1.0 + jnp.tanh(math.sqrt(2.0 / math.pi) * (x + 0.044715 * (x * x * x))))


def _compress_kernel(x_ref, pea_ref, peb_ref, w1a_ref, w1b_ref, b1_ref, w2_ref, o_ref):
    for s in range(4):
        r = s // 2
        x = x_ref[0, s].astype(F32)
        a = _dot((x + pea_ref[r]).astype(BF16), w1a_ref[r])
        b = _dot((x + peb_ref[r]).astype(BF16), w1b_ref[r])
        hid = a + pltpu.roll(b, shift=b.shape[0] - 1, axis=0) + b1_ref[r]
        o_ref[0, s] = _dot(_gelu_tanh(hid).astype(BF16), w2_ref[r]).astype(BF16)


def _compress(xc, pea, peb, w1a, w1b, b1, w2d):
    b = xc.shape[0]
    nchunk, width = xc.shape[2], xc.shape[3]
    full = lambda shape: pl.BlockSpec(shape, lambda i: (0,) * len(shape))
    return pl.pallas_call(
        _compress_kernel,
        grid=(b,),
        in_specs=[pl.BlockSpec((1, 4, nchunk, width), lambda i: (i, 0, 0, 0)),
                  full(pea.shape), full(peb.shape), full(w1a.shape), full(w1b.shape),
                  full(b1.shape), full(w2d.shape)],
        out_specs=pl.BlockSpec((1, 4, nchunk, LANES), lambda i: (i, 0, 0, 0)),
        out_shape=jax.ShapeDtypeStruct((b, 4, nchunk, LANES), BF16),
        compiler_params=pltpu.CompilerParams(dimension_semantics=("parallel",),
                                             vmem_limit_bytes=VMEM_LIMIT),
        name="compress",
    )(xc, pea, peb, w1a, w1b, b1, w2d)


def _nsa_kernel(q_ref, ks_ref, vs_ref, kw_ref, vw_ref, g_ref, gexp_ref, kvc_ref,
                bc_ref, tzd_ref, ovt_ref, qaux_ref, kaux_ref, o_ref,
                ksx_ref, vsx_ref, kwx_ref, vwx_ref, m_ref, acc_ref):
    i = pl.program_id(1)
    ng, hg, nh = NSA_KV_HEADS, NSA_GROUP, NSA_HEADS
    t = ks_ref.shape[1]
    lane = lax.broadcasted_iota(jnp.int32, (TN, LANES), 1)
    row = lax.broadcasted_iota(jnp.int32, (TN, LANES), 0)
    lo = lane < HEAD_DIM
    t_abs = i * TN + row
    krow = lax.broadcasted_iota(jnp.int32, (TN, TN), 0)
    kcol = lax.broadcasted_iota(jnp.int32, (TN, TN), 1)

    def hrows(h):
        return slice(h * TN, (h + 1) * TN)

    @pl.when(i == 0)
    def _():
        ch = SETUP_ROWS
        lo_c = lax.broadcasted_iota(jnp.int32, (ch, LANES), 1) < HEAD_DIM
        for r0 in range(0, t, ch):
            rs = slice(r0, r0 + ch)
            for g in range(ng):
                def own(ref):
                    x = ref[0, rs].astype(F32)
                    return x if g == 0 else pltpu.roll(x, HEAD_DIM, 1)

                ksx_ref[g, rs] = jnp.where(lo_c, own(ks_ref), kaux_ref[0, rs].astype(F32)).astype(BF16)
                kwx_ref[g, rs] = jnp.where(lo_c, own(kw_ref), kaux_ref[1, rs].astype(F32)).astype(BF16)
                vsx_ref[g, rs] = jnp.where(lo_c, own(vs_ref), 1.0).astype(BF16)
                vwx_ref[g, rs] = jnp.where(lo_c, own(vw_ref), 1.0).astype(BF16)

    base = []
    for c in range(nh // 2):
        qc = q_ref[0, :, c * LANES:(c + 1) * LANES].astype(F32)
        for h, qh in ((2 * c, qc), (2 * c + 1, pltpu.roll(qc, HEAD_DIM, 1))):
            base.append(jnp.where(lo, qh, qaux_ref[h]))

    lo_c = lax.broadcasted_iota(jnp.int32, (LANES, LANES), 1) < HEAD_DIM
    valid_c = (t_abs >= CMP_STRIDE * lane + (CMP_LEN - 1)) & (lane < LANES - 1)
    s_cmp = [_dot_nt(jnp.concatenate(base[g * hg:(g + 1) * hg], axis=0).astype(BF16),
                     jnp.where(lo_c, kvc_ref[0, g].astype(F32), 0.0).astype(BF16)) for g in range(ng)]
    ps = []
    for h in range(nh):
        sh = jnp.where(valid_c, s_cmp[h // hg][hrows(h % hg)] + bc_ref[h, 0], NEG)
        e = jnp.exp2(sh - jnp.max(sh, axis=1, keepdims=True))
        ps.append(jnp.where(valid_c, e, 0.0) / jnp.sum(e, axis=1, keepdims=True))
    o_cmp = [_dot(jnp.concatenate(ps[g * hg:(g + 1) * hg], axis=0).astype(BF16), kvc_ref[0, ng + g])
             for g in range(ng)]

    n_sel = t // SEL_LEN
    nidx = lax.broadcasted_iota(jnp.int32, (n_sel, TN), 0)
    tq = i * TN + lax.broadcasted_iota(jnp.int32, (n_sel, TN), 1)
    forced = (nidx == tq // SEL_LEN) | (nidx == 0)
    sel_ok = nidx * SEL_LEN <= tq
    nidx_f = nidx.astype(F32)
    scores, selms = [], []
    for g in range(ng):
        pg = ps[g * hg:(g + 1) * hg]
        psum = (pg[0] + pg[1]) + (pg[2] + pg[3])
        p_hi = psum.astype(BF16)
        p_lo = (psum - p_hi.astype(F32)).astype(BF16)
        imp = (_dot_nt(ovt_ref[...], p_hi) + _dot_nt(ovt_ref[...], p_lo))[:n_sel]
        scores.append(jnp.where(forced, -jnp.inf, jnp.where(sel_ok, imp, -FORCE_BONUS)))
        selms.append(jnp.where(forced, 1.0, 0.0))

    def process(jobs, first):
        ss = [post(_dot_nt(lhs[br][h], kt)) for (br, h, kt, _, post) in jobs]
        ps, alphas = [], []
        for (br, h, _, _, _), sh in zip(jobs, ss):
            mx = jnp.max(sh, axis=1, keepdims=True)
            if first:
                m_new = jnp.broadcast_to(mx, (TN, LANES))
                alphas.append(None)
            else:
                m_old = m_ref[br, h]
                m_new = jnp.maximum(m_old, mx)
                alphas.append(jnp.exp2(m_old - m_new))
            m_ref[br, h] = m_new
            ps.append(jnp.exp2(sh - jnp.concatenate([m_new] * (sh.shape[1] // LANES), axis=1)).astype(BF16))
        for (br, h, _, vt, _), p, alpha in zip(jobs, ps, alphas):
            pv = _dot(p, vt)
            acc_ref[br, h] = pv if first else alpha * acc_ref[br, h] + pv

    def pair_out(br, he, ho):
        acc_e, acc_o = acc_ref[br, he], acc_ref[br, ho]
        num = jnp.where(lo, acc_e, pltpu.roll(acc_o, HEAD_DIM, 1))
        den = jnp.where(lo, pltpu.roll(acc_e, HEAD_DIM, 1), acc_o)
        return num / den

    kxs, vxs = (ksx_ref, kwx_ref), (vsx_ref, vwx_ref)
    causal = kcol <= krow
    inside = kcol > krow

    def tile_jobs(branches, off, post, width=TN):
        return [(br, h, kxs[br][h // hg, pl.ds(off, width), :], vxs[br][h // hg, pl.ds(off, width), :],
                 functools.partial(post, h)) for br in branches for h in range(nh)]

    for k in range(SEL_TOPK - 1):
        for g in range(ng):
            mx = jnp.max(scores[g], axis=0, keepdims=True)
            idx = jnp.min(jnp.where(scores[g] == mx, nidx_f, float(LANES)), axis=0, keepdims=True)
            hit = nidx_f == idx
            if k == SEL_TOPK - 2:
                hit = hit & (tq < SEL_LEN)
            selms[g] = jnp.where(hit, 1.0, selms[g])
            scores[g] = jnp.where(hit, -jnp.inf, scores[g])
    pad_lo = jnp.zeros((SEL_LANE0, TN), F32)
    pad_hi = jnp.zeros((LANES - SEL_LANE0 - n_sel, TN), F32)
    negsel = [jnp.transpose(jnp.concatenate([pad_lo, (sm - 1.0) * MASK_BIG, pad_hi], axis=0)) for sm in selms]
    lhs = [[(base[h] + negsel[h // hg]).astype(BF16) for h in range(nh)],
           [bh.astype(BF16) for bh in base]]

    back = WINDOW // TN

    def near_post(n_tiles, br, h, s):
        chunks = []
        for c in range(n_tiles):
            r, sc = n_tiles - 1 - c, s[:, c * TN:(c + 1) * TN]
            if r == 0:
                sc = jnp.where(causal, sc + tzd_ref[h, 0], NEG)
            elif r == 1:
                sc = sc + tzd_ref[h, 1]
            elif br == 1:
                sc = jnp.where(inside, sc, NEG)
            chunks.append(sc)
        return jnp.concatenate(chunks, axis=1)

    def near_jobs(n_tiles):
        off = pl.multiple_of(jnp.maximum(i - (n_tiles - 1), 0) * TN, TN)
        return [job for br in (1, 0)
                for job in tile_jobs((br,), off, functools.partial(near_post, n_tiles, br), n_tiles * TN)]

    for n_tiles in range(1, back + 2):
        @pl.when((i == n_tiles - 1) if n_tiles <= back else (i >= back))
        def _():
            process(near_jobs(n_tiles), True)

    n_far = jnp.maximum(i - back, 0)

    def sel_body(jj, carry):
        process(tile_jobs((0,), pl.multiple_of(2 * jj * TN, TN), lambda h, s: s, 2 * TN), False)
        return carry

    lax.fori_loop(0, n_far // 2, sel_body, 0)

    @pl.when(n_far % 2 == 1)
    def _():
        process(tile_jobs((0,), pl.multiple_of(jnp.maximum(n_far - 1, 0) * TN, TN), lambda h, s: s), False)

    width = hg * HEAD_DIM
    for g in range(ng):
        gates = 0.5 * jnp.tanh(0.5 * _dot(g_ref[0], gexp_ref[g])) + 0.5
        for c in range(hg // 2):
            g_cmp, g_sel, g_win = (gates[:, br * width + c * LANES:br * width + (c + 1) * LANES] for br in range(3))
            he, ho = g * hg + 2 * c, g * hg + 2 * c + 1
            out = (g_cmp * jnp.where(lo, o_cmp[g][hrows(2 * c)], o_cmp[g][hrows(2 * c + 1)])
                   + g_sel * pair_out(0, he, ho) + g_win * pair_out(1, he, ho))
            o_ref[0, :, (he // 2) * LANES:(he // 2 + 1) * LANES] = out.astype(BF16)


def _nsa(p3, kvc, bias_c, tzd, ovt, qaux, kaux, gexp):
    b, t, _ = p3.shape
    nq = t // TN
    width = NSA_HEADS * HEAD_DIM
    seq = lambda col: pl.BlockSpec((1, t, LANES), lambda bi, i, col=col: (bi, 0, col))
    const = lambda a: pl.BlockSpec(a.shape, lambda bi, i: (0,) * a.ndim, pipeline_mode=pl.Buffered(1))
    return pl.pallas_call(
        _nsa_kernel,
        grid=(b, nq),
        in_specs=[pl.BlockSpec((1, TN, width), lambda bi, i: (bi, i, _C_QN * LANES // width)),
                  seq(_C_KS), seq(_C_VS), seq(_C_KW), seq(_C_VW),
                  pl.BlockSpec((1, TN, LANES), lambda bi, i: (bi, i, _C_GATE)),
                  const(gexp),
                  pl.BlockSpec((1,) + kvc.shape[1:], lambda bi, i: (bi, 0, 0, 0)),
                  pl.BlockSpec((NSA_HEADS, 1, TN, LANES), lambda bi, i: (0, i, 0, 0)),
                  const(tzd), const(ovt), const(qaux), const(kaux)],
        out_specs=pl.BlockSpec((1, TN, width), lambda bi, i: (bi, i, 0)),
        out_shape=jax.ShapeDtypeStruct((b, t, width), BF16),
        scratch_shapes=[pltpu.VMEM((NSA_KV_HEADS, t, LANES), BF16)] * 4
                       + [pltpu.VMEM((2, NSA_HEADS, TN, LANES), F32)] * 2,
        compiler_params=pltpu.CompilerParams(dimension_semantics=("parallel", "arbitrary"),
                                             vmem_limit_bytes=VMEM_LIMIT),
        name="nsa",
    )(p3, p3, p3, p3, p3, p3, gexp, kvc, bias_c, tzd, ovt, qaux, kaux)


def _sb_kernel(q_ref, k_ref, v_ref, u_ref, o_ref, acc_ref, carry_ref):
    i = pl.program_id(1)
    nb = q_ref.shape[0]
    chains = [(bb, c) for bb in range(nb) for c in range(SB_HEADS // 2)]
    cols = [slice(c * LANES, (c + 1) * LANES) for c in range(SB_HEADS // 2)]
    lane = lax.broadcasted_iota(jnp.int32, (TS, LANES), 1)
    lo = lane < HEAD_DIM

    qs = []
    for bb, c in chains:
        q = q_ref[bb, :, cols[c]].astype(F32)
        qs.append(jnp.concatenate([jnp.where(lo, q, 0.0), jnp.where(lo, 0.0, q)], axis=0).astype(BF16))

    def tile(j):
        off = pl.multiple_of(j * TS, TS)
        zs = [_dot_nt(qs[n], k_ref[bb, pl.ds(off, TS), cols[c]]) for n, (bb, c) in enumerate(chains)]
        log_betas, log_keeps, k_bfs = [], [], []
        for z in zs:
            z_neg = jnp.minimum(z, 0.0)
            log_beta = z_neg - jnp.log(1.0 + jnp.exp2((z_neg + z_neg) - z)) * LOG2E
            log_keep = log_beta - z
            log_betas.append(log_beta)
            log_keeps.append(log_keep)
            k_bfs.append(log_keep.astype(BF16))
        laters = [_dot(k_bf, u_ref[...]) for k_bf in k_bfs]
        a_s = []
        for n, (bb, c) in enumerate(chains):
            carry = carry_ref[bb, c]
            a = jnp.exp2(log_betas[n] + laters[n] + jnp.concatenate([carry] * (TS // LANES), axis=1))
            a_s.append(a.astype(BF16))
            carry_ref[bb, c] = carry + jnp.sum(log_keeps[n], axis=1, keepdims=True)
        for n, (bb, c) in enumerate(chains):
            acc_ref[bb, c] += _dot(a_s[n], v_ref[bb, pl.ds(off, TS), cols[c]])

    def diag_tile():
        off = pl.multiple_of(i * TS, TS)
        half = TS // 2
        mask_full = (lax.broadcasted_iota(jnp.int32, (TS, TS), 1)
                     < (lax.broadcasted_iota(jnp.int32, (TS, TS), 0) & (half - 1)) + half)
        mask_half = (lax.broadcasted_iota(jnp.int32, (TS, half), 1)
                     < (lax.broadcasted_iota(jnp.int32, (TS, half), 0) & (half - 1)))
        pieces = []
        for n in range(len(chains)):
            pieces.append((n, (slice(half, TS), slice(TS + half, 2 * TS)), TS, mask_full))
            pieces.append((n, (slice(0, half), slice(TS, TS + half)), half, mask_half))
        zs = []
        for n, rows, nk, _ in pieces:
            bb, c = chains[n]
            q = jnp.concatenate([qs[n][rows[0]], qs[n][rows[1]]], axis=0)
            zs.append(_dot_nt(q, k_ref[bb, pl.ds(off, nk), cols[c]]))
        log_betas, log_keeps, k_bfs = [], [], []
        for z, (_, _, _, mask) in zip(zs, pieces):
            z_neg = jnp.minimum(z, 0.0)
            log_beta = z_neg - jnp.log(1.0 + jnp.exp2((z_neg + z_neg) - z)) * LOG2E
            log_keep = jnp.where(mask, log_beta - z, 0.0)
            log_betas.append(log_beta)
            log_keeps.append(log_keep)
            k_bfs.append(log_keep.astype(BF16))
        laters = [_dot(k_bf, u_ref[:nk, :nk]) for k_bf, (_, _, nk, _) in zip(k_bfs, pieces)]
        a_s = []
        for w, (n, rows, nk, mask) in enumerate(pieces):
            bb, c = chains[n]
            a_s.append(jnp.where(mask, jnp.exp2(log_betas[w] + laters[w]), 0.0).astype(BF16))
            for rs, part in zip(rows, (log_keeps[w][:half], log_keeps[w][half:])):
                carry_ref[bb, c, rs] = jnp.broadcast_to(jnp.sum(part, axis=1, keepdims=True), (half, LANES))
        for w, (n, rows, nk, _) in enumerate(pieces):
            bb, c = chains[n]
            pv = _dot(a_s[w], v_ref[bb, pl.ds(off, nk), cols[c]])
            acc_ref[bb, c, rows[0]] = pv[:half]
            acc_ref[bb, c, rows[1]] = pv[half:]

    diag_tile()

    def body(n, carry):
        tile(i - 1 - n)
        return carry

    lax.fori_loop(0, i, body, 0)
    for bb, c in chains:
        o = acc_ref[bb, c]
        o_ref[bb, :, cols[c]] = jnp.where(lo, o[:TS], o[TS:]).astype(BF16)


def _sb(p3, u):
    b, t, _ = p3.shape
    width = SB_HEADS * HEAD_DIM
    nb = 2 if b % 2 == 0 else 1
    return pl.pallas_call(
        _sb_kernel,
        grid=(b // nb, t // TS),
        in_specs=[pl.BlockSpec((nb, TS, width), lambda bi, i: (bi, i, _C_SBQ * LANES // width)),
                  pl.BlockSpec((nb, t, width), lambda bi, i: (bi, 0, _C_SBK * LANES // width)),
                  pl.BlockSpec((nb, t, width), lambda bi, i: (bi, 0, _C_SBV * LANES // width)),
                  pl.BlockSpec((TS, TS), lambda bi, i: (0, 0))],
        out_specs=pl.BlockSpec((nb, TS, width), lambda bi, i: (bi, i, 0)),
        out_shape=jax.ShapeDtypeStruct((b, t, width), BF16),
        scratch_shapes=[pltpu.VMEM((nb, SB_HEADS // 2, 2 * TS, LANES), F32),
                        pltpu.VMEM((nb, SB_HEADS // 2, 2 * TS, LANES), F32)],
        compiler_params=pltpu.CompilerParams(dimension_semantics=("parallel", "arbitrary"),
                                             vmem_limit_bytes=VMEM_LIMIT),
        name="sb",
    )(p3, p3, p3, u)


def _tail_kernel(x_ref, on_ref, os_ref, ga_ref, wgm_ref, wun_ref, wus_ref, wo_ref, gm_ref,
                 w1_ref, w2_ref, gf_ref, o_ref, *, last_layer):
    x = x_ref[...]
    d = x.shape[1]
    h = _rms(x, ga_ref[...]).astype(BF16)
    gm = 1.0 / (1.0 + jnp.exp(-_dot(h, wgm_ref[...])))
    mixed = gm[:, :d] * _dot(on_ref[...], wun_ref[...]) + gm[:, d:] * _dot(os_ref[...], wus_ref[...])
    x1 = x + _dot(mixed.astype(BF16), wo_ref[...])
    h2 = _rms(x1, gm_ref[...]).astype(BF16)
    dff = w1_ref.shape[1]
    acc = jnp.zeros_like(x1)
    for f in range(0, dff, FF_CHUNK):
        u = jnp.maximum(_dot(h2, w1_ref[:, f:f + FF_CHUNK]), 0.0)
        acc = acc + _dot((u * u).astype(BF16), w2_ref[f:f + FF_CHUNK, :])
    x2 = x1 + acc
    o_ref[...] = _rms(x2, gf_ref[...]) if last_layer else x2


def _tail(x2, o_nsa, o_sb, g_attn, wgm, wun, wus, wo, g_mlp, w1, w2, g_final, last_layer):
    m, d = x2.shape
    tm = ROWS_TAIL
    tok = lambda w: pl.BlockSpec((tm, w), lambda i: (i, 0))
    const = lambda a: pl.BlockSpec(a.shape, lambda i: (0, 0), pipeline_mode=pl.Buffered(1))
    return pl.pallas_call(
        functools.partial(_tail_kernel, last_layer=last_layer),
        grid=(m // tm,),
        in_specs=[tok(d), tok(o_nsa.shape[1]), tok(o_sb.shape[1]), const(g_attn), const(wgm),
                  const(wun), const(wus), const(wo), const(g_mlp), const(w1), const(w2), const(g_final)],
        out_specs=tok(d),
        out_shape=jax.ShapeDtypeStruct((m, d), F32),
        compiler_params=pltpu.CompilerParams(dimension_semantics=("parallel",),
                                             vmem_limit_bytes=VMEM_LIMIT),
        name="tail",
    )(x2, o_nsa, o_sb, g_attn, wgm, wun, wus, wo, g_mlp, w1, w2, g_final)


def _t5_bucket_np(dist):
    n = np.maximum(dist, 0)
    max_exact = N_BUCKETS // 2
    nf = np.maximum(n, 1).astype(np.float32)
    large = max_exact + (np.log(nf / np.float32(max_exact)) / np.float32(math.log(MAX_DISTANCE / max_exact))
                         * np.float32(N_BUCKETS - max_exact)).astype(np.int32)
    return np.where(n < max_exact, n, np.minimum(large, N_BUCKETS - 1)).astype(np.int32)


def _bias_tables(rel_bias, t):
    tbl = rel_bias * LOG2E
    far_idx = int(_t5_bucket_np(np.asarray(MAX_DISTANCE)))
    far = tbl[far_idx]

    def lookup(idx, table):
        onehot = jnp.asarray(idx.reshape(-1)[:, None] == np.arange(N_BUCKETS)[None, :], BF16)
        out = jnp.einsum("bh,nb->hn", table, onehot, precision=lax.Precision.HIGHEST)
        return out.reshape((table.shape[1],) + idx.shape)

    qi, kj = np.arange(TN)[:, None], np.arange(TN)[None, :]
    tzd = lookup(np.stack([_t5_bucket_np(r * TN + qi - kj) for r in range(2)]), tbl - far[None, :])
    dist_c = (np.arange(t).reshape(t // TN, TN, 1) - CMP_STRIDE * np.arange(LANES)[None, None, :]
              - (CMP_LEN - 1))
    bias_c = lookup(np.where(dist_c >= 0, _t5_bucket_np(dist_c), far_idx), tbl)
    far_hi = far.astype(BF16).astype(F32)
    qaux = jnp.zeros((rel_bias.shape[1], 1, LANES), F32)
    qaux = qaux.at[:, 0, BIAS_LANE].set(far_hi).at[:, 0, BIAS_LANE + 1].set(far - far_hi)
    return tzd, bias_c, qaux


def _const_tables(t):
    c_start = np.arange(LANES) * CMP_STRIDE
    s_start = np.arange(LANES) * SEL_LEN
    n_cmp = (t - CMP_LEN) // CMP_STRIDE + 1
    n_sel = t // SEL_LEN
    ov = ((c_start[:, None] < s_start[None, :] + SEL_LEN) & (c_start[:, None] + CMP_LEN > s_start[None, :])
          & (np.arange(LANES)[:, None] < n_cmp) & (np.arange(LANES)[None, :] < n_sel))
    kaux = np.zeros((2, t, LANES), np.float32)
    kaux[0, np.arange(t), SEL_LANE0 + np.arange(t) // SEL_LEN] = 1.0
    kaux[:, :, BIAS_LANE:BIAS_LANE + 2] = 1.0
    u = np.arange(TS)[:, None] > np.arange(TS)[None, :]
    gexp = np.zeros((NSA_KV_HEADS, LANES, 3 * NSA_GROUP * HEAD_DIM), np.float32)
    for h in range(NSA_HEADS):
        for br in range(3):
            c0 = (br * NSA_GROUP + h % NSA_GROUP) * HEAD_DIM
            gexp[h // NSA_GROUP, h * 3 + br, c0:c0 + HEAD_DIM] = 1.0
    return jnp.asarray(ov.T, BF16), jnp.asarray(kaux, BF16), jnp.asarray(u, BF16), jnp.asarray(gexp, BF16)


def _arrange_w_in(w):
    d = w.shape[0]
    nq, nkv, ng = NSA_HEADS * HEAD_DIM, 6 * NSA_KV_HEADS * HEAD_DIM, 3 * NSA_HEADS
    nsb = SB_HEADS * HEAD_DIM
    scale = HEAD_DIM ** -0.5 * LOG2E
    q = w[:, :nq] * scale
    kv = w[:, nq:nq + nkv]
    gl = w[:, nq + nkv:nq + nkv + ng]
    sb = w[:, nq + nkv + ng:nq + nkv + ng + 3 * nsb]
    gm = w[:, nq + nkv + ng + 3 * nsb:]
    kvw = NSA_KV_HEADS * HEAD_DIM
    assert kvw == LANES and ng <= LANES
    gates = jnp.pad(gl, ((0, 0), (0, LANES - ng)))
    cols = [sb[:, :nsb] * scale, sb[:, nsb:], q, kv[:, 2 * kvw:], gates, kv[:, :2 * kvw]]
    return jnp.concatenate(cols, axis=1).astype(BF16), gm.astype(BF16)


def kernel(x, norm_attn, w_in, rel_bias, cmp_k_pe, cmp_k_w1, cmp_k_b1, cmp_k_w2, cmp_v_pe, cmp_v_w1, cmp_v_b1, cmp_v_w2, w_up_nsa, w_up_sb, w_out, norm_mlp, w_ff1, w_ff2, norm_final):
    b, t, d = x.shape
    depth = w_in.shape[0]
    assert t % TN == 0 and t % TS == 0 and WINDOW == 2 * TN and MAX_DISTANCE <= TN
    assert t // SEL_LEN <= N_BUCKETS and (t - CMP_LEN) // CMP_STRIDE + 1 < LANES
    assert (b * t) % ROWS_INPROJ == 0 and (b * t) % ROWS_TAIL == 0
    tzd, bias_c, qaux = _bias_tables(rel_bias, t)
    ovt, kaux, u, gexp = _const_tables(t)
    half = CMP_LEN // 2 * HEAD_DIM
    xf = x.reshape(b * t, d)
    for layer in range(depth):
        w_arr, w_gm = _arrange_w_in(w_in[layer])
        assert w_arr.shape[1] == _N_CHUNKS * LANES
        proj, kvc = _inproj(xf, norm_attn[layer][None, :], w_arr)
        proj = proj.reshape(b, t, -1)

        kvc = kvc.reshape(b, t, 4, HEAD_DIM)
        kvc = jnp.transpose(kvc, (0, 2, 1, 3)).reshape(b, 4, t // CMP_STRIDE, CMP_STRIDE * HEAD_DIM)
        pe = jnp.stack([cmp_k_pe[layer], cmp_v_pe[layer]]).reshape(2, 1, CMP_LEN * HEAD_DIM)
        w1 = jnp.stack([cmp_k_w1[layer], cmp_v_w1[layer]]).astype(BF16)
        b1 = jnp.stack([cmp_k_b1[layer], cmp_v_b1[layer]])[:, None, :]
        w2 = jnp.stack([cmp_k_w2[layer], cmp_v_w2[layer]])
        w2d = jnp.concatenate([w2, w2], axis=2).astype(BF16)
        kvc = _compress(kvc, pe[:, :, :half], pe[:, :, half:], w1[:, :half], w1[:, half:], b1, w2d)

        o_nsa = _nsa(proj, kvc, bias_c, tzd, ovt, qaux, kaux, gexp)
        o_sb = _sb(proj, u)
        xf = _tail(xf, o_nsa.reshape(b * t, -1), o_sb.reshape(b * t, -1), norm_attn[layer][None, :], w_gm,
                   w_up_nsa[layer].astype(BF16), w_up_sb[layer].astype(BF16), w_out[layer].astype(BF16),
                   norm_mlp[layer][None, :], w_ff1[layer].astype(BF16), w_ff2[layer].astype(BF16),
                   norm_final[None, :], layer == depth - 1)
    return xf.reshape(b, t, d)
```

```python
import functools
import math

import numpy as np
import jax
import jax.numpy as jnp
from jax import lax
from jax.experimental import pallas as pl
from jax.experimental.pallas import tpu as pltpu

HEAD_DIM = 64
NSA_HEADS = 8
NSA_KV_HEADS = 2
NSA_GROUP = NSA_HEADS // NSA_KV_HEADS
SB_HEADS = 8
CMP_LEN = 32
CMP_STRIDE = 16
CMP_HIDDEN = 256
SEL_LEN = 64
SEL_TOPK = 4
WINDOW = 512
N_BUCKETS = 32
MAX_DISTANCE = 128
EPS = 1e-6
NEG = -1e30
FORCE_BONUS = 1e4

LANES = 128
TN = 256
TS = 256
ROWS_INPROJ = 1024
ROWS_TAIL = 512
FF_CHUNK = 512
MXU_COLS = 256
SETUP_ROWS = 512
VMEM_LIMIT = 56 * 1024 * 1024
LOG2E = math.log2(math.e)
MASK_BIG = 2.0 ** 60
SEL_LANE0 = HEAD_DIM
BIAS_LANE = HEAD_DIM + N_BUCKETS

F32 = jnp.float32
BF16 = jnp.bfloat16

_C_SBQ = 0
_C_SBK = 4
_C_SBV = 8
_C_QN = 12
_C_KS = 16
_C_VS = 17
_C_KW = 18
_C_VW = 19
_C_GATE = 20
_C_KVC = 21
_N_CHUNKS = 23


def _dot(a, b):
    return jnp.dot(a, b, preferred_element_type=F32)


def _dot_nt(a, b):
    return lax.dot_general(a, b, (((1,), (1,)), ((), ())), preferred_element_type=F32)


def _rms(x, g):
    return x * lax.rsqrt(jnp.mean(x * x, axis=-1, keepdims=True) + EPS) * g


def _inproj_kernel(x_ref, g_ref, w_ref, o_ref, kvc_ref):
    h = _rms(x_ref[...], g_ref[...]).astype(BF16)
    n = o_ref.shape[1]
    for c in range(0, n, MXU_COLS):
        cs = slice(c, min(c + MXU_COLS, n))
        o_ref[:, cs] = _dot(h, w_ref[:, cs]).astype(BF16)
    kvc_ref[...] = _dot(h, w_ref[:, n:]).astype(BF16)


def _inproj(x2, g, w):
    m, d = x2.shape
    n = _C_KVC * LANES
    nc = w.shape[1] - n
    tm = ROWS_INPROJ
    return pl.pallas_call(
        _inproj_kernel,
        grid=(m // tm,),
        in_specs=[pl.BlockSpec((tm, d), lambda i: (i, 0)),
                  pl.BlockSpec((1, d), lambda i: (0, 0)),
                  pl.BlockSpec(w.shape, lambda i: (0, 0), pipeline_mode=pl.Buffered(1))],
        out_specs=[pl.BlockSpec((tm, n), lambda i: (i, 0)), pl.BlockSpec((tm, nc), lambda i: (i, 0))],
        out_shape=[jax.ShapeDtypeStruct((m, n), BF16), jax.ShapeDtypeStruct((m, nc), BF16)],
        compiler_params=pltpu.CompilerParams(dimension_semantics=("parallel",),
                                             vmem_limit_bytes=VMEM_LIMIT),
        name="inproj",
    )(x2, g, w)


def _gelu_tanh(x):
    return 0.5 * x * (1.0 + jnp.tanh(math.sqrt(2.0 / math.pi) * (x + 0.044715 * (x * x * x))))


def _compress_kernel(x_ref, pea_ref, peb_ref, w1a_ref, w1b_ref, b1_ref, w2_ref, o_ref, xf_ref):
    t = x_ref.shape[1]
    nchunk = t // CMP_STRIDE
    for j in range(x_ref.shape[2] // LANES):
        xf_ref[j] = x_ref[0, :, j * LANES:(j + 1) * LANES].astype(F32)
    lo = lax.broadcasted_iota(jnp.int32, (nchunk, LANES), 1) < HEAD_DIM
    for s in range(4):
        r = s // 2
        tiles = []
        for m in range(CMP_STRIDE // 2):
            a = xf_ref[s // 2, pl.ds(2 * m, nchunk, stride=CMP_STRIDE), :]
            b = xf_ref[s // 2, pl.ds(2 * m + 1, nchunk, stride=CMP_STRIDE), :]
            if s % 2 == 0:
                tiles.append(jnp.where(lo, a, pltpu.roll(b, HEAD_DIM, 1)))
            else:
                tiles.append(jnp.where(lo, pltpu.roll(a, HEAD_DIM, 1), b))
        x = jnp.concatenate(tiles, axis=1)
        a = _dot((x + pea_ref[r]).astype(BF16), w1a_ref[r])
        b = _dot((x + peb_ref[r]).astype(BF16), w1b_ref[r])
        hid = a + pltpu.roll(b, shift=b.shape[0] - 1, axis=0) + b1_ref[r]
        o_ref[0, s] = _dot(_gelu_tanh(hid).astype(BF16), w2_ref[r]).astype(BF16)


def _compress(xc, pea, peb, w1a, w1b, b1, w2d):
    b, t, width = xc.shape
    nchunk = t // CMP_STRIDE
    full = lambda shape: pl.BlockSpec(shape, lambda i: (0,) * len(shape))
    return pl.pallas_call(
        _compress_kernel,
        grid=(b,),
        in_specs=[pl.BlockSpec((1, t, width), lambda i: (i, 0, 0)),
                  full(pea.shape), full(peb.shape), full(w1a.shape), full(w1b.shape),
                  full(b1.shape), full(w2d.shape)],
        out_specs=pl.BlockSpec((1, 4, nchunk, LANES), lambda i: (i, 0, 0, 0)),
        out_shape=jax.ShapeDtypeStruct((b, 4, nchunk, LANES), BF16),
        scratch_shapes=[pltpu.VMEM((width // LANES, t, LANES), F32)],
        compiler_params=pltpu.CompilerParams(dimension_semantics=("parallel",),
                                             vmem_limit_bytes=VMEM_LIMIT),
        name="compress",
    )(xc, pea, peb, w1a, w1b, b1, w2d)


def _nsa_kernel(q_ref, ks_ref, vs_ref, kw_ref, vw_ref, g_ref, gexp_ref, kvc_ref,
                bc_ref, tzd_ref, ovt_ref, qaux_ref, kaux_ref, o_ref,
                ksx_ref, vsx_ref, kwx_ref, vwx_ref, m_ref, acc_ref):
    i = pl.program_id(1)
    ng, hg, nh = NSA_KV_HEADS, NSA_GROUP, NSA_HEADS
    t = ks_ref.shape[1]
    lane = lax.broadcasted_iota(jnp.int32, (TN, LANES), 1)
    row = lax.broadcasted_iota(jnp.int32, (TN, LANES), 0)
    lo = lane < HEAD_DIM
    t_abs = i * TN + row
    krow = lax.broadcasted_iota(jnp.int32, (TN, TN), 0)
    kcol = lax.broadcasted_iota(jnp.int32, (TN, TN), 1)

    def hrows(h):
        return slice(h * TN, (h + 1) * TN)

    @pl.when(i == 0)
    def _():
        ch = SETUP_ROWS
        lo_c = lax.broadcasted_iota(jnp.int32, (ch, LANES), 1) < HEAD_DIM
        for r0 in range(0, t, ch):
            rs = slice(r0, r0 + ch)
            for g in range(ng):
                def own(ref):
                    x = ref[0, rs].astype(F32)
                    return x if g == 0 else pltpu.roll(x, HEAD_DIM, 1)

                ksx_ref[g, rs] = jnp.where(lo_c, own(ks_ref), kaux_ref[0, rs].astype(F32)).astype(BF16)
                kwx_ref[g, rs] = jnp.where(lo_c, own(kw_ref), kaux_ref[1, rs].astype(F32)).astype(BF16)
                vsx_ref[g, rs] = jnp.where(lo_c, own(vs_ref), 1.0).astype(BF16)
                vwx_ref[g, rs] = jnp.where(lo_c, own(vw_ref), 1.0).astype(BF16)

    base = []
    for c in range(nh // 2):
        qc = q_ref[0, :, c * LANES:(c + 1) * LANES].astype(F32)
        for h, qh in ((2 * c, qc), (2 * c + 1, pltpu.roll(qc, HEAD_DIM, 1))):
            base.append(jnp.where(lo, qh, qaux_ref[h]))

    lo_c = lax.broadcasted_iota(jnp.int32, (LANES, LANES), 1) < HEAD_DIM
    valid_c = (t_abs >= CMP_STRIDE * lane + (CMP_LEN - 1)) & (lane < LANES - 1)
    s_cmp = [_dot_nt(jnp.concatenate(base[g * hg:(g + 1) * hg], axis=0).astype(BF16),
                     jnp.where(lo_c, kvc_ref[0, g].astype(F32), 0.0).astype(BF16)) for g in range(ng)]
    ps = []
    for h in range(nh):
        sh = jnp.where(valid_c, s_cmp[h // hg][hrows(h % hg)] + bc_ref[h, 0], NEG)
        e = jnp.exp2(sh - jnp.max(sh, axis=1, keepdims=True))
        ps.append(jnp.where(valid_c, e, 0.0) / jnp.sum(e, axis=1, keepdims=True))
    o_cmp = [_dot(jnp.concatenate(ps[g * hg:(g + 1) * hg], axis=0).astype(BF16), kvc_ref[0, ng + g])
             for g in range(ng)]

    n_sel = t // SEL_LEN
    nidx = lax.broadcasted_iota(jnp.int32, (n_sel, TN), 0)
    tq = i * TN + lax.broadcasted_iota(jnp.int32, (n_sel, TN), 1)
    forced = (nidx == tq // SEL_LEN) | (nidx == 0)
    sel_ok = nidx * SEL_LEN <= tq
    nidx_f = nidx.astype(F32)
    scores, selms = [], []
    for g in range(ng):
        pg = ps[g * hg:(g + 1) * hg]
        psum = (pg[0] + pg[1]) + (pg[2] + pg[3])
        p_hi = psum.astype(BF16)
        p_lo = (psum - p_hi.astype(F32)).astype(BF16)
        imp = (_dot_nt(ovt_ref[...], p_hi) + _dot_nt(ovt_ref[...], p_lo))[:n_sel]
        scores.append(jnp.where(forced, -jnp.inf, jnp.where(sel_ok, imp, -FORCE_BONUS)))
        selms.append(jnp.where(forced, 1.0, 0.0))

    def process(jobs, first):
        ss = [post(_dot_nt(lhs[br][h], kt)) for (br, h, kt, _, post) in jobs]
        ps, alphas = [], []
        for (br, h, _, _, _), sh in zip(jobs, ss):
            mx = jnp.max(sh, axis=1, keepdims=True)
            if first:
                m_new = jnp.broadcast_to(mx, (TN, LANES))
                alphas.append(None)
            else:
                m_old = m_ref[br, h]
                m_new = jnp.maximum(m_old, mx)
                alphas.append(jnp.exp2(m_old - m_new))
            m_ref[br, h] = m_new
            ps.append(jnp.exp2(sh - jnp.concatenate([m_new] * (sh.shape[1] // LANES), axis=1)).astype(BF16))
        for (br, h, _, vt, _), p, alpha in zip(jobs, ps, alphas):
            pv = _dot(p, vt)
            acc_ref[br, h] = pv if first else alpha * acc_ref[br, h] + pv

    def pair_out(br, he, ho):
        acc_e, acc_o = acc_ref[br, he], acc_ref[br, ho]
        num = jnp.where(lo, acc_e, pltpu.roll(acc_o, HEAD_DIM, 1))
        den = jnp.where(lo, pltpu.roll(acc_e, HEAD_DIM, 1), acc_o)
        return num / den

    kxs, vxs = (ksx_ref, kwx_ref), (vsx_ref, vwx_ref)
    causal = kcol <= krow
    inside = kcol > krow

    def tile_jobs(branches, off, post, width=TN):
        return [(br, h, kxs[br][h // hg, pl.ds(off, width), :], vxs[br][h // hg, pl.ds(off, width), :],
                 functools.partial(post, h)) for br in branches for h in range(nh)]

    for k in range(SEL_TOPK - 1):
        for g in range(ng):
            mx = jnp.max(scores[g], axis=0, keepdims=True)
            idx = jnp.min(jnp.where(scores[g] == mx, nidx_f, float(LANES)), axis=0, keepdims=True)
            hit = nidx_f == idx
            if k == SEL_TOPK - 2:
                hit = hit & (tq < SEL_LEN)
            selms[g] = jnp.where(hit, 1.0, selms[g])
            scores[g] = jnp.where(hit, -jnp.inf, scores[g])
    pad_lo = jnp.zeros((SEL_LANE0, TN), F32)
    pad_hi = jnp.zeros((LANES - SEL_LANE0 - n_sel, TN), F32)
    negsel = [jnp.transpose(jnp.concatenate([pad_lo, (sm - 1.0) * MASK_BIG, pad_hi], axis=0)) for sm in selms]
    lhs = [[(base[h] + negsel[h // hg]).astype(BF16) for h in range(nh)],
           [bh.astype(BF16) for bh in base]]

    back = WINDOW // TN

    def near_post(n_tiles, br, h, s):
        chunks = []
        for c in range(n_tiles):
            r, sc = n_tiles - 1 - c, s[:, c * TN:(c + 1) * TN]
            if r == 0:
                sc = jnp.where(causal, sc + tzd_ref[h, 0], NEG)
            elif r == 1:
                sc = sc + tzd_ref[h, 1]
            elif br == 1:
                sc = jnp.where(inside, sc, NEG)
            chunks.append(sc)
        return jnp.concatenate(chunks, axis=1)

    def near_jobs(n_tiles):
        off = pl.multiple_of(jnp.maximum(i - (n_tiles - 1), 0) * TN, TN)
        return [job for br in (1, 0)
                for job in tile_jobs((br,), off, functools.partial(near_post, n_tiles, br), n_tiles * TN)]

    for n_tiles in range(1, back + 2):
        @pl.when((i == n_tiles - 1) if n_tiles <= back else (i >= back))
        def _():
            process(near_jobs(n_tiles), True)

    n_far = jnp.maximum(i - back, 0)

    def sel_body(jj, carry):
        process(tile_jobs((0,), pl.multiple_of(2 * jj * TN, TN), lambda h, s: s, 2 * TN), False)
        return carry

    lax.fori_loop(0, n_far // 2, sel_body, 0)

    @pl.when(n_far % 2 == 1)
    def _():
        process(tile_jobs((0,), pl.multiple_of(jnp.maximum(n_far - 1, 0) * TN, TN), lambda h, s: s), False)

    width = hg * HEAD_DIM
    for g in range(ng):
        gates = 0.5 * jnp.tanh(0.5 * _dot(g_ref[0], gexp_ref[g])) + 0.5
        for c in range(hg // 2):
            g_cmp, g_sel, g_win = (gates[:, br * width + c * LANES:br * width + (c + 1) * LANES] for br in range(3))
            he, ho = g * hg + 2 * c, g * hg + 2 * c + 1
            out = (g_cmp * jnp.where(lo, o_cmp[g][hrows(2 * c)], o_cmp[g][hrows(2 * c + 1)])
                   + g_sel * pair_out(0, he, ho) + g_win * pair_out(1, he, ho))
            o_ref[0, :, (he // 2) * LANES:(he // 2 + 1) * LANES] = out.astype(BF16)


def _nsa(p3, kvc, bias_c, tzd, ovt, qaux, kaux, gexp):
    b, t, _ = p3.shape
    nq = t // TN
    width = NSA_HEADS * HEAD_DIM
    seq = lambda col: pl.BlockSpec((1, t, LANES), lambda bi, i, col=col: (bi, 0, col))
    const = lambda a: pl.BlockSpec(a.shape, lambda bi, i: (0,) * a.ndim, pipeline_mode=pl.Buffered(1))
    return pl.pallas_call(
        _nsa_kernel,
        grid=(b, nq),
        in_specs=[pl.BlockSpec((1, TN, width), lambda bi, i: (bi, i, _C_QN * LANES // width)),
                  seq(_C_KS), seq(_C_VS), seq(_C_KW), seq(_C_VW),
                  pl.BlockSpec((1, TN, LANES), lambda bi, i: (bi, i, _C_GATE)),
                  const(gexp),
                  pl.BlockSpec((1,) + kvc.shape[1:], lambda bi, i: (bi, 0, 0, 0)),
                  pl.BlockSpec((NSA_HEADS, 1, TN, LANES), lambda bi, i: (0, i, 0, 0)),
                  const(tzd), const(ovt), const(qaux), const(kaux)],
        out_specs=pl.BlockSpec((1, TN, width), lambda bi, i: (bi, i, 0)),
        out_shape=jax.ShapeDtypeStruct((b, t, width), BF16),
        scratch_shapes=[pltpu.VMEM((NSA_KV_HEADS, t, LANES), BF16)] * 4
                       + [pltpu.VMEM((2, NSA_HEADS, TN, LANES), F32)] * 2,
        compiler_params=pltpu.CompilerParams(dimension_semantics=("parallel", "arbitrary"),
                                             vmem_limit_bytes=VMEM_LIMIT),
        name="nsa",
    )(p3, p3, p3, p3, p3, p3, gexp, kvc, bias_c, tzd, ovt, qaux, kaux)


def _sb_kernel(q_ref, k_ref, v_ref, u_ref, o_ref, acc_ref, carry_ref):
    i = pl.program_id(1)
    nb = q_ref.shape[0]
    chains = [(bb, c) for bb in range(nb) for c in range(SB_HEADS // 2)]
    cols = [slice(c * LANES, (c + 1) * LANES) for c in range(SB_HEADS // 2)]
    lane = lax.broadcasted_iota(jnp.int32, (TS, LANES), 1)
    lo = lane < HEAD_DIM

    qs = []
    for bb, c in chains:
        q = q_ref[bb, :, cols[c]].astype(F32)
        qs.append(jnp.concatenate([jnp.where(lo, q, 0.0), jnp.where(lo, 0.0, q)], axis=0).astype(BF16))

    def tile(j):
        off = pl.multiple_of(j * TS, TS)
        zs = [_dot_nt(qs[n], k_ref[bb, pl.ds(off, TS), cols[c]]) for n, (bb, c) in enumerate(chains)]
        log_betas, log_keeps, k_bfs = [], [], []
        for z in zs:
            z_neg = jnp.minimum(z, 0.0)
            log_beta = z_neg - jnp.log(1.0 + jnp.exp2((z_neg + z_neg) - z)) * LOG2E
            log_keep = log_beta - z
            log_betas.append(log_beta)
            log_keeps.append(log_keep)
            k_bfs.append(log_keep.astype(BF16))
        laters = [_dot(k_bf, u_ref[...]) for k_bf in k_bfs]
        a_s = []
        for n, (bb, c) in enumerate(chains):
            carry = carry_ref[bb, c]
            a = jnp.exp2(log_betas[n] + laters[n] + jnp.concatenate([carry] * (TS // LANES), axis=1))
            a_s.append(a.astype(BF16))
            carry_ref[bb, c] = carry + jnp.sum(log_keeps[n], axis=1, keepdims=True)
        for n, (bb, c) in enumerate(chains):
            acc_ref[bb, c] += _dot(a_s[n], v_ref[bb, pl.ds(off, TS), cols[c]])

    def diag_tile():
        off = pl.multiple_of(i * TS, TS)
        half = TS // 2
        mask_full = (lax.broadcasted_iota(jnp.int32, (TS, TS), 1)
                     < (lax.broadcasted_iota(jnp.int32, (TS, TS), 0) & (half - 1)) + half)
        mask_half = (lax.broadcasted_iota(jnp.int32, (TS, half), 1)
                     < (lax.broadcasted_iota(jnp.int32, (TS, half), 0) & (half - 1)))
        pieces = []
        for n in range(len(chains)):
            pieces.append((n, (slice(half, TS), slice(TS + half, 2 * TS)), TS, mask_full))
            pieces.append((n, (slice(0, half), slice(TS, TS + half)), half, mask_half))
        zs = []
        for n, rows, nk, _ in pieces:
            bb, c = chains[n]
            q = jnp.concatenate([qs[n][rows[0]], qs[n][rows[1]]], axis=0)
            zs.append(_dot_nt(q, k_ref[bb, pl.ds(off, nk), cols[c]]))
        log_betas, log_keeps, k_bfs = [], [], []
        for z, (_, _, _, mask) in zip(zs, pieces):
            z_neg = jnp.minimum(z, 0.0)
            log_beta = z_neg - jnp.log(1.0 + jnp.exp2((z_neg + z_neg) - z)) * LOG2E
            log_keep = jnp.where(mask, log_beta - z, 0.0)
            log_betas.append(log_beta)
            log_keeps.append(log_keep)
            k_bfs.append(log_keep.astype(BF16))
        laters = [_dot(k_bf, u_ref[:nk, :nk]) for k_bf, (_, _, nk, _) in zip(k_bfs, pieces)]
        a_s = []
        for w, (n, rows, nk, mask) in enumerate(pieces):
            bb, c = chains[n]
            a_s.append(jnp.where(mask, jnp.exp2(log_betas[w] + laters[w]), 0.0).astype(BF16))
            for rs, part in zip(rows, (log_keeps[w][:half], log_keeps[w][half:])):
                carry_ref[bb, c, rs] = jnp.broadcast_to(jnp.sum(part, axis=1, keepdims=True), (half, LANES))
        for w, (n, rows, nk, _) in enumerate(pieces):
            bb, c = chains[n]
            pv = _dot(a_s[w], v_ref[bb, pl.ds(off, nk), cols[c]])
            acc_ref[bb, c, rows[0]] = pv[:half]
            acc_ref[bb, c, rows[1]] = pv[half:]

    diag_tile()

    def body(n, carry):
        tile(i - 1 - n)
        return carry

    lax.fori_loop(0, i, body, 0)
    for bb, c in chains:
        o = acc_ref[bb, c]
        o_ref[bb, :, cols[c]] = jnp.where(lo, o[:TS], o[TS:]).astype(BF16)


def _sb(p3, u):
    b, t, _ = p3.shape
    width = SB_HEADS * HEAD_DIM
    nb = 2 if b % 2 == 0 else 1
    return pl.pallas_call(
        _sb_kernel,
        grid=(b // nb, t // TS),
        in_specs=[pl.BlockSpec((nb, TS, width), lambda bi, i: (bi, i, _C_SBQ * LANES // width)),
                  pl.BlockSpec((nb, t, width), lambda bi, i: (bi, 0, _C_SBK * LANES // width)),
                  pl.BlockSpec((nb, t, width), lambda bi, i: (bi, 0, _C_SBV * LANES // width)),
                  pl.BlockSpec((TS, TS), lambda bi, i: (0, 0))],
        out_specs=pl.BlockSpec((nb, TS, width), lambda bi, i: (bi, i, 0)),
        out_shape=jax.ShapeDtypeStruct((b, t, width), BF16),
        scratch_shapes=[pltpu.VMEM((nb, SB_HEADS // 2, 2 * TS, LANES), F32),
                        pltpu.VMEM((nb, SB_HEADS // 2, 2 * TS, LANES), F32)],
        compiler_params=pltpu.CompilerParams(dimension_semantics=("parallel", "arbitrary"),
                                             vmem_limit_bytes=VMEM_LIMIT),
        name="sb",
    )(p3, p3, p3, u)


def _tail_kernel(x_ref, on_ref, os_ref, ga_ref, wgm_ref, wun_ref, wus_ref, wo_ref, gm_ref,
                 w1_ref, w2_ref, gf_ref, o_ref, *, last_layer):
    x = x_ref[...]
    d = x.shape[1]
    h = _rms(x, ga_ref[...]).astype(BF16)
    gm = 1.0 / (1.0 + jnp.exp(-_dot(h, wgm_ref[...])))
    mixed = gm[:, :d] * _dot(on_ref[...], wun_ref[...]) + gm[:, d:] * _dot(os_ref[...], wus_ref[...])
    x1 = x + _dot(mixed.astype(BF16), wo_ref[...])
    h2 = _rms(x1, gm_ref[...]).astype(BF16)
    dff = w1_ref.shape[1]
    acc = jnp.zeros_like(x1)
    for f in range(0, dff, FF_CHUNK):
        u = jnp.maximum(_dot(h2, w1_ref[:, f:f + FF_CHUNK]), 0.0)
        acc = acc + _dot((u * u).astype(BF16), w2_ref[f:f + FF_CHUNK, :])
    x2 = x1 + acc
    o_ref[...] = _rms(x2, gf_ref[...]) if last_layer else x2


def _tail(x2, o_nsa, o_sb, g_attn, wgm, wun, wus, wo, g_mlp, w1, w2, g_final, last_layer):
    m, d = x2.shape
    tm = ROWS_TAIL
    tok = lambda w: pl.BlockSpec((tm, w), lambda i: (i, 0))
    const = lambda a: pl.BlockSpec(a.shape, lambda i: (0, 0), pipeline_mode=pl.Buffered(1))
    return pl.pallas_call(
        functools.partial(_tail_kernel, last_layer=last_layer),
        grid=(m // tm,),
        in_specs=[tok(d), tok(o_nsa.shape[1]), tok(o_sb.shape[1]), const(g_attn), const(wgm),
                  const(wun), const(wus), const(wo), const(g_mlp), const(w1), const(w2), const(g_final)],
        out_specs=tok(d),
        out_shape=jax.ShapeDtypeStruct((m, d), F32),
        compiler_params=pltpu.CompilerParams(dimension_semantics=("parallel",),
                                             vmem_limit_bytes=VMEM_LIMIT),
        name="tail",
    )(x2, o_nsa, o_sb, g_attn, wgm, wun, wus, wo, g_mlp, w1, w2, g_final)


def _t5_bucket_np(dist):
    n = np.maximum(dist, 0)
    max_exact = N_BUCKETS // 2
    nf = np.maximum(n, 1).astype(np.float32)
    large = max_exact + (np.log(nf / np.float32(max_exact)) / np.float32(math.log(MAX_DISTANCE / max_exact))
                         * np.float32(N_BUCKETS - max_exact)).astype(np.int32)
    return np.where(n < max_exact, n, np.minimum(large, N_BUCKETS - 1)).astype(np.int32)


def _bias_tables(rel_bias, t):
    tbl = rel_bias * LOG2E
    far_idx = int(_t5_bucket_np(np.asarray(MAX_DISTANCE)))
    far = tbl[far_idx]

    def lookup(idx, table):
        onehot = jnp.asarray(idx.reshape(-1)[:, None] == np.arange(N_BUCKETS)[None, :], BF16)
        out = jnp.einsum("bh,nb->hn", table, onehot, precision=lax.Precision.HIGHEST)
        return out.reshape((table.shape[1],) + idx.shape)

    qi, kj = np.arange(TN)[:, None], np.arange(TN)[None, :]
    tzd = lookup(np.stack([_t5_bucket_np(r * TN + qi - kj) for r in range(2)]), tbl - far[None, :])
    dist_c = (np.arange(t).reshape(t // TN, TN, 1) - CMP_STRIDE * np.arange(LANES)[None, None, :]
              - (CMP_LEN - 1))
    bias_c = lookup(np.where(dist_c >= 0, _t5_bucket_np(dist_c), far_idx), tbl)
    far_hi = far.astype(BF16).astype(F32)
    qaux = jnp.zeros((rel_bias.shape[1], 1, LANES), F32)
    qaux = qaux.at[:, 0, BIAS_LANE].set(far_hi).at[:, 0, BIAS_LANE + 1].set(far - far_hi)
    return tzd, bias_c, qaux


def _const_tables(t):
    c_start = np.arange(LANES) * CMP_STRIDE
    s_start = np.arange(LANES) * SEL_LEN
    n_cmp = (t - CMP_LEN) // CMP_STRIDE + 1
    n_sel = t // SEL_LEN
    ov = ((c_start[:, None] < s_start[None, :] + SEL_LEN) & (c_start[:, None] + CMP_LEN > s_start[None, :])
          & (np.arange(LANES)[:, None] < n_cmp) & (np.arange(LANES)[None, :] < n_sel))
    kaux = np.zeros((2, t, LANES), np.float32)
    kaux[0, np.arange(t), SEL_LANE0 + np.arange(t) // SEL_LEN] = 1.0
    kaux[:, :, BIAS_LANE:BIAS_LANE + 2] = 1.0
    u = np.arange(TS)[:, None] > np.arange(TS)[None, :]
    gexp = np.zeros((NSA_KV_HEADS, LANES, 3 * NSA_GROUP * HEAD_DIM), np.float32)
    for h in range(NSA_HEADS):
        for br in range(3):
            c0 = (br * NSA_GROUP + h % NSA_GROUP) * HEAD_DIM
            gexp[h // NSA_GROUP, h * 3 + br, c0:c0 + HEAD_DIM] = 1.0
    return jnp.asarray(ov.T, BF16), jnp.asarray(kaux, BF16), jnp.asarray(u, BF16), jnp.asarray(gexp, BF16)


def _arrange_w_in(w):
    d = w.shape[0]
    nq, nkv, ng = NSA_HEADS * HEAD_DIM, 6 * NSA_KV_HEADS * HEAD_DIM, 3 * NSA_HEADS
    nsb = SB_HEADS * HEAD_DIM
    scale = HEAD_DIM ** -0.5 * LOG2E
    q = w[:, :nq] * scale
    kv = w[:, nq:nq + nkv]
    gl = w[:, nq + nkv:nq + nkv + ng]
    sb = w[:, nq + nkv + ng:nq + nkv + ng + 3 * nsb]
    gm = w[:, nq + nkv + ng + 3 * nsb:]
    kvw = NSA_KV_HEADS * HEAD_DIM
    assert kvw == LANES and ng <= LANES
    gates = jnp.pad(gl, ((0, 0), (0, LANES - ng)))
    cols = [sb[:, :nsb] * scale, sb[:, nsb:], q, kv[:, 2 * kvw:], gates, kv[:, :2 * kvw]]
    return jnp.concatenate(cols, axis=1).astype(BF16), gm.astype(BF16)


def kernel(x, norm_attn, w_in, rel_bias, cmp_k_pe, cmp_k_w1, cmp_k_b1, cmp_k_w2, cmp_v_pe, cmp_v_w1, cmp_v_b1, cmp_v_w2, w_up_nsa, w_up_sb, w_out, norm_mlp, w_ff1, w_ff2, norm_final):
    b, t, d = x.shape
    depth = w_in.shape[0]
    assert t % TN == 0 and t % TS == 0 and WINDOW == 2 * TN and MAX_DISTANCE <= TN
    assert t // SEL_LEN <= N_BUCKETS and (t - CMP_LEN) // CMP_STRIDE + 1 < LANES
    assert (b * t) % ROWS_INPROJ == 0 and (b * t) % ROWS_TAIL == 0
    tzd, bias_c, qaux = _bias_tables(rel_bias, t)
    ovt, kaux, u, gexp = _const_tables(t)
    half = CMP_LEN // 2 * HEAD_DIM
    xf = x.reshape(b * t, d)
    for layer in range(depth):
        w_arr, w_gm = _arrange_w_in(w_in[layer])
        assert w_arr.shape[1] == _N_CHUNKS * LANES
        proj, kvc = _inproj(xf, norm_attn[layer][None, :], w_arr)
        proj = proj.reshape(b, t, -1)

        kvc = kvc.reshape(b, t, -1)
        pe = jnp.stack([cmp_k_pe[layer], cmp_v_pe[layer]]).reshape(2, 1, CMP_LEN * HEAD_DIM)
        w1 = jnp.stack([cmp_k_w1[layer], cmp_v_w1[layer]]).astype(BF16)
        b1 = jnp.stack([cmp_k_b1[layer], cmp_v_b1[layer]])[:, None, :]
        w2 = jnp.stack([cmp_k_w2[layer], cmp_v_w2[layer]])
        w2d = jnp.concatenate([w2, w2], axis=2).astype(BF16)
        kvc = _compress(kvc, pe[:, :, :half], pe[:, :, half:], w1[:, :half], w1[:, half:], b1, w2d)

        o_nsa = _nsa(proj, kvc, bias_c, tzd, ovt, qaux, kaux, gexp)
        o_sb = _sb(proj, u)
        xf = _tail(xf, o_nsa.reshape(b * t, -1), o_sb.reshape(b * t, -1), norm_attn[layer][None, :], w_gm,
                   w_up_nsa[layer].astype(BF16), w_up_sb[layer].astype(BF16), w_out[layer].astype(BF16),
                   norm_mlp[layer][None, :], w_ff1[layer].astype(BF16), w_ff2[layer].astype(BF16),
                   norm_final[None, :], layer == depth - 1)
    return xf.reshape(b, t, d)
```

```python
import functools
import math

import numpy as np
import jax
import jax.numpy as jnp
from jax import lax
from jax.experimental import pallas as pl
from jax.experimental.pallas import tpu as pltpu

HEAD_DIM = 64
NSA_HEADS = 8
NSA_KV_HEADS = 2
NSA_GROUP = NSA_HEADS // NSA_KV_HEADS
SB_HEADS = 8
CMP_LEN = 32
CMP_STRIDE = 16
CMP_HIDDEN = 256
SEL_LEN = 64
SEL_TOPK = 4
WINDOW = 512
N_BUCKETS = 32
MAX_DISTANCE = 128
EPS = 1e-6
NEG = -1e30
FORCE_BONUS = 1e4

LANES = 128
TN = 256
TS = 256
ROWS_INPROJ = 1024
ROWS_TAIL = 512
FF_CHUNK = 512
MXU_COLS = 256
SETUP_ROWS = 512
VMEM_LIMIT = 56 * 1024 * 1024
LOG2E = math.log2(math.e)
MASK_BIG = 2.0 ** 60
SEL_LANE0 = HEAD_DIM
BIAS_LANE = HEAD_DIM + N_BUCKETS

F32 = jnp.float32
BF16 = jnp.bfloat16

_C_SBQ = 0
_C_SBK = 4
_C_SBV = 8
_C_QN = 12
_C_KS = 16
_C_VS = 17
_C_KW = 18
_C_VW = 19
_C_GATE = 20
_C_KVC = 21
_N_CHUNKS = 23


def _dot(a, b):
    return jnp.dot(a, b, preferred_element_type=F32)


def _dot_nt(a, b):
    return lax.dot_general(a, b, (((1,), (1,)), ((), ())), preferred_element_type=F32)


def _rms(x, g):
    return x * lax.rsqrt(jnp.mean(x * x, axis=-1, keepdims=True) + EPS) * g


def _inproj_kernel(x_ref, g_ref, w_ref, o_ref, kvc_ref):
    h = _rms(x_ref[...], g_ref[...]).astype(BF16)
    n = o_ref.shape[1]
    for c in range(0, n, MXU_COLS):
        cs = slice(c, min(c + MXU_COLS, n))
        o_ref[:, cs] = _dot(h, w_ref[:, cs]).astype(BF16)
    kvc_ref[...] = _dot(h, w_ref[:, n:]).astype(BF16)


def _inproj(x2, g, w):
    m, d = x2.shape
    n = _C_KVC * LANES
    nc = w.shape[1] - n
    tm = ROWS_INPROJ
    return pl.pallas_call(
        _inproj_kernel,
        grid=(m // tm,),
        in_specs=[pl.BlockSpec((tm, d), lambda i: (i, 0)),
                  pl.BlockSpec((1, d), lambda i: (0, 0)),
                  pl.BlockSpec(w.shape, lambda i: (0, 0), pipeline_mode=pl.Buffered(1))],
        out_specs=[pl.BlockSpec((tm, n), lambda i: (i, 0)), pl.BlockSpec((tm, nc), lambda i: (i, 0))],
        out_shape=[jax.ShapeDtypeStruct((m, n), BF16), jax.ShapeDtypeStruct((m, nc), BF16)],
        compiler_params=pltpu.CompilerParams(dimension_semantics=("parallel",),
                                             vmem_limit_bytes=VMEM_LIMIT),
        name="inproj",
    )(x2, g, w)


def _gelu_tanh(x):
    return 0.5 * x * (1.0 + jnp.tanh(math.sqrt(2.0 / math.pi) * (x + 0.044715 * (x * x * x))))


def _compress_kernel(x_ref, pea_ref, peb_ref, w1a_ref, w1b_ref, b1_ref, w2_ref, o_ref, xf_ref):
    t = x_ref.shape[1]
    nchunk = t // CMP_STRIDE
    for j in range(x_ref.shape[2] // LANES):
        xf_ref[j] = x_ref[0, :, j * LANES:(j + 1) * LANES].astype(F32)
    lo = lax.broadcasted_iota(jnp.int32, (nchunk, LANES), 1) < HEAD_DIM
    for s in range(4):
        r = s // 2
        tiles = []
        for m in range(CMP_STRIDE // 2):
            a = xf_ref[s // 2, pl.ds(2 * m, nchunk, stride=CMP_STRIDE), :]
            b = xf_ref[s // 2, pl.ds(2 * m + 1, nchunk, stride=CMP_STRIDE), :]
            if s % 2 == 0:
                tiles.append(jnp.where(lo, a, pltpu.roll(b, HEAD_DIM, 1)))
            else:
                tiles.append(jnp.where(lo, pltpu.roll(a, HEAD_DIM, 1), b))
        x = jnp.concatenate(tiles, axis=1)
        a = _dot((x + pea_ref[r]).astype(BF16), w1a_ref[r])
        b = _dot((x + peb_ref[r]).astype(BF16), w1b_ref[r])
        hid = a + pltpu.roll(b, shift=b.shape[0] - 1, axis=0) + b1_ref[r]
        o_ref[0, s] = _dot(_gelu_tanh(hid).astype(BF16), w2_ref[r]).astype(BF16)


def _compress(xc, pea, peb, w1a, w1b, b1, w2d):
    b, t, width = xc.shape
    nchunk = t // CMP_STRIDE
    full = lambda shape: pl.BlockSpec(shape, lambda i: (0,) * len(shape))
    return pl.pallas_call(
        _compress_kernel,
        grid=(b,),
        in_specs=[pl.BlockSpec((1, t, width), lambda i: (i, 0, 0)),
                  full(pea.shape), full(peb.shape), full(w1a.shape), full(w1b.shape),
                  full(b1.shape), full(w2d.shape)],
        out_specs=pl.BlockSpec((1, 4, nchunk, LANES), lambda i: (i, 0, 0, 0)),
        out_shape=jax.ShapeDtypeStruct((b, 4, nchunk, LANES), BF16),
        scratch_shapes=[pltpu.VMEM((width // LANES, t, LANES), F32)],
        compiler_params=pltpu.CompilerParams(dimension_semantics=("parallel",),
                                             vmem_limit_bytes=VMEM_LIMIT),
        name="compress",
    )(xc, pea, peb, w1a, w1b, b1, w2d)


def _nsa_kernel(q_ref, ks_ref, vs_ref, kw_ref, vw_ref, g_ref, gexp_ref, kvc_ref,
                bc_ref, tzd_ref, ovt_ref, qaux_ref, kaux_ref, o_ref,
                ksx_ref, vsx_ref, kwx_ref, vwx_ref, m_ref, acc_ref):
    i = pl.program_id(1)
    ng, hg, nh = NSA_KV_HEADS, NSA_GROUP, NSA_HEADS
    t = ks_ref.shape[1]
    lane = lax.broadcasted_iota(jnp.int32, (TN, LANES), 1)
    row = lax.broadcasted_iota(jnp.int32, (TN, LANES), 0)
    lo = lane < HEAD_DIM
    t_abs = i * TN + row
    krow = lax.broadcasted_iota(jnp.int32, (TN, TN), 0)
    kcol = lax.broadcasted_iota(jnp.int32, (TN, TN), 1)

    def hrows(h):
        return slice(h * TN, (h + 1) * TN)

    @pl.when(i == 0)
    def _():
        ch = SETUP_ROWS
        lo_c = lax.broadcasted_iota(jnp.int32, (ch, LANES), 1) < HEAD_DIM
        for r0 in range(0, t, ch):
            rs = slice(r0, r0 + ch)
            for g in range(ng):
                def own(ref):
                    x = ref[0, rs].astype(F32)
                    return x if g == 0 else pltpu.roll(x, HEAD_DIM, 1)

                ksx_ref[g, rs] = jnp.where(lo_c, own(ks_ref), kaux_ref[0, rs].astype(F32)).astype(BF16)
                kwx_ref[g, rs] = jnp.where(lo_c, own(kw_ref), kaux_ref[1, rs].astype(F32)).astype(BF16)
                vsx_ref[g, rs] = jnp.where(lo_c, own(vs_ref), 1.0).astype(BF16)
                vwx_ref[g, rs] = jnp.where(lo_c, own(vw_ref), 1.0).astype(BF16)

    base = []
    for c in range(nh // 2):
        qc = q_ref[0, :, c * LANES:(c + 1) * LANES].astype(F32)
        for h, qh in ((2 * c, qc), (2 * c + 1, pltpu.roll(qc, HEAD_DIM, 1))):
            base.append(jnp.where(lo, qh, qaux_ref[h]))

    lo_c = lax.broadcasted_iota(jnp.int32, (LANES, LANES), 1) < HEAD_DIM
    valid_c = (t_abs >= CMP_STRIDE * lane + (CMP_LEN - 1)) & (lane < LANES - 1)
    s_cmp = [_dot_nt(jnp.concatenate(base[g * hg:(g + 1) * hg], axis=0).astype(BF16),
                     jnp.where(lo_c, kvc_ref[0, g].astype(F32), 0.0).astype(BF16)) for g in range(ng)]
    ps = []
    for h in range(nh):
        sh = jnp.where(valid_c, s_cmp[h // hg][hrows(h % hg)] + bc_ref[h, 0], NEG)
        e = jnp.exp2(sh - jnp.max(sh, axis=1, keepdims=True))
        ps.append(jnp.where(valid_c, e, 0.0) / jnp.sum(e, axis=1, keepdims=True))
    o_cmp = [_dot(jnp.concatenate(ps[g * hg:(g + 1) * hg], axis=0).astype(BF16), kvc_ref[0, ng + g])
             for g in range(ng)]

    n_sel = t // SEL_LEN
    nidx = lax.broadcasted_iota(jnp.int32, (n_sel, TN), 0)
    tq = i * TN + lax.broadcasted_iota(jnp.int32, (n_sel, TN), 1)
    forced = (nidx == tq // SEL_LEN) | (nidx == 0)
    sel_ok = nidx * SEL_LEN <= tq
    nidx_f = nidx.astype(F32)
    scores, selms = [], []
    for g in range(ng):
        pg = ps[g * hg:(g + 1) * hg]
        psum = (pg[0] + pg[1]) + (pg[2] + pg[3])
        p_hi = psum.astype(BF16)
        p_lo = (psum - p_hi.astype(F32)).astype(BF16)
        imp = (_dot_nt(ovt_ref[...], p_hi) + _dot_nt(ovt_ref[...], p_lo))[:n_sel]
        scores.append(jnp.where(forced, -jnp.inf, jnp.where(sel_ok, imp, -FORCE_BONUS)))
        selms.append(jnp.where(forced, 1.0, 0.0))

    def process(jobs, first):
        ss = [post(_dot_nt(lhs[br][h], kt)) for (br, h, kt, _, post) in jobs]
        ps, alphas = [], []
        for (br, h, _, _, _), sh in zip(jobs, ss):
            mx = jnp.max(sh, axis=1, keepdims=True)
            if first:
                m_new = jnp.broadcast_to(mx, (TN, LANES))
                alphas.append(None)
            else:
                m_old = m_ref[br, h]
                m_new = jnp.maximum(m_old, mx)
                alphas.append(jnp.exp2(m_old - m_new))
            m_ref[br, h] = m_new
            ps.append(jnp.exp2(sh - jnp.concatenate([m_new] * (sh.shape[1] // LANES), axis=1)).astype(BF16))
        for (br, h, _, vt, _), p, alpha in zip(jobs, ps, alphas):
            pv = _dot(p, vt)
            acc_ref[br, h] = pv if first else alpha * acc_ref[br, h] + pv

    def pair_out(br, he, ho):
        acc_e, acc_o = acc_ref[br, he], acc_ref[br, ho]
        num = jnp.where(lo, acc_e, pltpu.roll(acc_o, HEAD_DIM, 1))
        den = jnp.where(lo, pltpu.roll(acc_e, HEAD_DIM, 1), acc_o)
        return num / den

    kxs, vxs = (ksx_ref, kwx_ref), (vsx_ref, vwx_ref)
    causal = kcol <= krow
    inside = kcol > krow

    def tile_jobs(branches, off, post, width=TN):
        return [(br, h, kxs[br][h // hg, pl.ds(off, width), :], vxs[br][h // hg, pl.ds(off, width), :],
                 functools.partial(post, h)) for br in branches for h in range(nh)]

    for k in range(SEL_TOPK - 1):
        for g in range(ng):
            mx = jnp.max(scores[g], axis=0, keepdims=True)
            idx = jnp.min(jnp.where(scores[g] == mx, nidx_f, float(LANES)), axis=0, keepdims=True)
            hit = nidx_f == idx
            if k == SEL_TOPK - 2:
                hit = hit & (tq < SEL_LEN)
            selms[g] = jnp.where(hit, 1.0, selms[g])
            scores[g] = jnp.where(hit, -jnp.inf, scores[g])
    pad_lo = jnp.zeros((SEL_LANE0, TN), F32)
    pad_hi = jnp.zeros((LANES - SEL_LANE0 - n_sel, TN), F32)
    negsel = [jnp.transpose(jnp.concatenate([pad_lo, (sm - 1.0) * MASK_BIG, pad_hi], axis=0)) for sm in selms]
    lhs = [[(base[h] + negsel[h // hg]).astype(BF16) for h in range(nh)],
           [bh.astype(BF16) for bh in base]]

    back = WINDOW // TN

    def near_post(n_tiles, br, h, s):
        chunks = []
        for c in range(n_tiles):
            r, sc = n_tiles - 1 - c, s[:, c * TN:(c + 1) * TN]
            if r == 0:
                sc = jnp.where(causal, sc + tzd_ref[h, 0], NEG)
            elif r == 1:
                sc = sc + tzd_ref[h, 1]
            elif br == 1:
                sc = jnp.where(inside, sc, NEG)
            chunks.append(sc)
        return jnp.concatenate(chunks, axis=1)

    def near_jobs(n_tiles):
        off = pl.multiple_of(jnp.maximum(i - (n_tiles - 1), 0) * TN, TN)
        return [job for br in (1, 0)
                for job in tile_jobs((br,), off, functools.partial(near_post, n_tiles, br), n_tiles * TN)]

    for n_tiles in range(1, back + 2):
        @pl.when((i == n_tiles - 1) if n_tiles <= back else (i >= back))
        def _():
            process(near_jobs(n_tiles), True)

    n_far = jnp.maximum(i - back, 0)

    def sel_body(jj, carry):
        process(tile_jobs((0,), pl.multiple_of(2 * jj * TN, TN), lambda h, s: s, 2 * TN), False)
        return carry

    lax.fori_loop(0, n_far // 2, sel_body, 0)

    @pl.when(n_far % 2 == 1)
    def _():
        process(tile_jobs((0,), pl.multiple_of(jnp.maximum(n_far - 1, 0) * TN, TN), lambda h, s: s), False)

    width = hg * HEAD_DIM
    for g in range(ng):
        gates = 0.5 * jnp.tanh(0.5 * _dot(g_ref[0], gexp_ref[g])) + 0.5
        for c in range(hg // 2):
            g_cmp, g_sel, g_win = (gates[:, br * width + c * LANES:br * width + (c + 1) * LANES] for br in range(3))
            he, ho = g * hg + 2 * c, g * hg + 2 * c + 1
            out = (g_cmp * jnp.where(lo, o_cmp[g][hrows(2 * c)], o_cmp[g][hrows(2 * c + 1)])
                   + g_sel * pair_out(0, he, ho) + g_win * pair_out(1, he, ho))
            o_ref[0, :, (he // 2) * LANES:(he // 2 + 1) * LANES] = out.astype(BF16)


def _nsa(p3, kvc, bias_c, tzd, ovt, qaux, kaux, gexp):
    b, t, _ = p3.shape
    nq = t // TN
    width = NSA_HEADS * HEAD_DIM
    seq = lambda col: pl.BlockSpec((1, t, LANES), lambda bi, i, col=col: (bi, 0, col))
    const = lambda a: pl.BlockSpec(a.shape, lambda bi, i: (0,) * a.ndim, pipeline_mode=pl.Buffered(1))
    return pl.pallas_call(
        _nsa_kernel,
        grid=(b, nq),
        in_specs=[pl.BlockSpec((1, TN, width), lambda bi, i: (bi, i, _C_QN * LANES // width)),
                  seq(_C_KS), seq(_C_VS), seq(_C_KW), seq(_C_VW),
                  pl.BlockSpec((1, TN, LANES), lambda bi, i: (bi, i, _C_GATE)),
                  const(gexp),
                  pl.BlockSpec((1,) + kvc.shape[1:], lambda bi, i: (bi, 0, 0, 0)),
                  pl.BlockSpec((NSA_HEADS, 1, TN, LANES), lambda bi, i: (0, i, 0, 0)),
                  const(tzd), const(ovt), const(qaux), const(kaux)],
        out_specs=pl.BlockSpec((1, TN, width), lambda bi, i: (bi, i, 0)),
        out_shape=jax.ShapeDtypeStruct((b, t, width), BF16),
        scratch_shapes=[pltpu.VMEM((NSA_KV_HEADS, t, LANES), BF16)] * 4
                       + [pltpu.VMEM((2, NSA_HEADS, TN, LANES), F32)] * 2,
        compiler_params=pltpu.CompilerParams(dimension_semantics=("parallel", "arbitrary"),
                                             vmem_limit_bytes=VMEM_LIMIT),
        name="nsa",
    )(p3, p3, p3, p3, p3, p3, gexp, kvc, bias_c, tzd, ovt, qaux, kaux)


def _sb_kernel(q_ref, k_ref, v_ref, u_ref, o_ref, acc_ref, carry_ref):
    i = pl.program_id(1)
    nb = q_ref.shape[0]
    chains = [(bb, c) for bb in range(nb) for c in range(SB_HEADS // 2)]
    cols = [slice(c * LANES, (c + 1) * LANES) for c in range(SB_HEADS // 2)]
    lane = lax.broadcasted_iota(jnp.int32, (TS, LANES), 1)
    lo = lane < HEAD_DIM

    qs = []
    for bb, c in chains:
        q = q_ref[bb, :, cols[c]].astype(F32)
        qs.append(jnp.concatenate([jnp.where(lo, q, 0.0), jnp.where(lo, 0.0, q)], axis=0).astype(BF16))

    def tile(j):
        off = pl.multiple_of(j * TS, TS)
        zs = [_dot_nt(qs[n], k_ref[bb, pl.ds(off, TS), cols[c]]) for n, (bb, c) in enumerate(chains)]
        log_betas, log_keeps, k_bfs = [], [], []
        for z in zs:
            z_neg = jnp.minimum(z, 0.0)
            log_beta = z_neg - jnp.log(1.0 + jnp.exp2((z_neg + z_neg) - z)) * LOG2E
            log_keep = log_beta - z
            log_betas.append(log_beta)
            log_keeps.append(log_keep)
            k_bfs.append(log_keep.astype(BF16))
        laters = [_dot(k_bf, u_ref[...]) for k_bf in k_bfs]
        a_s = []
        for n, (bb, c) in enumerate(chains):
            carry = carry_ref[bb, c]
            a = jnp.exp2(log_betas[n] + laters[n] + jnp.concatenate([carry] * (TS // LANES), axis=1))
            a_s.append(a.astype(BF16))
            carry_ref[bb, c] = carry + jnp.sum(log_keeps[n], axis=1, keepdims=True)
        for n, (bb, c) in enumerate(chains):
            acc_ref[bb, c] += _dot(a_s[n], v_ref[bb, pl.ds(off, TS), cols[c]])

    def diag_tile():
        off = pl.multiple_of(i * TS, TS)
        half = TS // 2
        mask_full = (lax.broadcasted_iota(jnp.int32, (TS, TS), 1)
                     < (lax.broadcasted_iota(jnp.int32, (TS, TS), 0) & (half - 1)) + half)
        mask_half = (lax.broadcasted_iota(jnp.int32, (TS, half), 1)
                     < (lax.broadcasted_iota(jnp.int32, (TS, half), 0) & (half - 1)))
        pieces = []
        for n in range(len(chains)):
            pieces.append((n, (slice(half, TS), slice(TS + half, 2 * TS)), TS, mask_full))
            pieces.append((n, (slice(0, half), slice(TS, TS + half)), half, mask_half))
        zs = []
        for n, rows, nk, _ in pieces:
            bb, c = chains[n]
            q = jnp.concatenate([qs[n][rows[0]], qs[n][rows[1]]], axis=0)
            zs.append(_dot_nt(q, k_ref[bb, pl.ds(off, nk), cols[c]]))
        log_betas, log_keeps, k_bfs = [], [], []
        for z, (_, _, _, mask) in zip(zs, pieces):
            z_neg = jnp.minimum(z, 0.0)
            log_beta = z_neg - jnp.log(1.0 + jnp.exp2((z_neg + z_neg) - z)) * LOG2E
            log_keep = jnp.where(mask, log_beta - z, 0.0)
            log_betas.append(log_beta)
            log_keeps.append(log_keep)
            k_bfs.append(log_keep.astype(BF16))
        laters = [_dot(k_bf, u_ref[:nk, :nk]) for k_bf, (_, _, nk, _) in zip(k_bfs, pieces)]
        a_s = []
        for w, (n, rows, nk, mask) in enumerate(pieces):
            bb, c = chains[n]
            a_s.append(jnp.where(mask, jnp.exp2(log_betas[w] + laters[w]), 0.0).astype(BF16))
            for rs, part in zip(rows, (log_keeps[w][:half], log_keeps[w][half:])):
                carry_ref[bb, c, rs] = jnp.broadcast_to(jnp.sum(part, axis=1, keepdims=True), (half, LANES))
        for w, (n, rows, nk, _) in enumerate(pieces):
            bb, c = chains[n]
            pv = _dot(a_s[w], v_ref[bb, pl.ds(off, nk), cols[c]])
            acc_ref[bb, c, rows[0]] = pv[:half]
            acc_ref[bb, c, rows[1]] = pv[half:]

    diag_tile()

    def body(n, carry):
        tile(i - 1 - n)
        return carry

    lax.fori_loop(0, i, body, 0)
    for bb, c in chains:
        o = acc_ref[bb, c]
        o_ref[bb, :, cols[c]] = jnp.where(lo, o[:TS], o[TS:]).astype(BF16)


def _sb(p3, u):
    b, t, _ = p3.shape
    width = SB_HEADS * HEAD_DIM
    nb = 2 if b % 2 == 0 else 1
    return pl.pallas_call(
        _sb_kernel,
        grid=(b // nb, t // TS),
        in_specs=[pl.BlockSpec((nb, TS, width), lambda bi, i: (bi, i, _C_SBQ * LANES // width)),
                  pl.BlockSpec((nb, t, width), lambda bi, i: (bi, 0, _C_SBK * LANES // width)),
                  pl.BlockSpec((nb, t, width), lambda bi, i: (bi, 0, _C_SBV * LANES // width)),
                  pl.BlockSpec((TS, TS), lambda bi, i: (0, 0))],
        out_specs=pl.BlockSpec((nb, TS, width), lambda bi, i: (bi, i, 0)),
        out_shape=jax.ShapeDtypeStruct((b, t, width), BF16),
        scratch_shapes=[pltpu.VMEM((nb, SB_HEADS // 2, 2 * TS, LANES), F32),
                        pltpu.VMEM((nb, SB_HEADS // 2, 2 * TS, LANES), F32)],
        compiler_params=pltpu.CompilerParams(dimension_semantics=("parallel", "arbitrary"),
                                             vmem_limit_bytes=VMEM_LIMIT),
        name="sb",
    )(p3, p3, p3, u)


def _tail_kernel(x_ref, on_ref, os_ref, ga_ref, wgm_ref, wun_ref, wus_ref, wo_ref, gm_ref,
                 w1_ref, w2_ref, gf_ref, o_ref, *, last_layer):
    x = x_ref[...]
    d = x.shape[1]
    h = _rms(x, ga_ref[...]).astype(BF16)
    gm = 1.0 / (1.0 + jnp.exp(-_dot(h, wgm_ref[...])))
    mixed = gm[:, :d] * _dot(on_ref[...], wun_ref[...]) + gm[:, d:] * _dot(os_ref[...], wus_ref[...])
    x1 = x + _dot(mixed.astype(BF16), wo_ref[...])
    h2 = _rms(x1, gm_ref[...]).astype(BF16)
    dff = w1_ref.shape[1]
    acc = jnp.zeros_like(x1)
    for f in range(0, dff, FF_CHUNK):
        u = jnp.maximum(_dot(h2, w1_ref[:, f:f + FF_CHUNK]), 0.0)
        acc = acc + _dot((u * u).astype(BF16), w2_ref[f:f + FF_CHUNK, :])
    x2 = x1 + acc
    o_ref[...] = _rms(x2, gf_ref[...]) if last_layer else x2


def _tail(x2, o_nsa, o_sb, g_attn, wgm, wun, wus, wo, g_mlp, w1, w2, g_final, last_layer):
    m, d = x2.shape
    tm = ROWS_TAIL
    tok = lambda w: pl.BlockSpec((tm, w), lambda i: (i, 0))
    const = lambda a: pl.BlockSpec(a.shape, lambda i: (0, 0), pipeline_mode=pl.Buffered(1))
    return pl.pallas_call(
        functools.partial(_tail_kernel, last_layer=last_layer),
        grid=(m // tm,),
        in_specs=[tok(d), tok(o_nsa.shape[1]), tok(o_sb.shape[1]), const(g_attn), const(wgm),
                  const(wun), const(wus), const(wo), const(g_mlp), const(w1), const(w2), const(g_final)],
        out_specs=tok(d),
        out_shape=jax.ShapeDtypeStruct((m, d), F32),
        compiler_params=pltpu.CompilerParams(dimension_semantics=("parallel",),
                                             vmem_limit_bytes=VMEM_LIMIT),
        name="tail",
    )(x2, o_nsa, o_sb, g_attn, wgm, wun, wus, wo, g_mlp, w1, w2, g_final)


def _t5_bucket_np(dist):
    n = np.maximum(dist, 0)
    max_exact = N_BUCKETS // 2
    nf = np.maximum(n, 1).astype(np.float32)
    large = max_exact + (np.log(nf / np.float32(max_exact)) / np.float32(math.log(MAX_DISTANCE / max_exact))
                         * np.float32(N_BUCKETS - max_exact)).astype(np.int32)
    return np.where(n < max_exact, n, np.minimum(large, N_BUCKETS - 1)).astype(np.int32)


def _bias_tables(rel_bias, t):
    tbl = rel_bias * LOG2E
    far_idx = int(_t5_bucket_np(np.asarray(MAX_DISTANCE)))
    far = tbl[far_idx]

    def lookup(idx, table):
        onehot = jnp.asarray(idx.reshape(-1)[:, None] == np.arange(N_BUCKETS)[None, :], BF16)
        out = jnp.einsum("bh,nb->hn", table, onehot, precision=lax.Precision.HIGHEST)
        return out.reshape((table.shape[1],) + idx.shape)

    qi, kj = np.arange(TN)[:, None], np.arange(TN)[None, :]
    tzd = lookup(np.stack([_t5_bucket_np(r * TN + qi - kj) for r in range(2)]), tbl - far[None, :])
    dist_c = (np.arange(t).reshape(t // TN, TN, 1) - CMP_STRIDE * np.arange(LANES)[None, None, :]
              - (CMP_LEN - 1))
    bias_c = lookup(np.where(dist_c >= 0, _t5_bucket_np(dist_c), far_idx), tbl)
    far_hi = far.astype(BF16).astype(F32)
    qaux = jnp.zeros((rel_bias.shape[1], 1, LANES), F32)
    qaux = qaux.at[:, 0, BIAS_LANE].set(far_hi).at[:, 0, BIAS_LANE + 1].set(far - far_hi)
    return tzd, bias_c, qaux


def _const_tables(t):
    c_start = np.arange(LANES) * CMP_STRIDE
    s_start = np.arange(LANES) * SEL_LEN
    n_cmp = (t - CMP_LEN) // CMP_STRIDE + 1
    n_sel = t // SEL_LEN
    ov = ((c_start[:, None] < s_start[None, :] + SEL_LEN) & (c_start[:, None] + CMP_LEN > s_start[None, :])
          & (np.arange(LANES)[:, None] < n_cmp) & (np.arange(LANES)[None, :] < n_sel))
    kaux = np.zeros((2, t, LANES), np.float32)
    kaux[0, np.arange(t), SEL_LANE0 + np.arange(t) // SEL_LEN] = 1.0
    kaux[:, :, BIAS_LANE:BIAS_LANE + 2] = 1.0
    u = np.arange(TS)[:, None] > np.arange(TS)[None, :]
    gexp = np.zeros((NSA_KV_HEADS, LANES, 3 * NSA_GROUP * HEAD_DIM), np.float32)
    for h in range(NSA_HEADS):
        for br in range(3):
            c0 = (br * NSA_GROUP + h % NSA_GROUP) * HEAD_DIM
            gexp[h // NSA_GROUP, h * 3 + br, c0:c0 + HEAD_DIM] = 1.0
    return jnp.asarray(ov.T, BF16), jnp.asarray(kaux, BF16), jnp.asarray(u, BF16), jnp.asarray(gexp, BF16)


def _arrange_kernel(w_ref, o_ref, gm_ref):
    nq, nkv, ng = NSA_HEADS * HEAD_DIM, 6 * NSA_KV_HEADS * HEAD_DIM, 3 * NSA_HEADS
    nsb = SB_HEADS * HEAD_DIM
    kvw = NSA_KV_HEADS * HEAD_DIM
    scale = HEAD_DIM ** -0.5 * LOG2E
    sb0 = nq + nkv + ng
    pieces = [(sb0, nsb, scale), (sb0 + nsb, 2 * nsb, None), (0, nq, scale),
              (nq + 2 * kvw, 4 * kvw, None), (nq + nkv, ng, None), (nq, 2 * kvw, None)]
    col = 0
    for src, width, mul in pieces:
        x = w_ref[:, src:src + width]
        if mul is not None:
            x = x * mul
        if width % LANES:
            pad = LANES - width % LANES
            x = jnp.concatenate([x, jnp.zeros((x.shape[0], pad), F32)], axis=1)
            width += pad
        o_ref[:, col:col + width] = x.astype(BF16)
        col += width
    gm_ref[...] = w_ref[:, sb0 + 3 * nsb:].astype(BF16)


def _arrange_w_in(w):
    d, n_in = w.shape
    n_gm = 2 * d
    rows = 128
    assert NSA_KV_HEADS * HEAD_DIM == LANES and 3 * NSA_HEADS <= LANES
    return pl.pallas_call(
        _arrange_kernel,
        grid=(d // rows,),
        in_specs=[pl.BlockSpec((rows, n_in), lambda i: (i, 0))],
        out_specs=[pl.BlockSpec((rows, _N_CHUNKS * LANES), lambda i: (i, 0)),
                   pl.BlockSpec((rows, n_gm), lambda i: (i, 0))],
        out_shape=[jax.ShapeDtypeStruct((d, _N_CHUNKS * LANES), BF16), jax.ShapeDtypeStruct((d, n_gm), BF16)],
        compiler_params=pltpu.CompilerParams(dimension_semantics=("parallel",),
                                             vmem_limit_bytes=VMEM_LIMIT),
        name="arrange",
    )(w)


def kernel(x, norm_attn, w_in, rel_bias, cmp_k_pe, cmp_k_w1, cmp_k_b1, cmp_k_w2, cmp_v_pe, cmp_v_w1, cmp_v_b1, cmp_v_w2, w_up_nsa, w_up_sb, w_out, norm_mlp, w_ff1, w_ff2, norm_final):
    b, t, d = x.shape
    depth = w_in.shape[0]
    assert t % TN == 0 and t % TS == 0 and WINDOW == 2 * TN and MAX_DISTANCE <= TN
    assert t // SEL_LEN <= N_BUCKETS and (t - CMP_LEN) // CMP_STRIDE + 1 < LANES
    assert (b * t) % ROWS_INPROJ == 0 and (b * t) % ROWS_TAIL == 0
    tzd, bias_c, qaux = _bias_tables(rel_bias, t)
    ovt, kaux, u, gexp = _const_tables(t)
    half = CMP_LEN // 2 * HEAD_DIM
    xf = x.reshape(b * t, d)
    for layer in range(depth):
        w_arr, w_gm = _arrange_w_in(w_in[layer])
        assert w_arr.shape[1] == _N_CHUNKS * LANES
        proj, kvc = _inproj(xf, norm_attn[layer][None, :], w_arr)
        proj = proj.reshape(b, t, -1)

        kvc = kvc.reshape(b, t, -1)
        pe = jnp.stack([cmp_k_pe[layer], cmp_v_pe[layer]]).reshape(2, 1, CMP_LEN * HEAD_DIM)
        w1 = jnp.stack([cmp_k_w1[layer], cmp_v_w1[layer]]).astype(BF16)
        b1 = jnp.stack([cmp_k_b1[layer], cmp_v_b1[layer]])[:, None, :]
        w2 = jnp.stack([cmp_k_w2[layer], cmp_v_w2[layer]])
        w2d = jnp.concatenate([w2, w2], axis=2).astype(BF16)
        kvc = _compress(kvc, pe[:, :, :half], pe[:, :, half:], w1[:, :half], w1[:, half:], b1, w2d)

        o_nsa = _nsa(proj, kvc, bias_c, tzd, ovt, qaux, kaux, gexp)
        o_sb = _sb(proj, u)
        xf = _tail(xf, o_nsa.reshape(b * t, -1), o_sb.reshape(b * t, -1), norm_attn[layer][None, :], w_gm,
                   w_up_nsa[layer].astype(BF16), w_up_sb[layer].astype(BF16), w_out[layer].astype(BF16),
                   norm_mlp[layer][None, :], w_ff1[layer].astype(BF16), w_ff2[layer].astype(BF16),
                   norm_final[None, :], layer == depth - 1)
    return xf.reshape(b, t, d)
```

```python
import functools
import math

import numpy as np
import jax
import jax.numpy as jnp
from jax import lax
from jax.experimental import pallas as pl
from jax.experimental.pallas import tpu as pltpu

HEAD_DIM = 64
NSA_HEADS = 8
NSA_KV_HEADS = 2
NSA_GROUP = NSA_HEADS // NSA_KV_HEADS
SB_HEADS = 8
CMP_LEN = 32
CMP_STRIDE = 16
CMP_HIDDEN = 256
SEL_LEN = 64
SEL_TOPK = 4
WINDOW = 512
N_BUCKETS = 32
MAX_DISTANCE = 128
EPS = 1e-6
NEG = -1e30
FORCE_BONUS = 1e4

LANES = 128
TN = 256
TS = 256
ROWS_INPROJ = 1024
ROWS_TAIL = 512
FF_CHUNK = 512
MXU_COLS = 256
SETUP_ROWS = 512
VMEM_LIMIT = 56 * 1024 * 1024
LOG2E = math.log2(math.e)
MASK_BIG = 2.0 ** 60
SEL_LANE0 = HEAD_DIM
BIAS_LANE = HEAD_DIM + N_BUCKETS

F32 = jnp.float32
BF16 = jnp.bfloat16

_C_SBQ = 0
_C_SBK = 4
_C_SBV = 8
_C_QN = 12
_C_KS = 16
_C_VS = 17
_C_KW = 18
_C_VW = 19
_C_GATE = 20
_C_KVC = 21
_N_CHUNKS = 23


def _dot(a, b):
    return jnp.dot(a, b, preferred_element_type=F32)


def _dot_nt(a, b):
    return lax.dot_general(a, b, (((1,), (1,)), ((), ())), preferred_element_type=F32)


def _rms(x, g):
    return x * lax.rsqrt(jnp.mean(x * x, axis=-1, keepdims=True) + EPS) * g


def _inproj_kernel(x_ref, g_ref, w_ref, o_ref, kvc_ref):
    h = _rms(x_ref[...], g_ref[...]).astype(BF16)
    n = o_ref.shape[1]
    for c in range(0, n, MXU_COLS):
        cs = slice(c, min(c + MXU_COLS, n))
        o_ref[:, cs] = _dot_nt(h, w_ref[cs, :]).astype(BF16)
    kvc_ref[...] = _dot_nt(h, w_ref[n:, :]).astype(BF16)


def _inproj(x2, g, w):
    m, d = x2.shape
    n = _C_KVC * LANES
    nc = w.shape[0] - n
    tm = ROWS_INPROJ
    return pl.pallas_call(
        _inproj_kernel,
        grid=(m // tm,),
        in_specs=[pl.BlockSpec((tm, d), lambda i: (i, 0)),
                  pl.BlockSpec((1, d), lambda i: (0, 0)),
                  pl.BlockSpec(w.shape, lambda i: (0, 0), pipeline_mode=pl.Buffered(1))],
        out_specs=[pl.BlockSpec((tm, n), lambda i: (i, 0)), pl.BlockSpec((tm, nc), lambda i: (i, 0))],
        out_shape=[jax.ShapeDtypeStruct((m, n), BF16), jax.ShapeDtypeStruct((m, nc), BF16)],
        compiler_params=pltpu.CompilerParams(dimension_semantics=("parallel",),
                                             vmem_limit_bytes=VMEM_LIMIT),
        name="inproj",
    )(x2, g, w)


def _gelu_tanh(x):
    return 0.5 * x * (1.0 + jnp.tanh(math.sqrt(2.0 / math.pi) * (x + 0.044715 * (x * x * x))))


def _compress_kernel(x_ref, pea_ref, peb_ref, w1a_ref, w1b_ref, b1_ref, w2_ref, o_ref, xf_ref):
    t = x_ref.shape[1]
    nchunk = t // CMP_STRIDE
    for j in range(x_ref.shape[2] // LANES):
        xf_ref[j] = x_ref[0, :, j * LANES:(j + 1) * LANES].astype(F32)
    lo = lax.broadcasted_iota(jnp.int32, (nchunk, LANES), 1) < HEAD_DIM
    for s in range(4):
        r = s // 2
        tiles = []
        for m in range(CMP_STRIDE // 2):
            a = xf_ref[s // 2, pl.ds(2 * m, nchunk, stride=CMP_STRIDE), :]
            b = xf_ref[s // 2, pl.ds(2 * m + 1, nchunk, stride=CMP_STRIDE), :]
            if s % 2 == 0:
                tiles.append(jnp.where(lo, a, pltpu.roll(b, HEAD_DIM, 1)))
            else:
                tiles.append(jnp.where(lo, pltpu.roll(a, HEAD_DIM, 1), b))
        x = jnp.concatenate(tiles, axis=1)
        a = _dot((x + pea_ref[r]).astype(BF16), w1a_ref[r])
        b = _dot((x + peb_ref[r]).astype(BF16), w1b_ref[r])
        hid = a + pltpu.roll(b, shift=b.shape[0] - 1, axis=0) + b1_ref[r]
        o_ref[0, s] = _dot(_gelu_tanh(hid).astype(BF16), w2_ref[r]).astype(BF16)


def _compress(xc, pea, peb, w1a, w1b, b1, w2d):
    b, t, width = xc.shape
    nchunk = t // CMP_STRIDE
    full = lambda shape: pl.BlockSpec(shape, lambda i: (0,) * len(shape))
    return pl.pallas_call(
        _compress_kernel,
        grid=(b,),
        in_specs=[pl.BlockSpec((1, t, width), lambda i: (i, 0, 0)),
                  full(pea.shape), full(peb.shape), full(w1a.shape), full(w1b.shape),
                  full(b1.shape), full(w2d.shape)],
        out_specs=pl.BlockSpec((1, 4, nchunk, LANES), lambda i: (i, 0, 0, 0)),
        out_shape=jax.ShapeDtypeStruct((b, 4, nchunk, LANES), BF16),
        scratch_shapes=[pltpu.VMEM((width // LANES, t, LANES), F32)],
        compiler_params=pltpu.CompilerParams(dimension_semantics=("parallel",),
                                             vmem_limit_bytes=VMEM_LIMIT),
        name="compress",
    )(xc, pea, peb, w1a, w1b, b1, w2d)


def _nsa_kernel(q_ref, ks_ref, vs_ref, kw_ref, vw_ref, g_ref, gexp_ref, kvc_ref,
                bc_ref, tzd_ref, ovt_ref, qaux_ref, kaux_ref, o_ref,
                ksx_ref, vsx_ref, kwx_ref, vwx_ref, m_ref, acc_ref):
    i = pl.program_id(1)
    ng, hg, nh = NSA_KV_HEADS, NSA_GROUP, NSA_HEADS
    t = ks_ref.shape[1]
    lane = lax.broadcasted_iota(jnp.int32, (TN, LANES), 1)
    row = lax.broadcasted_iota(jnp.int32, (TN, LANES), 0)
    lo = lane < HEAD_DIM
    t_abs = i * TN + row
    krow = lax.broadcasted_iota(jnp.int32, (TN, TN), 0)
    kcol = lax.broadcasted_iota(jnp.int32, (TN, TN), 1)

    def hrows(h):
        return slice(h * TN, (h + 1) * TN)

    @pl.when(i == 0)
    def _():
        ch = SETUP_ROWS
        lo_c = lax.broadcasted_iota(jnp.int32, (ch, LANES), 1) < HEAD_DIM
        for r0 in range(0, t, ch):
            rs = slice(r0, r0 + ch)
            for g in range(ng):
                def own(ref):
                    x = ref[0, rs].astype(F32)
                    return x if g == 0 else pltpu.roll(x, HEAD_DIM, 1)

                ksx_ref[g, rs] = jnp.where(lo_c, own(ks_ref), kaux_ref[0, rs].astype(F32)).astype(BF16)
                kwx_ref[g, rs] = jnp.where(lo_c, own(kw_ref), kaux_ref[1, rs].astype(F32)).astype(BF16)
                vsx_ref[g, rs] = jnp.where(lo_c, own(vs_ref), 1.0).astype(BF16)
                vwx_ref[g, rs] = jnp.where(lo_c, own(vw_ref), 1.0).astype(BF16)

    base = []
    for c in range(nh // 2):
        qc = q_ref[0, :, c * LANES:(c + 1) * LANES].astype(F32)
        for h, qh in ((2 * c, qc), (2 * c + 1, pltpu.roll(qc, HEAD_DIM, 1))):
            base.append(jnp.where(lo, qh, qaux_ref[h]))

    lo_c = lax.broadcasted_iota(jnp.int32, (LANES, LANES), 1) < HEAD_DIM
    valid_c = (t_abs >= CMP_STRIDE * lane + (CMP_LEN - 1)) & (lane < LANES - 1)
    s_cmp = [_dot_nt(jnp.concatenate(base[g * hg:(g + 1) * hg], axis=0).astype(BF16),
                     jnp.where(lo_c, kvc_ref[0, g].astype(F32), 0.0).astype(BF16)) for g in range(ng)]
    ps = []
    for h in range(nh):
        sh = jnp.where(valid_c, s_cmp[h // hg][hrows(h % hg)] + bc_ref[h, 0], NEG)
        e = jnp.exp2(sh - jnp.max(sh, axis=1, keepdims=True))
        ps.append(jnp.where(valid_c, e, 0.0) / jnp.sum(e, axis=1, keepdims=True))
    o_cmp = [_dot(jnp.concatenate(ps[g * hg:(g + 1) * hg], axis=0).astype(BF16), kvc_ref[0, ng + g])
             for g in range(ng)]

    n_sel = t // SEL_LEN
    nidx = lax.broadcasted_iota(jnp.int32, (n_sel, TN), 0)
    tq = i * TN + lax.broadcasted_iota(jnp.int32, (n_sel, TN), 1)
    forced = (nidx == tq // SEL_LEN) | (nidx == 0)
    sel_ok = nidx * SEL_LEN <= tq
    nidx_f = nidx.astype(F32)
    scores, selms = [], []
    for g in range(ng):
        pg = ps[g * hg:(g + 1) * hg]
        psum = (pg[0] + pg[1]) + (pg[2] + pg[3])
        p_hi = psum.astype(BF16)
        p_lo = (psum - p_hi.astype(F32)).astype(BF16)
        imp = (_dot_nt(ovt_ref[...], p_hi) + _dot_nt(ovt_ref[...], p_lo))[:n_sel]
        scores.append(jnp.where(forced, -jnp.inf, jnp.where(sel_ok, imp, -FORCE_BONUS)))
        selms.append(jnp.where(forced, 1.0, 0.0))

    def process(jobs, first):
        ss = [post(_dot_nt(lhs[br][h], kt)) for (br, h, kt, _, post) in jobs]
        ps, alphas = [], []
        for (br, h, _, _, _), sh in zip(jobs, ss):
            mx = jnp.max(sh, axis=1, keepdims=True)
            if first:
                m_new = jnp.broadcast_to(mx, (TN, LANES))
                alphas.append(None)
            else:
                m_old = m_ref[br, h]
                m_new = jnp.maximum(m_old, mx)
                alphas.append(jnp.exp2(m_old - m_new))
            m_ref[br, h] = m_new
            ps.append(jnp.exp2(sh - jnp.concatenate([m_new] * (sh.shape[1] // LANES), axis=1)).astype(BF16))
        for (br, h, _, vt, _), p, alpha in zip(jobs, ps, alphas):
            pv = _dot(p, vt)
            acc_ref[br, h] = pv if first else alpha * acc_ref[br, h] + pv

    def pair_out(br, he, ho):
        acc_e, acc_o = acc_ref[br, he], acc_ref[br, ho]
        num = jnp.where(lo, acc_e, pltpu.roll(acc_o, HEAD_DIM, 1))
        den = jnp.where(lo, pltpu.roll(acc_e, HEAD_DIM, 1), acc_o)
        return num / den

    kxs, vxs = (ksx_ref, kwx_ref), (vsx_ref, vwx_ref)
    causal = kcol <= krow
    inside = kcol > krow

    def tile_jobs(branches, off, post, width=TN):
        return [(br, h, kxs[br][h // hg, pl.ds(off, width), :], vxs[br][h // hg, pl.ds(off, width), :],
                 functools.partial(post, h)) for br in branches for h in range(nh)]

    for k in range(SEL_TOPK - 1):
        for g in range(ng):
            mx = jnp.max(scores[g], axis=0, keepdims=True)
            idx = jnp.min(jnp.where(scores[g] == mx, nidx_f, float(LANES)), axis=0, keepdims=True)
            hit = nidx_f == idx
            if k == SEL_TOPK - 2:
                hit = hit & (tq < SEL_LEN)
            selms[g] = jnp.where(hit, 1.0, selms[g])
            scores[g] = jnp.where(hit, -jnp.inf, scores[g])
    pad_lo = jnp.zeros((SEL_LANE0, TN), F32)
    pad_hi = jnp.zeros((LANES - SEL_LANE0 - n_sel, TN), F32)
    negsel = [jnp.transpose(jnp.concatenate([pad_lo, (sm - 1.0) * MASK_BIG, pad_hi], axis=0)) for sm in selms]
    lhs = [[(base[h] + negsel[h // hg]).astype(BF16) for h in range(nh)],
           [bh.astype(BF16) for bh in base]]

    back = WINDOW // TN

    def near_post(n_tiles, br, h, s):
        chunks = []
        for c in range(n_tiles):
            r, sc = n_tiles - 1 - c, s[:, c * TN:(c + 1) * TN]
            if r == 0:
                sc = jnp.where(causal, sc + tzd_ref[h, 0], NEG)
            elif r == 1:
                sc = sc + tzd_ref[h, 1]
            elif br == 1:
                sc = jnp.where(inside, sc, NEG)
            chunks.append(sc)
        return jnp.concatenate(chunks, axis=1)

    def near_jobs(n_tiles):
        off = pl.multiple_of(jnp.maximum(i - (n_tiles - 1), 0) * TN, TN)
        return [job for br in (1, 0)
                for job in tile_jobs((br,), off, functools.partial(near_post, n_tiles, br), n_tiles * TN)]

    for n_tiles in range(1, back + 2):
        @pl.when((i == n_tiles - 1) if n_tiles <= back else (i >= back))
        def _():
            process(near_jobs(n_tiles), True)

    n_far = jnp.maximum(i - back, 0)

    def sel_body(jj, carry):
        process(tile_jobs((0,), pl.multiple_of(2 * jj * TN, TN), lambda h, s: s, 2 * TN), False)
        return carry

    lax.fori_loop(0, n_far // 2, sel_body, 0)

    @pl.when(n_far % 2 == 1)
    def _():
        process(tile_jobs((0,), pl.multiple_of(jnp.maximum(n_far - 1, 0) * TN, TN), lambda h, s: s), False)

    width = hg * HEAD_DIM
    for g in range(ng):
        gates = 0.5 * jnp.tanh(0.5 * _dot(g_ref[0], gexp_ref[g])) + 0.5
        for c in range(hg // 2):
            g_cmp, g_sel, g_win = (gates[:, br * width + c * LANES:br * width + (c + 1) * LANES] for br in range(3))
            he, ho = g * hg + 2 * c, g * hg + 2 * c + 1
            out = (g_cmp * jnp.where(lo, o_cmp[g][hrows(2 * c)], o_cmp[g][hrows(2 * c + 1)])
                   + g_sel * pair_out(0, he, ho) + g_win * pair_out(1, he, ho))
            o_ref[0, :, (he // 2) * LANES:(he // 2 + 1) * LANES] = out.astype(BF16)


def _nsa(p3, kvc, bias_c, tzd, ovt, qaux, kaux, gexp):
    b, t, _ = p3.shape
    nq = t // TN
    width = NSA_HEADS * HEAD_DIM
    seq = lambda col: pl.BlockSpec((1, t, LANES), lambda bi, i, col=col: (bi, 0, col))
    const = lambda a: pl.BlockSpec(a.shape, lambda bi, i: (0,) * a.ndim, pipeline_mode=pl.Buffered(1))
    return pl.pallas_call(
        _nsa_kernel,
        grid=(b, nq),
        in_specs=[pl.BlockSpec((1, TN, width), lambda bi, i: (bi, i, _C_QN * LANES // width)),
                  seq(_C_KS), seq(_C_VS), seq(_C_KW), seq(_C_VW),
                  pl.BlockSpec((1, TN, LANES), lambda bi, i: (bi, i, _C_GATE)),
                  const(gexp),
                  pl.BlockSpec((1,) + kvc.shape[1:], lambda bi, i: (bi, 0, 0, 0)),
                  pl.BlockSpec((NSA_HEADS, 1, TN, LANES), lambda bi, i: (0, i, 0, 0)),
                  const(tzd), const(ovt), const(qaux), const(kaux)],
        out_specs=pl.BlockSpec((1, TN, width), lambda bi, i: (bi, i, 0)),
        out_shape=jax.ShapeDtypeStruct((b, t, width), BF16),
        scratch_shapes=[pltpu.VMEM((NSA_KV_HEADS, t, LANES), BF16)] * 4
                       + [pltpu.VMEM((2, NSA_HEADS, TN, LANES), F32)] * 2,
        compiler_params=pltpu.CompilerParams(dimension_semantics=("parallel", "arbitrary"),
                                             vmem_limit_bytes=VMEM_LIMIT),
        name="nsa",
    )(p3, p3, p3, p3, p3, p3, gexp, kvc, bias_c, tzd, ovt, qaux, kaux)


def _sb_kernel(q_ref, k_ref, v_ref, u_ref, o_ref, acc_ref, carry_ref):
    i = pl.program_id(1)
    nb = q_ref.shape[0]
    chains = [(bb, c) for bb in range(nb) for c in range(SB_HEADS // 2)]
    cols = [slice(c * LANES, (c + 1) * LANES) for c in range(SB_HEADS // 2)]
    lane = lax.broadcasted_iota(jnp.int32, (TS, LANES), 1)
    lo = lane < HEAD_DIM

    qs = []
    for bb, c in chains:
        q = q_ref[bb, :, cols[c]].astype(F32)
        qs.append(jnp.concatenate([jnp.where(lo, q, 0.0), jnp.where(lo, 0.0, q)], axis=0).astype(BF16))

    def tile(j):
        off = pl.multiple_of(j * TS, TS)
        zs = [_dot_nt(qs[n], k_ref[bb, pl.ds(off, TS), cols[c]]) for n, (bb, c) in enumerate(chains)]
        log_betas, log_keeps, k_bfs = [], [], []
        for z in zs:
            z_neg = jnp.minimum(z, 0.0)
            log_beta = z_neg - jnp.log(1.0 + jnp.exp2((z_neg + z_neg) - z)) * LOG2E
            log_keep = log_beta - z
            log_betas.append(log_beta)
            log_keeps.append(log_keep)
            k_bfs.append(log_keep.astype(BF16))
        laters = [_dot(k_bf, u_ref[...]) for k_bf in k_bfs]
        a_s = []
        for n, (bb, c) in enumerate(chains):
            carry = carry_ref[bb, c]
            a = jnp.exp2(log_betas[n] + laters[n] + jnp.concatenate([carry] * (TS // LANES), axis=1))
            a_s.append(a.astype(BF16))
            carry_ref[bb, c] = carry + jnp.sum(log_keeps[n], axis=1, keepdims=True)
        for n, (bb, c) in enumerate(chains):
            acc_ref[bb, c] += _dot(a_s[n], v_ref[bb, pl.ds(off, TS), cols[c]])

    def diag_tile():
        off = pl.multiple_of(i * TS, TS)
        half = TS // 2
        mask_full = (lax.broadcasted_iota(jnp.int32, (TS, TS), 1)
                     < (lax.broadcasted_iota(jnp.int32, (TS, TS), 0) & (half - 1)) + half)
        mask_half = (lax.broadcasted_iota(jnp.int32, (TS, half), 1)
                     < (lax.broadcasted_iota(jnp.int32, (TS, half), 0) & (half - 1)))
        pieces = []
        for n in range(len(chains)):
            pieces.append((n, (slice(half, TS), slice(TS + half, 2 * TS)), TS, mask_full))
            pieces.append((n, (slice(0, half), slice(TS, TS + half)), half, mask_half))
        zs = []
        for n, rows, nk, _ in pieces:
            bb, c = chains[n]
            q = jnp.concatenate([qs[n][rows[0]], qs[n][rows[1]]], axis=0)
            zs.append(_dot_nt(q, k_ref[bb, pl.ds(off, nk), cols[c]]))
        log_betas, log_keeps, k_bfs = [], [], []
        for z, (_, _, _, mask) in zip(zs, pieces):
            z_neg = jnp.minimum(z, 0.0)
            log_beta = z_neg - jnp.log(1.0 + jnp.exp2((z_neg + z_neg) - z)) * LOG2E
            log_keep = jnp.where(mask, log_beta - z, 0.0)
            log_betas.append(log_beta)
            log_keeps.append(log_keep)
            k_bfs.append(log_keep.astype(BF16))
        laters = [_dot(k_bf, u_ref[:nk, :nk]) for k_bf, (_, _, nk, _) in zip(k_bfs, pieces)]
        a_s = []
        for w, (n, rows, nk, mask) in enumerate(pieces):
            bb, c = chains[n]
            a_s.append(jnp.where(mask, jnp.exp2(log_betas[w] + laters[w]), 0.0).astype(BF16))
            for rs, part in zip(rows, (log_keeps[w][:half], log_keeps[w][half:])):
                carry_ref[bb, c, rs] = jnp.broadcast_to(jnp.sum(part, axis=1, keepdims=True), (half, LANES))
        for w, (n, rows, nk, _) in enumerate(pieces):
            bb, c = chains[n]
            pv = _dot(a_s[w], v_ref[bb, pl.ds(off, nk), cols[c]])
            acc_ref[bb, c, rows[0]] = pv[:half]
            acc_ref[bb, c, rows[1]] = pv[half:]

    diag_tile()

    def body(n, carry):
        tile(i - 1 - n)
        return carry

    lax.fori_loop(0, i, body, 0)
    for bb, c in chains:
        o = acc_ref[bb, c]
        o_ref[bb, :, cols[c]] = jnp.where(lo, o[:TS], o[TS:]).astype(BF16)


def _sb(p3, u):
    b, t, _ = p3.shape
    width = SB_HEADS * HEAD_DIM
    nb = 2 if b % 2 == 0 else 1
    return pl.pallas_call(
        _sb_kernel,
        grid=(b // nb, t // TS),
        in_specs=[pl.BlockSpec((nb, TS, width), lambda bi, i: (bi, i, _C_SBQ * LANES // width)),
                  pl.BlockSpec((nb, t, width), lambda bi, i: (bi, 0, _C_SBK * LANES // width)),
                  pl.BlockSpec((nb, t, width), lambda bi, i: (bi, 0, _C_SBV * LANES // width)),
                  pl.BlockSpec((TS, TS), lambda bi, i: (0, 0))],
        out_specs=pl.BlockSpec((nb, TS, width), lambda bi, i: (bi, i, 0)),
        out_shape=jax.ShapeDtypeStruct((b, t, width), BF16),
        scratch_shapes=[pltpu.VMEM((nb, SB_HEADS // 2, 2 * TS, LANES), F32),
                        pltpu.VMEM((nb, SB_HEADS // 2, 2 * TS, LANES), F32)],
        compiler_params=pltpu.CompilerParams(dimension_semantics=("parallel", "arbitrary"),
                                             vmem_limit_bytes=VMEM_LIMIT),
        name="sb",
    )(p3, p3, p3, u)


def _tail_kernel(x_ref, on_ref, os_ref, ga_ref, wgm_ref, wun_ref, wus_ref, wo_ref, gm_ref,
                 w1_ref, w2_ref, gf_ref, o_ref, *, last_layer):
    x = x_ref[...]
    d = x.shape[1]
    h = _rms(x, ga_ref[...]).astype(BF16)
    gm = 1.0 / (1.0 + jnp.exp(-_dot_nt(h, wgm_ref[...])))
    mixed = gm[:, :d] * _dot(on_ref[...], wun_ref[...]) + gm[:, d:] * _dot(os_ref[...], wus_ref[...])
    x1 = x + _dot(mixed.astype(BF16), wo_ref[...])
    h2 = _rms(x1, gm_ref[...]).astype(BF16)
    dff = w1_ref.shape[1]
    acc = jnp.zeros_like(x1)
    for f in range(0, dff, FF_CHUNK):
        u = jnp.maximum(_dot(h2, w1_ref[:, f:f + FF_CHUNK]), 0.0)
        acc = acc + _dot((u * u).astype(BF16), w2_ref[f:f + FF_CHUNK, :])
    x2 = x1 + acc
    o_ref[...] = _rms(x2, gf_ref[...]) if last_layer else x2


def _tail(x2, o_nsa, o_sb, g_attn, wgm, wun, wus, wo, g_mlp, w1, w2, g_final, last_layer):
    m, d = x2.shape
    tm = ROWS_TAIL
    tok = lambda w: pl.BlockSpec((tm, w), lambda i: (i, 0))
    const = lambda a: pl.BlockSpec(a.shape, lambda i: (0, 0), pipeline_mode=pl.Buffered(1))
    return pl.pallas_call(
        functools.partial(_tail_kernel, last_layer=last_layer),
        grid=(m // tm,),
        in_specs=[tok(d), tok(o_nsa.shape[1]), tok(o_sb.shape[1]), const(g_attn), const(wgm),
                  const(wun), const(wus), const(wo), const(g_mlp), const(w1), const(w2), const(g_final)],
        out_specs=tok(d),
        out_shape=jax.ShapeDtypeStruct((m, d), F32),
        compiler_params=pltpu.CompilerParams(dimension_semantics=("parallel",),
                                             vmem_limit_bytes=VMEM_LIMIT),
        name="tail",
    )(x2, o_nsa, o_sb, g_attn, wgm, wun, wus, wo, g_mlp, w1, w2, g_final)


def _t5_bucket_np(dist):
    n = np.maximum(dist, 0)
    max_exact = N_BUCKETS // 2
    nf = np.maximum(n, 1).astype(np.float32)
    large = max_exact + (np.log(nf / np.float32(max_exact)) / np.float32(math.log(MAX_DISTANCE / max_exact))
                         * np.float32(N_BUCKETS - max_exact)).astype(np.int32)
    return np.where(n < max_exact, n, np.minimum(large, N_BUCKETS - 1)).astype(np.int32)


def _bias_tables(rel_bias, t):
    tbl = rel_bias * LOG2E
    far_idx = int(_t5_bucket_np(np.asarray(MAX_DISTANCE)))
    far = tbl[far_idx]

    def lookup(idx, table):
        onehot = jnp.asarray(idx.reshape(-1)[:, None] == np.arange(N_BUCKETS)[None, :], BF16)
        out = jnp.einsum("bh,nb->hn", table, onehot, precision=lax.Precision.HIGHEST)
        return out.reshape((table.shape[1],) + idx.shape)

    qi, kj = np.arange(TN)[:, None], np.arange(TN)[None, :]
    tzd = lookup(np.stack([_t5_bucket_np(r * TN + qi - kj) for r in range(2)]), tbl - far[None, :])
    dist_c = (np.arange(t).reshape(t // TN, TN, 1) - CMP_STRIDE * np.arange(LANES)[None, None, :]
              - (CMP_LEN - 1))
    bias_c = lookup(np.where(dist_c >= 0, _t5_bucket_np(dist_c), far_idx), tbl)
    far_hi = far.astype(BF16).astype(F32)
    qaux = jnp.zeros((rel_bias.shape[1], 1, LANES), F32)
    qaux = qaux.at[:, 0, BIAS_LANE].set(far_hi).at[:, 0, BIAS_LANE + 1].set(far - far_hi)
    return tzd, bias_c, qaux


def _const_tables(t):
    c_start = np.arange(LANES) * CMP_STRIDE
    s_start = np.arange(LANES) * SEL_LEN
    n_cmp = (t - CMP_LEN) // CMP_STRIDE + 1
    n_sel = t // SEL_LEN
    ov = ((c_start[:, None] < s_start[None, :] + SEL_LEN) & (c_start[:, None] + CMP_LEN > s_start[None, :])
          & (np.arange(LANES)[:, None] < n_cmp) & (np.arange(LANES)[None, :] < n_sel))
    kaux = np.zeros((2, t, LANES), np.float32)
    kaux[0, np.arange(t), SEL_LANE0 + np.arange(t) // SEL_LEN] = 1.0
    kaux[:, :, BIAS_LANE:BIAS_LANE + 2] = 1.0
    u = np.arange(TS)[:, None] > np.arange(TS)[None, :]
    gexp = np.zeros((NSA_KV_HEADS, LANES, 3 * NSA_GROUP * HEAD_DIM), np.float32)
    for h in range(NSA_HEADS):
        for br in range(3):
            c0 = (br * NSA_GROUP + h % NSA_GROUP) * HEAD_DIM
            gexp[h // NSA_GROUP, h * 3 + br, c0:c0 + HEAD_DIM] = 1.0
    return jnp.asarray(ov.T, BF16), jnp.asarray(kaux, BF16), jnp.asarray(u, BF16), jnp.asarray(gexp, BF16)


def _arrange_kernel(w_ref, o_ref, gm_ref):
    nq, nkv, ng = NSA_HEADS * HEAD_DIM, 6 * NSA_KV_HEADS * HEAD_DIM, 3 * NSA_HEADS
    nsb = SB_HEADS * HEAD_DIM
    kvw = NSA_KV_HEADS * HEAD_DIM
    scale = HEAD_DIM ** -0.5 * LOG2E
    sb0 = nq + nkv + ng
    pieces = [(sb0, nsb, scale), (sb0 + nsb, 2 * nsb, None), (0, nq, scale),
              (nq + 2 * kvw, 4 * kvw, None), (nq + nkv, ng, None), (nq, 2 * kvw, None)]
    row = 0
    for src, height, mul in pieces:
        x = w_ref[src:src + height, :]
        if mul is not None:
            x = x * mul
        if height % LANES:
            pad = LANES - height % LANES
            x = jnp.concatenate([x, jnp.zeros((pad, x.shape[1]), F32)], axis=0)
            height += pad
        o_ref[row:row + height, :] = x.astype(BF16)
        row += height
    gm_ref[...] = w_ref[sb0 + 3 * nsb:, :].astype(BF16)


def _arrange_w_in(wt):
    n_in, d = wt.shape
    n_gm = 2 * d
    cols = 256
    assert NSA_KV_HEADS * HEAD_DIM == LANES and 3 * NSA_HEADS <= LANES
    return pl.pallas_call(
        _arrange_kernel,
        grid=(d // cols,),
        in_specs=[pl.BlockSpec((n_in, cols), lambda i: (0, i))],
        out_specs=[pl.BlockSpec((_N_CHUNKS * LANES, cols), lambda i: (0, i)),
                   pl.BlockSpec((n_gm, cols), lambda i: (0, i))],
        out_shape=[jax.ShapeDtypeStruct((_N_CHUNKS * LANES, d), BF16), jax.ShapeDtypeStruct((n_gm, d), BF16)],
        compiler_params=pltpu.CompilerParams(dimension_semantics=("parallel",),
                                             vmem_limit_bytes=VMEM_LIMIT),
        name="arrange",
    )(wt)


def kernel(x, norm_attn, w_in, rel_bias, cmp_k_pe, cmp_k_w1, cmp_k_b1, cmp_k_w2, cmp_v_pe, cmp_v_w1, cmp_v_b1, cmp_v_w2, w_up_nsa, w_up_sb, w_out, norm_mlp, w_ff1, w_ff2, norm_final):
    b, t, d = x.shape
    depth = w_in.shape[0]
    assert t % TN == 0 and t % TS == 0 and WINDOW == 2 * TN and MAX_DISTANCE <= TN
    assert t // SEL_LEN <= N_BUCKETS and (t - CMP_LEN) // CMP_STRIDE + 1 < LANES
    assert (b * t) % ROWS_INPROJ == 0 and (b * t) % ROWS_TAIL == 0
    tzd, bias_c, qaux = _bias_tables(rel_bias, t)
    ovt, kaux, u, gexp = _const_tables(t)
    half = CMP_LEN // 2 * HEAD_DIM
    xf = x.reshape(b * t, d)
    for layer in range(depth):
        w_arr, w_gm = _arrange_w_in(jnp.transpose(w_in[layer]))
        proj, kvc = _inproj(xf, norm_attn[layer][None, :], w_arr)
        proj = proj.reshape(b, t, -1)

        kvc = kvc.reshape(b, t, -1)
        pe = jnp.stack([cmp_k_pe[layer], cmp_v_pe[layer]]).reshape(2, 1, CMP_LEN * HEAD_DIM)
        w1 = jnp.stack([cmp_k_w1[layer], cmp_v_w1[layer]]).astype(BF16)
        b1 = jnp.stack([cmp_k_b1[layer], cmp_v_b1[layer]])[:, None, :]
        w2 = jnp.stack([cmp_k_w2[layer], cmp_v_w2[layer]])
        w2d = jnp.concatenate([w2, w2], axis=2).astype(BF16)
        kvc = _compress(kvc, pe[:, :, :half], pe[:, :, half:], w1[:, :half], w1[:, half:], b1, w2d)

        o_nsa = _nsa(proj, kvc, bias_c, tzd, ovt, qaux, kaux, gexp)
        o_sb = _sb(proj, u)
        xf = _tail(xf, o_nsa.reshape(b * t, -1), o_sb.reshape(b * t, -1), norm_attn[layer][None, :], w_gm,
                   w_up_nsa[layer].astype(BF16), w_up_sb[layer].astype(BF16), w_out[layer].astype(BF16),
                   norm_mlp[layer][None, :], w_ff1[layer].astype(BF16), w_ff2[layer].astype(BF16),
                   norm_final[None, :], layer == depth - 1)
    return xf.reshape(b, t, d)
```

```python
import functools
import math

import numpy as np
import jax
import jax.numpy as jnp
from jax import lax
from jax.experimental import pallas as pl
from jax.experimental.pallas import tpu as pltpu

HEAD_DIM = 64
NSA_HEADS = 8
NSA_KV_HEADS = 2
NSA_GROUP = NSA_HEADS // NSA_KV_HEADS
SB_HEADS = 8
CMP_LEN = 32
CMP_STRIDE = 16
CMP_HIDDEN = 256
SEL_LEN = 64
SEL_TOPK = 4
WINDOW = 512
N_BUCKETS = 32
MAX_DISTANCE = 128
EPS = 1e-6
NEG = -1e30
FORCE_BONUS = 1e4

LANES = 128
TN = 256
TS = 256
ROWS_INPROJ = 1024
ROWS_TAIL = 512
FF_CHUNK = 512
MXU_COLS = 256
SETUP_ROWS = 512
VMEM_LIMIT = 56 * 1024 * 1024
LOG2E = math.log2(math.e)
MASK_BIG = 2.0 ** 60
SEL_LANE0 = HEAD_DIM
BIAS_LANE = HEAD_DIM + N_BUCKETS

F32 = jnp.float32
BF16 = jnp.bfloat16

_C_SBQ = 0
_C_SBK = 4
_C_SBV = 8
_C_QN = 12
_C_KS = 16
_C_VS = 17
_C_KW = 18
_C_VW = 19
_C_GATE = 20
_C_KVC = 21
_N_CHUNKS = 23


def _dot(a, b):
    return jnp.dot(a, b, preferred_element_type=F32)


def _dot_nt(a, b):
    return lax.dot_general(a, b, (((1,), (1,)), ((), ())), preferred_element_type=F32)


def _rms(x, g):
    return x * lax.rsqrt(jnp.mean(x * x, axis=-1, keepdims=True) + EPS) * g


def _inproj_kernel(x_ref, g_ref, w_ref, o_ref, kvc_ref):
    h = _rms(x_ref[...], g_ref[...]).astype(BF16)
    n = o_ref.shape[1]
    for c in range(0, n, MXU_COLS):
        cs = slice(c, min(c + MXU_COLS, n))
        o_ref[:, cs] = _dot_nt(h, w_ref[cs, :]).astype(BF16)
    kvc_ref[...] = _dot_nt(h, w_ref[n:, :]).astype(BF16)


def _inproj(x2, g, w):
    m, d = x2.shape
    n = _C_KVC * LANES
    nc = w.shape[0] - n
    tm = ROWS_INPROJ
    return pl.pallas_call(
        _inproj_kernel,
        grid=(m // tm,),
        in_specs=[pl.BlockSpec((tm, d), lambda i: (i, 0)),
                  pl.BlockSpec((1, d), lambda i: (0, 0)),
                  pl.BlockSpec(w.shape, lambda i: (0, 0), pipeline_mode=pl.Buffered(1))],
        out_specs=[pl.BlockSpec((tm, n), lambda i: (i, 0)), pl.BlockSpec((tm, nc), lambda i: (i, 0))],
        out_shape=[jax.ShapeDtypeStruct((m, n), BF16), jax.ShapeDtypeStruct((m, nc), BF16)],
        compiler_params=pltpu.CompilerParams(dimension_semantics=("parallel",),
                                             vmem_limit_bytes=VMEM_LIMIT),
        name="inproj",
    )(x2, g, w)


def _gelu_tanh(x):
    return 0.5 * x * (1.0 + jnp.tanh(math.sqrt(2.0 / math.pi) * (x + 0.044715 * (x * x * x))))


def _compress_kernel(x_ref, pea_ref, peb_ref, w1a_ref, w1b_ref, b1_ref, w2_ref, o_ref, xf_ref):
    t = x_ref.shape[1]
    nchunk = t // CMP_STRIDE
    for j in range(x_ref.shape[2] // LANES):
        xf_ref[j] = x_ref[0, :, j * LANES:(j + 1) * LANES].astype(F32)
    lo = lax.broadcasted_iota(jnp.int32, (nchunk, LANES), 1) < HEAD_DIM
    for s in range(4):
        r = s // 2
        tiles = []
        for m in range(CMP_STRIDE // 2):
            a = xf_ref[s // 2, pl.ds(2 * m, nchunk, stride=CMP_STRIDE), :]
            b = xf_ref[s // 2, pl.ds(2 * m + 1, nchunk, stride=CMP_STRIDE), :]
            if s % 2 == 0:
                tiles.append(jnp.where(lo, a, pltpu.roll(b, HEAD_DIM, 1)))
            else:
                tiles.append(jnp.where(lo, pltpu.roll(a, HEAD_DIM, 1), b))
        x = jnp.concatenate(tiles, axis=1)
        a = _dot((x + pea_ref[r]).astype(BF16), w1a_ref[r])
        b = _dot((x + peb_ref[r]).astype(BF16), w1b_ref[r])
        hid = a + pltpu.roll(b, shift=b.shape[0] - 1, axis=0) + b1_ref[r]
        o_ref[0, s] = _dot(_gelu_tanh(hid).astype(BF16), w2_ref[r]).astype(BF16)


def _compress(xc, pea, peb, w1a, w1b, b1, w2d):
    b, t, width = xc.shape
    nchunk = t // CMP_STRIDE
    full = lambda shape: pl.BlockSpec(shape, lambda i: (0,) * len(shape))
    return pl.pallas_call(
        _compress_kernel,
        grid=(b,),
        in_specs=[pl.BlockSpec((1, t, width), lambda i: (i, 0, 0)),
                  full(pea.shape), full(peb.shape), full(w1a.shape), full(w1b.shape),
                  full(b1.shape), full(w2d.shape)],
        out_specs=pl.BlockSpec((1, 4, nchunk, LANES), lambda i: (i, 0, 0, 0)),
        out_shape=jax.ShapeDtypeStruct((b, 4, nchunk, LANES), BF16),
        scratch_shapes=[pltpu.VMEM((width // LANES, t, LANES), F32)],
        compiler_params=pltpu.CompilerParams(dimension_semantics=("parallel",),
                                             vmem_limit_bytes=VMEM_LIMIT),
        name="compress",
    )(xc, pea, peb, w1a, w1b, b1, w2d)


def _nsa_kernel(q_ref, ks_ref, vs_ref, kw_ref, vw_ref, g_ref, gexp_ref, kvc_ref,
                bc_ref, tzd_ref, ovt_ref, qaux_ref, kaux_ref, o_ref,
                ksx_ref, vsx_ref, kwx_ref, vwx_ref, m_ref, acc_ref):
    i = pl.program_id(1)
    ng, hg, nh = NSA_KV_HEADS, NSA_GROUP, NSA_HEADS
    t = ks_ref.shape[1]
    lane = lax.broadcasted_iota(jnp.int32, (TN, LANES), 1)
    row = lax.broadcasted_iota(jnp.int32, (TN, LANES), 0)
    lo = lane < HEAD_DIM
    t_abs = i * TN + row
    krow = lax.broadcasted_iota(jnp.int32, (TN, TN), 0)
    kcol = lax.broadcasted_iota(jnp.int32, (TN, TN), 1)

    def hrows(h):
        return slice(h * TN, (h + 1) * TN)

    @pl.when(i == 0)
    def _():
        ch = SETUP_ROWS
        lo_c = lax.broadcasted_iota(jnp.int32, (ch, LANES), 1) < HEAD_DIM
        for r0 in range(0, t, ch):
            rs = slice(r0, r0 + ch)
            for g in range(ng):
                def own(ref):
                    x = ref[0, rs].astype(F32)
                    return x if g == 0 else pltpu.roll(x, HEAD_DIM, 1)

                ksx_ref[g, rs] = jnp.where(lo_c, own(ks_ref), kaux_ref[0, rs].astype(F32)).astype(BF16)
                kwx_ref[g, rs] = jnp.where(lo_c, own(kw_ref), kaux_ref[1, rs].astype(F32)).astype(BF16)
                vsx_ref[g, rs] = jnp.where(lo_c, own(vs_ref), 1.0).astype(BF16)
                vwx_ref[g, rs] = jnp.where(lo_c, own(vw_ref), 1.0).astype(BF16)

    base = []
    for c in range(nh // 2):
        qc = q_ref[0, :, c * LANES:(c + 1) * LANES].astype(F32)
        for h, qh in ((2 * c, qc), (2 * c + 1, pltpu.roll(qc, HEAD_DIM, 1))):
            base.append(jnp.where(lo, qh, qaux_ref[h]))

    lo_c = lax.broadcasted_iota(jnp.int32, (LANES, LANES), 1) < HEAD_DIM
    valid_c = (t_abs >= CMP_STRIDE * lane + (CMP_LEN - 1)) & (lane < LANES - 1)
    s_cmp = [_dot_nt(jnp.concatenate(base[g * hg:(g + 1) * hg], axis=0).astype(BF16),
                     jnp.where(lo_c, kvc_ref[0, g].astype(F32), 0.0).astype(BF16)) for g in range(ng)]
    ps = []
    for h in range(nh):
        sh = jnp.where(valid_c, s_cmp[h // hg][hrows(h % hg)] + bc_ref[h, 0], NEG)
        e = jnp.exp2(sh - jnp.max(sh, axis=1, keepdims=True))
        ps.append(jnp.where(valid_c, e, 0.0) / jnp.sum(e, axis=1, keepdims=True))
    o_cmp = [_dot(jnp.concatenate(ps[g * hg:(g + 1) * hg], axis=0).astype(BF16), kvc_ref[0, ng + g])
             for g in range(ng)]

    n_sel = t // SEL_LEN
    nidx = lax.broadcasted_iota(jnp.int32, (n_sel, TN), 0)
    tq = i * TN + lax.broadcasted_iota(jnp.int32, (n_sel, TN), 1)
    forced = (nidx == tq // SEL_LEN) | (nidx == 0)
    sel_ok = nidx * SEL_LEN <= tq
    nidx_f = nidx.astype(F32)
    scores, selms = [], []
    for g in range(ng):
        pg = ps[g * hg:(g + 1) * hg]
        psum = (pg[0] + pg[1]) + (pg[2] + pg[3])
        p_hi = psum.astype(BF16)
        p_lo = (psum - p_hi.astype(F32)).astype(BF16)
        imp = (_dot_nt(ovt_ref[...], p_hi) + _dot_nt(ovt_ref[...], p_lo))[:n_sel]
        scores.append(jnp.where(forced, -jnp.inf, jnp.where(sel_ok, imp, -FORCE_BONUS)))
        selms.append(jnp.where(forced, 1.0, 0.0))

    def process(jobs, first):
        ss = [post(_dot_nt(lhs[br][h], kt)) for (br, h, kt, _, post) in jobs]
        ps, alphas = [], []
        for (br, h, _, _, _), sh in zip(jobs, ss):
            mx = jnp.max(sh, axis=1, keepdims=True)
            if first:
                m_new = jnp.broadcast_to(mx, (TN, LANES))
                alphas.append(None)
            else:
                m_old = m_ref[br, h]
                m_new = jnp.maximum(m_old, mx)
                alphas.append(jnp.exp2(m_old - m_new))
            m_ref[br, h] = m_new
            ps.append(jnp.exp2(sh - jnp.concatenate([m_new] * (sh.shape[1] // LANES), axis=1)).astype(BF16))
        for (br, h, _, vt, _), p, alpha in zip(jobs, ps, alphas):
            pv = _dot(p, vt)
            acc_ref[br, h] = pv if first else alpha * acc_ref[br, h] + pv

    def pair_out(br, he, ho):
        acc_e, acc_o = acc_ref[br, he], acc_ref[br, ho]
        num = jnp.where(lo, acc_e, pltpu.roll(acc_o, HEAD_DIM, 1))
        den = jnp.where(lo, pltpu.roll(acc_e, HEAD_DIM, 1), acc_o)
        return num / den

    kxs, vxs = (ksx_ref, kwx_ref), (vsx_ref, vwx_ref)
    causal = kcol <= krow
    inside = kcol > krow

    def tile_jobs(branches, off, post, width=TN):
        return [(br, h, kxs[br][h // hg, pl.ds(off, width), :], vxs[br][h // hg, pl.ds(off, width), :],
                 functools.partial(post, h)) for br in branches for h in range(nh)]

    for k in range(SEL_TOPK - 1):
        for g in range(ng):
            mx = jnp.max(scores[g], axis=0, keepdims=True)
            idx = jnp.min(jnp.where(scores[g] == mx, nidx_f, float(LANES)), axis=0, keepdims=True)
            hit = nidx_f == idx
            if k == SEL_TOPK - 2:
                hit = hit & (tq < SEL_LEN)
            selms[g] = jnp.where(hit, 1.0, selms[g])
            scores[g] = jnp.where(hit, -jnp.inf, scores[g])
    pad_lo = jnp.zeros((SEL_LANE0, TN), F32)
    pad_hi = jnp.zeros((LANES - SEL_LANE0 - n_sel, TN), F32)
    negsel = [jnp.transpose(jnp.concatenate([pad_lo, (sm - 1.0) * MASK_BIG, pad_hi], axis=0)) for sm in selms]
    lhs = [[(base[h] + negsel[h // hg]).astype(BF16) for h in range(nh)],
           [bh.astype(BF16) for bh in base]]

    back = WINDOW // TN

    def near_post(n_tiles, br, h, s):
        chunks = []
        for c in range(n_tiles):
            r, sc = n_tiles - 1 - c, s[:, c * TN:(c + 1) * TN]
            if r == 0:
                sc = jnp.where(causal, sc + tzd_ref[h, 0], NEG)
            elif r == 1:
                sc = sc + tzd_ref[h, 1]
            elif br == 1:
                sc = jnp.where(inside, sc, NEG)
            chunks.append(sc)
        return jnp.concatenate(chunks, axis=1)

    def near_jobs(n_tiles):
        off = pl.multiple_of(jnp.maximum(i - (n_tiles - 1), 0) * TN, TN)
        return [job for br in (1, 0)
                for job in tile_jobs((br,), off, functools.partial(near_post, n_tiles, br), n_tiles * TN)]

    for n_tiles in range(1, back + 2):
        @pl.when((i == n_tiles - 1) if n_tiles <= back else (i >= back))
        def _():
            process(near_jobs(n_tiles), True)

    n_far = jnp.maximum(i - back, 0)

    def sel_body(jj, carry):
        process(tile_jobs((0,), pl.multiple_of(2 * jj * TN, TN), lambda h, s: s, 2 * TN), False)
        return carry

    lax.fori_loop(0, n_far // 2, sel_body, 0)

    @pl.when(n_far % 2 == 1)
    def _():
        process(tile_jobs((0,), pl.multiple_of(jnp.maximum(n_far - 1, 0) * TN, TN), lambda h, s: s), False)

    width = hg * HEAD_DIM
    for g in range(ng):
        gates = 0.5 * jnp.tanh(0.5 * _dot(g_ref[0], gexp_ref[g])) + 0.5
        for c in range(hg // 2):
            g_cmp, g_sel, g_win = (gates[:, br * width + c * LANES:br * width + (c + 1) * LANES] for br in range(3))
            he, ho = g * hg + 2 * c, g * hg + 2 * c + 1
            out = (g_cmp * jnp.where(lo, o_cmp[g][hrows(2 * c)], o_cmp[g][hrows(2 * c + 1)])
                   + g_sel * pair_out(0, he, ho) + g_win * pair_out(1, he, ho))
            o_ref[0, :, (he // 2) * LANES:(he // 2 + 1) * LANES] = out.astype(BF16)


def _nsa(p3, kvc, bias_c, tzd, ovt, qaux, kaux, gexp):
    b, t, _ = p3.shape
    nq = t // TN
    width = NSA_HEADS * HEAD_DIM
    seq = lambda col: pl.BlockSpec((1, t, LANES), lambda bi, i, col=col: (bi, 0, col))
    const = lambda a: pl.BlockSpec(a.shape, lambda bi, i: (0,) * a.ndim, pipeline_mode=pl.Buffered(1))
    return pl.pallas_call(
        _nsa_kernel,
        grid=(b, nq),
        in_specs=[pl.BlockSpec((1, TN, width), lambda bi, i: (bi, i, _C_QN * LANES // width)),
                  seq(_C_KS), seq(_C_VS), seq(_C_KW), seq(_C_VW),
                  pl.BlockSpec((1, TN, LANES), lambda bi, i: (bi, i, _C_GATE)),
                  const(gexp),
                  pl.BlockSpec((1,) + kvc.shape[1:], lambda bi, i: (bi, 0, 0, 0)),
                  pl.BlockSpec((NSA_HEADS, 1, TN, LANES), lambda bi, i: (0, i, 0, 0)),
                  const(tzd), const(ovt), const(qaux), const(kaux)],
        out_specs=pl.BlockSpec((1, TN, width), lambda bi, i: (bi, i, 0)),
        out_shape=jax.ShapeDtypeStruct((b, t, width), BF16),
        scratch_shapes=[pltpu.VMEM((NSA_KV_HEADS, t, LANES), BF16)] * 4
                       + [pltpu.VMEM((2, NSA_HEADS, TN, LANES), F32)] * 2,
        compiler_params=pltpu.CompilerParams(dimension_semantics=("parallel", "arbitrary"),
                                             vmem_limit_bytes=VMEM_LIMIT),
        name="nsa",
    )(p3, p3, p3, p3, p3, p3, gexp, kvc, bias_c, tzd, ovt, qaux, kaux)


def _sb_kernel(q_ref, k_ref, v_ref, u_ref, o_ref, acc_ref, carry_ref):
    i = pl.program_id(1)
    nb = q_ref.shape[0]
    chains = [(bb, c) for bb in range(nb) for c in range(SB_HEADS // 2)]
    cols = [slice(c * LANES, (c + 1) * LANES) for c in range(SB_HEADS // 2)]
    lane = lax.broadcasted_iota(jnp.int32, (TS, LANES), 1)
    lo = lane < HEAD_DIM

    qs = []
    for bb, c in chains:
        q = q_ref[bb, :, cols[c]].astype(F32)
        qs.append(jnp.concatenate([jnp.where(lo, q, 0.0), jnp.where(lo, 0.0, q)], axis=0).astype(BF16))

    def tile(j):
        off = pl.multiple_of(j * TS, TS)
        zs = [_dot_nt(qs[n], k_ref[bb, pl.ds(off, TS), cols[c]]) for n, (bb, c) in enumerate(chains)]
        log_betas, log_keeps, k_bfs = [], [], []
        for z in zs:
            z_neg = jnp.minimum(z, 0.0)
            log_beta = z_neg - jnp.log(1.0 + jnp.exp2((z_neg + z_neg) - z)) * LOG2E
            log_keep = log_beta - z
            log_betas.append(log_beta)
            log_keeps.append(log_keep)
            k_bfs.append(log_keep.astype(BF16))
        laters = [_dot(k_bf, u_ref[...]) for k_bf in k_bfs]
        a_s = []
        for n, (bb, c) in enumerate(chains):
            carry = carry_ref[bb, c]
            a = jnp.exp2(log_betas[n] + laters[n] + jnp.concatenate([carry] * (TS // LANES), axis=1))
            a_s.append(a.astype(BF16))
            carry_ref[bb, c] = carry + jnp.sum(log_keeps[n], axis=1, keepdims=True)
        for n, (bb, c) in enumerate(chains):
            acc_ref[bb, c] += _dot(a_s[n], v_ref[bb, pl.ds(off, TS), cols[c]])

    def diag_tile():
        off = pl.multiple_of(i * TS, TS)
        half = TS // 2
        mask_full = (lax.broadcasted_iota(jnp.int32, (TS, TS), 1)
                     < (lax.broadcasted_iota(jnp.int32, (TS, TS), 0) & (half - 1)) + half)
        mask_half = (lax.broadcasted_iota(jnp.int32, (TS, half), 1)
                     < (lax.broadcasted_iota(jnp.int32, (TS, half), 0) & (half - 1)))
        pieces = []
        for n in range(len(chains)):
            pieces.append((n, (slice(half, TS), slice(TS + half, 2 * TS)), TS, mask_full))
            pieces.append((n, (slice(0, half), slice(TS, TS + half)), half, mask_half))
        zs = []
        for n, rows, nk, _ in pieces:
            bb, c = chains[n]
            q = jnp.concatenate([qs[n][rows[0]], qs[n][rows[1]]], axis=0)
            zs.append(_dot_nt(q, k_ref[bb, pl.ds(off, nk), cols[c]]))
        log_betas, log_keeps, k_bfs = [], [], []
        for z, (_, _, _, mask) in zip(zs, pieces):
            z_neg = jnp.minimum(z, 0.0)
            log_beta = z_neg - jnp.log(1.0 + jnp.exp2((z_neg + z_neg) - z)) * LOG2E
            log_keep = jnp.where(mask, log_beta - z, 0.0)
            log_betas.append(log_beta)
            log_keeps.append(log_keep)
            k_bfs.append(log_keep.astype(BF16))
        laters = [_dot(k_bf, u_ref[:nk, :nk]) for k_bf, (_, _, nk, _) in zip(k_bfs, pieces)]
        a_s = []
        for w, (n, rows, nk, mask) in enumerate(pieces):
            bb, c = chains[n]
            a_s.append(jnp.where(mask, jnp.exp2(log_betas[w] + laters[w]), 0.0).astype(BF16))
            for rs, part in zip(rows, (log_keeps[w][:half], log_keeps[w][half:])):
                carry_ref[bb, c, rs] = jnp.broadcast_to(jnp.sum(part, axis=1, keepdims=True), (half, LANES))
        for w, (n, rows, nk, _) in enumerate(pieces):
            bb, c = chains[n]
            pv = _dot(a_s[w], v_ref[bb, pl.ds(off, nk), cols[c]])
            acc_ref[bb, c, rows[0]] = pv[:half]
            acc_ref[bb, c, rows[1]] = pv[half:]

    diag_tile()

    def body(n, carry):
        tile(i - 1 - n)
        return carry

    lax.fori_loop(0, i, body, 0)
    for bb, c in chains:
        o = acc_ref[bb, c]
        o_ref[bb, :, cols[c]] = jnp.where(lo, o[:TS], o[TS:]).astype(BF16)


def _sb(p3, u):
    b, t, _ = p3.shape
    width = SB_HEADS * HEAD_DIM
    nb = 2 if b % 2 == 0 else 1
    return pl.pallas_call(
        _sb_kernel,
        grid=(b // nb, t // TS),
        in_specs=[pl.BlockSpec((nb, TS, width), lambda bi, i: (bi, i, _C_SBQ * LANES // width)),
                  pl.BlockSpec((nb, t, width), lambda bi, i: (bi, 0, _C_SBK * LANES // width)),
                  pl.BlockSpec((nb, t, width), lambda bi, i: (bi, 0, _C_SBV * LANES // width)),
                  pl.BlockSpec((TS, TS), lambda bi, i: (0, 0))],
        out_specs=pl.BlockSpec((nb, TS, width), lambda bi, i: (bi, i, 0)),
        out_shape=jax.ShapeDtypeStruct((b, t, width), BF16),
        scratch_shapes=[pltpu.VMEM((nb, SB_HEADS // 2, 2 * TS, LANES), F32),
                        pltpu.VMEM((nb, SB_HEADS // 2, 2 * TS, LANES), F32)],
        compiler_params=pltpu.CompilerParams(dimension_semantics=("parallel", "arbitrary"),
                                             vmem_limit_bytes=VMEM_LIMIT),
        name="sb",
    )(p3, p3, p3, u)


def _tail_kernel(x_ref, on_ref, os_ref, ga_ref, wgm_ref, wun_ref, wus_ref, wo_ref, gm_ref,
                 w1_ref, w2_ref, gf_ref, o_ref, *, last_layer):
    x = x_ref[...]
    d = x.shape[1]
    h = _rms(x, ga_ref[...]).astype(BF16)
    gm = 0.5 * jnp.tanh(0.5 * _dot_nt(h, wgm_ref[...])) + 0.5
    mixed = gm[:, :d] * _dot(on_ref[...], wun_ref[...]) + gm[:, d:] * _dot(os_ref[...], wus_ref[...])
    x1 = x + _dot(mixed.astype(BF16), wo_ref[...])
    h2 = _rms(x1, gm_ref[...]).astype(BF16)
    dff = w1_ref.shape[1]
    acc = jnp.zeros_like(x1)
    for f in range(0, dff, FF_CHUNK):
        u = jnp.maximum(_dot(h2, w1_ref[:, f:f + FF_CHUNK]), 0.0)
        acc = acc + _dot((u * u).astype(BF16), w2_ref[f:f + FF_CHUNK, :])
    x2 = x1 + acc
    o_ref[...] = _rms(x2, gf_ref[...]) if last_layer else x2


def _tail(x2, o_nsa, o_sb, g_attn, wgm, wun, wus, wo, g_mlp, w1, w2, g_final, last_layer):
    m, d = x2.shape
    tm = ROWS_TAIL
    tok = lambda w: pl.BlockSpec((tm, w), lambda i: (i, 0))
    const = lambda a: pl.BlockSpec(a.shape, lambda i: (0, 0), pipeline_mode=pl.Buffered(1))
    return pl.pallas_call(
        functools.partial(_tail_kernel, last_layer=last_layer),
        grid=(m // tm,),
        in_specs=[tok(d), tok(o_nsa.shape[1]), tok(o_sb.shape[1]), const(g_attn), const(wgm),
                  const(wun), const(wus), const(wo), const(g_mlp), const(w1), const(w2), const(g_final)],
        out_specs=tok(d),
        out_shape=jax.ShapeDtypeStruct((m, d), F32),
        compiler_params=pltpu.CompilerParams(dimension_semantics=("parallel",),
                                             vmem_limit_bytes=VMEM_LIMIT),
        name="tail",
    )(x2, o_nsa, o_sb, g_attn, wgm, wun, wus, wo, g_mlp, w1, w2, g_final)


def _t5_bucket_np(dist):
    n = np.maximum(dist, 0)
    max_exact = N_BUCKETS // 2
    nf = np.maximum(n, 1).astype(np.float32)
    large = max_exact + (np.log(nf / np.float32(max_exact)) / np.float32(math.log(MAX_DISTANCE / max_exact))
                         * np.float32(N_BUCKETS - max_exact)).astype(np.int32)
    return np.where(n < max_exact, n, np.minimum(large, N_BUCKETS - 1)).astype(np.int32)


def _bias_tables(rel_bias, t):
    tbl = rel_bias * LOG2E
    far_idx = int(_t5_bucket_np(np.asarray(MAX_DISTANCE)))
    far = tbl[far_idx]

    def lookup(idx, table):
        onehot = jnp.asarray(idx.reshape(-1)[:, None] == np.arange(N_BUCKETS)[None, :], BF16)
        out = jnp.einsum("bh,nb->hn", table, onehot, precision=lax.Precision.HIGHEST)
        return out.reshape((table.shape[1],) + idx.shape)

    qi, kj = np.arange(TN)[:, None], np.arange(TN)[None, :]
    tzd = lookup(np.stack([_t5_bucket_np(r * TN + qi - kj) for r in range(2)]), tbl - far[None, :])
    dist_c = (np.arange(t).reshape(t // TN, TN, 1) - CMP_STRIDE * np.arange(LANES)[None, None, :]
              - (CMP_LEN - 1))
    bias_c = lookup(np.where(dist_c >= 0, _t5_bucket_np(dist_c), far_idx), tbl)
    far_hi = far.astype(BF16).astype(F32)
    qaux = jnp.zeros((rel_bias.shape[1], 1, LANES), F32)
    qaux = qaux.at[:, 0, BIAS_LANE].set(far_hi).at[:, 0, BIAS_LANE + 1].set(far - far_hi)
    return tzd, bias_c, qaux


def _const_tables(t):
    c_start = np.arange(LANES) * CMP_STRIDE
    s_start = np.arange(LANES) * SEL_LEN
    n_cmp = (t - CMP_LEN) // CMP_STRIDE + 1
    n_sel = t // SEL_LEN
    ov = ((c_start[:, None] < s_start[None, :] + SEL_LEN) & (c_start[:, None] + CMP_LEN > s_start[None, :])
          & (np.arange(LANES)[:, None] < n_cmp) & (np.arange(LANES)[None, :] < n_sel))
    kaux = np.zeros((2, t, LANES), np.float32)
    kaux[0, np.arange(t), SEL_LANE0 + np.arange(t) // SEL_LEN] = 1.0
    kaux[:, :, BIAS_LANE:BIAS_LANE + 2] = 1.0
    u = np.arange(TS)[:, None] > np.arange(TS)[None, :]
    gexp = np.zeros((NSA_KV_HEADS, LANES, 3 * NSA_GROUP * HEAD_DIM), np.float32)
    for h in range(NSA_HEADS):
        for br in range(3):
            c0 = (br * NSA_GROUP + h % NSA_GROUP) * HEAD_DIM
            gexp[h // NSA_GROUP, h * 3 + br, c0:c0 + HEAD_DIM] = 1.0
    return jnp.asarray(ov.T, BF16), jnp.asarray(kaux, BF16), jnp.asarray(u, BF16), jnp.asarray(gexp, BF16)


def _arrange_kernel(w_ref, o_ref, gm_ref):
    nq, nkv, ng = NSA_HEADS * HEAD_DIM, 6 * NSA_KV_HEADS * HEAD_DIM, 3 * NSA_HEADS
    nsb = SB_HEADS * HEAD_DIM
    kvw = NSA_KV_HEADS * HEAD_DIM
    scale = HEAD_DIM ** -0.5 * LOG2E
    sb0 = nq + nkv + ng
    pieces = [(sb0, nsb, scale), (sb0 + nsb, 2 * nsb, None), (0, nq, scale),
              (nq + 2 * kvw, 4 * kvw, None), (nq + nkv, ng, None), (nq, 2 * kvw, None)]
    row = 0
    for src, height, mul in pieces:
        x = w_ref[src:src + height, :]
        if mul is not None:
            x = x * mul
        if height % LANES:
            pad = LANES - height % LANES
            x = jnp.concatenate([x, jnp.zeros((pad, x.shape[1]), F32)], axis=0)
            height += pad
        o_ref[row:row + height, :] = x.astype(BF16)
        row += height
    gm_ref[...] = w_ref[sb0 + 3 * nsb:, :].astype(BF16)


def _arrange_w_in(wt):
    n_in, d = wt.shape
    n_gm = 2 * d
    cols = 256
    assert NSA_KV_HEADS * HEAD_DIM == LANES and 3 * NSA_HEADS <= LANES
    return pl.pallas_call(
        _arrange_kernel,
        grid=(d // cols,),
        in_specs=[pl.BlockSpec((n_in, cols), lambda i: (0, i))],
        out_specs=[pl.BlockSpec((_N_CHUNKS * LANES, cols), lambda i: (0, i)),
                   pl.BlockSpec((n_gm, cols), lambda i: (0, i))],
        out_shape=[jax.ShapeDtypeStruct((_N_CHUNKS * LANES, d), BF16), jax.ShapeDtypeStruct((n_gm, d), BF16)],
        compiler_params=pltpu.CompilerParams(dimension_semantics=("parallel",),
                                             vmem_limit_bytes=VMEM_LIMIT),
        name="arrange",
    )(wt)


def kernel(x, norm_attn, w_in, rel_bias, cmp_k_pe, cmp_k_w1, cmp_k_b1, cmp_k_w2, cmp_v_pe, cmp_v_w1, cmp_v_b1, cmp_v_w2, w_up_nsa, w_up_sb, w_out, norm_mlp, w_ff1, w_ff2, norm_final):
    b, t, d = x.shape
    depth = w_in.shape[0]
    assert t % TN == 0 and t % TS == 0 and WINDOW == 2 * TN and MAX_DISTANCE <= TN
    assert t // SEL_LEN <= N_BUCKETS and (t - CMP_LEN) // CMP_STRIDE + 1 < LANES
    assert (b * t) % ROWS_INPROJ == 0 and (b * t) % ROWS_TAIL == 0
    tzd, bias_c, qaux = _bias_tables(rel_bias, t)
    ovt, kaux, u, gexp = _const_tables(t)
    half = CMP_LEN // 2 * HEAD_DIM
    xf = x.reshape(b * t, d)
    for layer in range(depth):
        w_arr, w_gm = _arrange_w_in(jnp.transpose(w_in[layer]))
        proj, kvc = _inproj(xf, norm_attn[layer][None, :], w_arr)
        proj = proj.reshape(b, t, -1)

        kvc = kvc.reshape(b, t, -1)
        pe = jnp.stack([cmp_k_pe[layer], cmp_v_pe[layer]]).reshape(2, 1, CMP_LEN * HEAD_DIM)
        w1 = jnp.stack([cmp_k_w1[layer], cmp_v_w1[layer]]).astype(BF16)
        b1 = jnp.stack([cmp_k_b1[layer], cmp_v_b1[layer]])[:, None, :]
        w2 = jnp.stack([cmp_k_w2[layer], cmp_v_w2[layer]])
        w2d = jnp.concatenate([w2, w2], axis=2).astype(BF16)
        kvc = _compress(kvc, pe[:, :, :half], pe[:, :, half:], w1[:, :half], w1[:, half:], b1, w2d)

        o_nsa = _nsa(proj, kvc, bias_c, tzd, ovt, qaux, kaux, gexp)
        o_sb = _sb(proj, u)
        xf = _tail(xf, o_nsa.reshape(b * t, -1), o_sb.reshape(b * t, -1), norm_attn[layer][None, :], w_gm,
                   w_up_nsa[layer].astype(BF16), w_up_sb[layer].astype(BF16), w_out[layer].astype(BF16),
                   norm_mlp[layer][None, :], w_ff1[layer].astype(BF16), w_ff2[layer].astype(BF16),
                   norm_final[None, :], layer == depth - 1)
    return xf.reshape(b, t, d)
```
